```python
import math
import jax, jax.numpy as jnp
from jax import lax
import numpy as np

D_MODEL = 2048
BATCH = 4
SEQ = 8192
DEPTH = 1
DEC_BATCH = 8
DEC_SEQ = 16
PAST_LEN = 2048

CHUNK = 64
Q_BLOCK = 128
D_ATTN = 1024
N_HEADS = 8
N_KV_HEADS = 2
HEAD_DIM = 128
KV_DIM = N_KV_HEADS * HEAD_DIM
N_IDX_HEADS = 16
IDX_DIM = 64
TOPK_MAX = 256
D_SSM = 1024
SSM_GROUP = 16
N_SSM_GROUPS = D_SSM // SSM_GROUP
SSM_STATE = 64
D_MIX = D_ATTN + D_SSM
RMS_EPS = 1e-6
IN_SIZES = (D_ATTN, KV_DIM, KV_DIM, N_IDX_HEADS * IDX_DIM, IDX_DIM, N_IDX_HEADS, D_ATTN, D_SSM, D_SSM)
IN_COLS = sum(IN_SIZES)

kernel_name = "hymba_dsa_s5_streaming_step"


def _rmsnorm(x, g):
    xf = x.astype(jnp.float32)
    y = xf * lax.rsqrt(jnp.mean(xf * xf, axis=-1, keepdims=True) + RMS_EPS)
    return (y * g.astype(jnp.float32)).astype(x.dtype)


def _split_cols(z):
    offs = []
    acc = 0
    for s in IN_SIZES[:-1]:
        acc += s
        offs.append(acc)
    return jnp.split(z, offs, axis=-1)


def _dsa_block(q, iq, iw, k, v, ik, limit, topk):
    f32 = jnp.float32
    bsz, nq = q.shape[:2]
    n_keys = k.shape[1]
    s = jnp.einsum('bqhd,bsd->bqhs', iq.astype(f32), ik.astype(f32)) * (IDX_DIM ** -0.5)
    score = jnp.einsum('bqhs,bqh->bqs', jax.nn.relu(s), iw.astype(f32))
    visible = jnp.arange(n_keys, dtype=jnp.int32)[None, :] < limit[:, None]
    score = jnp.where(visible[None], score, -jnp.inf)
    _, idx = lax.top_k(score, topk)
    valid = idx < limit[None, :, None]
    kg = jax.vmap(lambda a, i: a[i])(k, idx)
    vg = jax.vmap(lambda a, i: a[i])(v, idx)
    qg = q.reshape(bsz, nq, N_KV_HEADS, N_HEADS // N_KV_HEADS, HEAD_DIM).astype(f32)
    logits = jnp.einsum('bqhgd,bqnhd->bqhgn', qg, kg.astype(f32)) * (HEAD_DIM ** -0.5)
    logits = jnp.where(valid[:, :, None, None, :], logits, -jnp.inf)
    p = jax.nn.softmax(logits, axis=-1)
    o = jnp.einsum('bqhgn,bqnhd->bqhgd', p, vg.astype(f32))
    return o.reshape(bsz, nq, D_ATTN).astype(q.dtype)


def _dsa_prompt(q, k, v, iq, ik, iw):
    bsz, s = q.shape[:2]
    nb = s // Q_BLOCK
    topk = min(TOPK_MAX, s // 4)
    lim = (((jnp.arange(s, dtype=jnp.int32) // CHUNK) + 1) * CHUNK).reshape(nb, Q_BLOCK)

    def blocks(a):
        return jnp.moveaxis(a.reshape((bsz, nb, Q_BLOCK) + a.shape[2:]), 1, 0)

    def one(args):
        qb, iqb, iwb, lb = args
        return _dsa_block(qb, iqb, iwb, k, v, ik, lb, topk)

    out = lax.map(one, (blocks(q), blocks(iq), blocks(iw), lim))
    return jnp.moveaxis(out, 0, 1).reshape(bsz, s, D_ATTN)


def _s5(u, h0, lam_re, lam_im, log_dt, b_re, b_im, c_re, c_im, d_skip):
    f32 = jnp.float32
    bsz, s, _ = u.shape
    t_len = CHUNK if s % CHUNK == 0 else s
    n_chunks = s // t_len
    lam = lax.complex(lam_re.astype(f32), lam_im.astype(f32))
    lam_dt = lam * jnp.exp(log_dt.astype(f32))[:, None]
    lam_bar = jnp.exp(lam_dt)
    b_bar = ((lam_bar - 1.0) / lam)[..., None] * lax.complex(b_re.astype(f32), b_im.astype(f32))
    decay = jnp.exp(lam_dt[None] * jnp.arange(1, t_len + 1, dtype=f32)[:, None, None])
    cr = c_re.astype(f32)
    ci = c_im.astype(f32)
    uc = jnp.moveaxis(u.astype(f32).reshape(bsz, n_chunks, t_len, N_SSM_GROUPS, SSM_GROUP), 1, 0)

    def combine(l, r):
        a_l, b_l = l
        a_r, b_r = r
        return a_r * a_l, a_r * b_l + b_r

    def step(h, u_blk):
        bu = jnp.einsum('gpc,btgc->btgp', b_bar, u_blk.astype(jnp.complex64))
        a = jnp.broadcast_to(lam_bar, bu.shape)
        _, hs = lax.associative_scan(combine, (a, bu), axis=1)
        hs = hs + decay[None] * h[:, None]
        y = jnp.einsum('gcp,btgp->btgc', cr, hs.real) - jnp.einsum('gcp,btgp->btgc', ci, hs.imag)
        return hs[:, -1], y

    h_last, ys = lax.scan(step, h0, uc)
    y = jnp.moveaxis(ys, 0, 1).reshape(bsz, s, D_SSM) + d_skip.astype(f32) * u.astype(f32)
    return y, h_last


def _layer(x, c, past, w_ada, b_ada, g_pre, g_post, w_in, lam_re, lam_im, log_dt,
           b_re, b_im, c_re, c_im, d_skip, w_glu, b_glu, w_out):
    bsz, s, _ = x.shape
    mod = c @ w_ada + b_ada
    shift, scale, gate = jnp.split(mod[:, None, :], 3, axis=-1)
    h = _rmsnorm(x, g_pre) * (1.0 + scale) + shift
    z = h @ w_in
    q, k, v, iq, ik, iw, za, u, zs = _split_cols(z)
    q = q.reshape(bsz, s, N_HEADS, HEAD_DIM)
    k = k.reshape(bsz, s, N_KV_HEADS, HEAD_DIM)
    v = v.reshape(bsz, s, N_KV_HEADS, HEAD_DIM)
    iq = iq.reshape(bsz, s, N_IDX_HEADS, IDX_DIM)
    iw = iw * (N_IDX_HEADS ** -0.5)
    if past is None:
        attn = _dsa_prompt(q, k, v, iq, ik, iw)
        h0 = jnp.zeros((bsz, N_SSM_GROUPS, SSM_STATE), jnp.complex64)
    else:
        ck, cv, cik, sre, sim = past
        k_all = jnp.concatenate([ck.astype(k.dtype), k], axis=1)
        v_all = jnp.concatenate([cv.astype(v.dtype), v], axis=1)
        ik_all = jnp.concatenate([cik.astype(ik.dtype), ik], axis=1)
        n_keys = k_all.shape[1]
        topk = min(TOPK_MAX, n_keys // 4)
        lim = jnp.full((s,), n_keys, dtype=jnp.int32)
        attn = _dsa_block(q, iq, iw, k_all, v_all, ik_all, lim, topk)
        h0 = lax.complex(sre.astype(jnp.float32), sim.astype(jnp.float32))
    attn = attn * jax.nn.silu(za)
    y_ssm, h_last = _s5(u, h0, lam_re, lam_im, log_dt, b_re, b_im, c_re, c_im, d_skip)
    y_ssm = jax.nn.gelu(y_ssm.astype(x.dtype))
    y_ssm = y_ssm * jax.nn.sigmoid(y_ssm @ w_glu + b_glu)
    y_ssm = y_ssm * jax.nn.silu(zs)
    out = jnp.concatenate([attn, y_ssm], axis=-1) @ w_out
    y = x + gate * _rmsnorm(out, g_post)
    return y, (k, v, ik, h_last.real.astype(x.dtype), h_last.imag.astype(x.dtype))


def setup_inputs(seed: int = 0) -> dict:
    key = jax.random.key(seed)
    ks = jax.random.split(key, 32)
    f32 = jnp.float32

    def nrm(k, shape, sd):
        return sd * jax.random.normal(k, shape, f32)

    G, P, Cg = N_SSM_GROUPS, SSM_STATE, SSM_GROUP
    b_ada = jnp.concatenate([nrm(ks[20], (DEPTH, 2 * D_MODEL), 0.02),
                             1.0 + nrm(ks[21], (DEPTH, D_MODEL), 0.02)], axis=-1)
    return {
        "x_prompt": nrm(ks[0], (BATCH, SEQ, D_MODEL), 1.0),
        "x_sample": nrm(ks[1], (DEC_BATCH, DEC_SEQ, D_MODEL), 1.0),
        "c_prompt": nrm(ks[2], (BATCH, D_MODEL), 1.0),
        "c_sample": nrm(ks[3], (DEC_BATCH, D_MODEL), 1.0),
        "cache_k": nrm(ks[4], (DEPTH, DEC_BATCH, PAST_LEN, N_KV_HEADS, HEAD_DIM), 1.0),
        "cache_v": nrm(ks[5], (DEPTH, DEC_BATCH, PAST_LEN, N_KV_HEADS, HEAD_DIM), 1.0),
        "cache_idx_k": nrm(ks[6], (DEPTH, DEC_BATCH, PAST_LEN, IDX_DIM), 1.0),
        "state_ssm_re": nrm(ks[7], (DEPTH, DEC_BATCH, G, P), 0.1),
        "state_ssm_im": nrm(ks[8], (DEPTH, DEC_BATCH, G, P), 0.1),
        "w_ada": nrm(ks[9], (DEPTH, D_MODEL, 3 * D_MODEL), 0.1 * D_MODEL ** -0.5),
        "b_ada": b_ada,
        "g_pre": 1.0 + nrm(ks[10], (DEPTH, D_MODEL), 0.02),
        "g_post": 1.0 + nrm(ks[11], (DEPTH, D_MODEL), 0.02),
        "w_in": nrm(ks[12], (DEPTH, D_MODEL, IN_COLS), D_MODEL ** -0.5),
        "lambda_re": -0.5 + nrm(ks[13], (DEPTH, G, P), 0.01),
        "lambda_im": jnp.pi * jnp.arange(P, dtype=f32)[None, None, :] + nrm(ks[14], (DEPTH, G, P), 0.01),
        "log_dt": jax.random.uniform(ks[15], (DEPTH, G), f32, math.log(1e-3), math.log(1e-1)),
        "b_re": nrm(ks[16], (DEPTH, G, P, Cg), (0.5 / Cg) ** 0.5),
        "b_im": nrm(ks[17], (DEPTH, G, P, Cg), (0.5 / Cg) ** 0.5),
        "c_re": nrm(ks[18], (DEPTH, G, Cg, P), (0.5 / P) ** 0.5),
        "c_im": nrm(ks[19], (DEPTH, G, Cg, P), (0.5 / P) ** 0.5),
        "d_skip": nrm(ks[22], (DEPTH, D_SSM), 1.0),
        "w_glu": nrm(ks[23], (DEPTH, D_SSM, D_SSM), D_SSM ** -0.5),
        "b_glu": nrm(ks[24], (DEPTH, D_SSM), 0.01),
        "w_out": nrm(ks[25], (DEPTH, D_MIX, D_MODEL), D_MIX ** -0.5),
    }


def reference(x_prompt, x_sample, c_prompt, c_sample, cache_k, cache_v, cache_idx_k,
              state_ssm_re, state_ssm_im, w_ada, b_ada, g_pre, g_post, w_in,
              lambda_re, lambda_im, log_dt, b_re, b_im, c_re, c_im, d_skip,
              w_glu, b_glu, w_out):
    yp, ys = x_prompt, x_sample
    kp, vp, ikp, srp, sip = [], [], [], [], []
    kd, vd, ikd, srd, sid = [], [], [], [], []
    for l in range(DEPTH):
        lw = (w_ada[l], b_ada[l], g_pre[l], g_post[l], w_in[l], lambda_re[l], lambda_im[l],
              log_dt[l], b_re[l], b_im[l], c_re[l], c_im[l], d_skip[l], w_glu[l], b_glu[l], w_out[l])
        yp, (k1, v1, ik1, r1, i1) = _layer(yp, c_prompt, None, *lw)
        past = (cache_k[l], cache_v[l], cache_idx_k[l], state_ssm_re[l], state_ssm_im[l])
        ys, (k2, v2, ik2, r2, i2) = _layer(ys, c_sample, past, *lw)
        kp.append(k1); vp.append(v1); ikp.append(ik1); srp.append(r1); sip.append(i1)
        kd.append(k2); vd.append(v2); ikd.append(ik2); srd.append(r2); sid.append(i2)
    return (yp, ys,
            jnp.stack(kp), jnp.stack(vp), jnp.stack(ikp), jnp.stack(srp), jnp.stack(sip),
            jnp.stack(kd), jnp.stack(vd), jnp.stack(ikd), jnp.stack(srd), jnp.stack(sid))
```

```python
import functools
import math

import jax
import jax.numpy as jnp
from jax import lax
from jax.experimental import pallas as pl
from jax.experimental.pallas import tpu as pltpu

F32 = jnp.float32
BF16 = jnp.bfloat16
I32 = jnp.int32

CHUNK = 64
Q_BLOCK = 128
D_ATTN = 1024
N_HEADS = 8
N_KV_HEADS = 2
HEAD_DIM = 128
KV_DIM = N_KV_HEADS * HEAD_DIM
HEADS_PER_KV = N_HEADS // N_KV_HEADS
N_IDX_HEADS = 16
IDX_DIM = 64
IDX_PAIRS = N_IDX_HEADS // 2
TOPK_MAX = 256
D_SSM = 1024
SSM_GROUP = 16
N_SSM_GROUPS = D_SSM // SSM_GROUP
SSM_STATE = 64
RMS_EPS = 1e-6
IN_SIZES = (D_ATTN, KV_DIM, KV_DIM, N_IDX_HEADS * IDX_DIM, IDX_DIM, N_IDX_HEADS, D_ATTN, D_SSM, D_SSM)

LANES = 128
V7X_VMEM_LIMIT_BYTES = 56 * 1024 * 1024

INT_MIN = -(2 ** 31)
MASKED_LOGIT = -1e30
MIN_ROW_SUM = 2.0 ** -100
KEY_TILE = 512


def _const_spec(shape):
    nd = len(shape)
    return pl.BlockSpec(shape, lambda *_: (0,) * nd, pipeline_mode=pl.Buffered(1))


def _dot_nt(a, b):
    return lax.dot_general(a, b, (((1,), (1,)), ((), ())), preferred_element_type=F32)


def _adaln_kernel(c_ref, w_ref, b_ref, o_ref):
    o_ref[...] = jnp.dot(c_ref[...], w_ref[...], preferred_element_type=F32) + b_ref[...]


def _adaln(c, w_ada, b_ada):
    rows, d = c.shape
    n_out = w_ada.shape[1]
    tn = 512
    return pl.pallas_call(
        _adaln_kernel,
        out_shape=jax.ShapeDtypeStruct((rows, n_out), F32),
        grid=(n_out // tn,),
        in_specs=[pl.BlockSpec((rows, d), lambda j: (0, 0)),
                  pl.BlockSpec((d, tn), lambda j: (0, j)),
                  pl.BlockSpec((1, tn), lambda j: (0, j))],
        out_specs=pl.BlockSpec((rows, tn), lambda j: (0, j)),
        compiler_params=pltpu.CompilerParams(dimension_semantics=("arbitrary",)),
        name="adaln",
    )(c, w_ada, b_ada)


def _proj_kernel(x_ref, scale_ref, shift_ref, g_ref,
                 wq_ref, wk_ref, wv_ref, wiq_ref, wik_ref, wiw_ref, wza_ref, wu_ref, wzs_ref,
                 q_ref, k_ref, kb_ref, v_ref, vb_ref, iq_ref, ik_ref, iklo_ref, ikhi_ref,
                 iw_ref, ga_ref, u_ref, gs_ref):
    x = x_ref[...]
    ms = jnp.mean(x * x, axis=-1, keepdims=True)
    y = x * lax.rsqrt(ms + RMS_EPS) * g_ref[...]
    h = (y * (1.0 + scale_ref[...]) + shift_ref[...]).astype(BF16)

    def mm(w_ref):
        return jnp.dot(h, w_ref[...], preferred_element_type=F32)

    q_ref[...] = (mm(wq_ref) * (HEAD_DIM ** -0.5 * math.log2(math.e))).astype(BF16)
    zk = mm(wk_ref)
    k_ref[...] = zk
    kb_ref[...] = zk.astype(BF16)
    zv = mm(wv_ref)
    v_ref[...] = zv
    vb_ref[...] = zv.astype(BF16)
    iq_ref[...] = (mm(wiq_ref) * (IDX_DIM ** -0.5)).astype(BF16)
    zik = mm(wik_ref)
    ik_ref[...] = zik[:, :IDX_DIM]
    lane = lax.broadcasted_iota(I32, zik.shape, 1)
    iklo_ref[...] = jnp.where(lane < IDX_DIM, zik, 0.0).astype(BF16)
    ikhi_ref[...] = jnp.where(lane >= IDX_DIM, zik, 0.0).astype(BF16)
    iw_ref[...] = mm(wiw_ref)[:, :N_IDX_HEADS] * (N_IDX_HEADS ** -0.5)
    za = mm(wza_ref)
    ga_ref[...] = (za * jax.nn.sigmoid(za)).astype(BF16)
    u_ref[...] = mm(wu_ref).astype(BF16)
    zs = mm(wzs_ref)
    gs_ref[...] = (zs * jax.nn.sigmoid(zs)).astype(BF16)


def _split_w_in(w_in):
    offs = [0]
    for s in IN_SIZES:
        offs.append(offs[-1] + s)
    cols = [w_in[:, offs[i]:offs[i + 1]].astype(BF16) for i in range(len(IN_SIZES))]
    wq, wk, wv, wiq, wik, wiw, wza, wu, wzs = cols
    wik2 = jnp.concatenate([wik, wik], axis=1)
    wiw_p = jnp.pad(wiw, ((0, 0), (0, LANES - N_IDX_HEADS)))
    return (wq, wk, wv, wiq, wik2, wiw_p, wza, wu, wzs)


def _proj(x2d, scale, shift, g_pre, weights, *, tm, rows_per_mod):
    n, d = x2d.shape
    if rows_per_mod is None:
        mod_spec = pl.BlockSpec((tm, d), lambda i: (i, 0))
    else:
        tiles_per_mod = rows_per_mod // tm
        mod_spec = pl.BlockSpec((None, 1, d), lambda i: (i // tiles_per_mod, 0, 0))

    def row_spec(width):
        return pl.BlockSpec((tm, width), lambda i: (i, 0))

    out_defs = [
        (D_ATTN, BF16), (KV_DIM, F32), (KV_DIM, BF16), (KV_DIM, F32), (KV_DIM, BF16),
        (N_IDX_HEADS * IDX_DIM, BF16), (IDX_DIM, F32), (LANES, BF16), (LANES, BF16),
        (N_IDX_HEADS, F32), (D_ATTN, BF16), (D_SSM, BF16), (D_SSM, BF16)]
    return pl.pallas_call(
        _proj_kernel,
        out_shape=[jax.ShapeDtypeStruct((n, w), dt) for w, dt in out_defs],
        grid=(n // tm,),
        in_specs=[row_spec(d), mod_spec, mod_spec, _const_spec((1, d))]
                 + [_const_spec(w.shape) for w in weights],
        out_specs=[row_spec(w) for w, _ in out_defs],
        compiler_params=pltpu.CompilerParams(dimension_semantics=("arbitrary",),
                                             vmem_limit_bytes=V7X_VMEM_LIMIT_BYTES),
        name="in_proj",
    )(x2d, scale, shift, g_pre, *weights)


def _dsa_kernel(q_ref, iq_ref, iw_ref, ga_ref, kb_ref, vb_ref, iklo_ref, ikhi_ref, o_ref,
                sc_ref, wb_ref, iqs_ref, qs_ref, m_ref, l_ref, acc_ref, kn_ref,
                *, qb, tk, topk, causal, n_keys, n_keys_pad):
    blk = pl.program_id(1)
    chunks_per_tile = tk // LANES
    row = lax.broadcasted_iota(I32, (qb, LANES), 0)
    if causal:
        n_vis = (blk + 1) * qb
        limit = (((blk * qb + row) // CHUNK) + 1) * CHUNK
        n_tiles = (n_vis + tk - 1) // tk
    else:
        limit = jnp.full((qb, LANES), n_keys, I32)
        n_tiles = (n_keys + tk - 1) // tk

    for j in range(IDX_PAIRS):
        iqs_ref[j * qb:(j + 1) * qb, :] = iq_ref[:, j * LANES:(j + 1) * LANES]
    for kv in range(N_KV_HEADS):
        for hh in range(HEADS_PER_KV):
            c0 = (kv * HEADS_PER_KV + hh) * HEAD_DIM
            qs_ref[kv, hh * qb:(hh + 1) * qb, :] = q_ref[:, c0:c0 + HEAD_DIM]
    iw = iw_ref[...]
    for h in range(N_IDX_HEADS):
        wb_ref[h] = jnp.broadcast_to(iw[:, h:h + 1], (qb, LANES))

    lane = lax.broadcasted_iota(I32, (qb, LANES), 1)

    def score_tile(t, carry):
        ks = pl.multiple_of(t * tk, tk)
        iqs = iqs_ref[...]
        s_lo = _dot_nt(iqs, iklo_ref[pl.ds(ks, tk), :])
        s_hi = _dot_nt(iqs, ikhi_ref[pl.ds(ks, tk), :])
        for c in range(chunks_per_tile):
            cs = slice(c * LANES, (c + 1) * LANES)
            acc = None
            for j in range(IDX_PAIRS):
                rs = slice(j * qb, (j + 1) * qb)
                term = (jnp.maximum(s_lo[rs, cs], 0.0) * wb_ref[2 * j]
                        + jnp.maximum(s_hi[rs, cs], 0.0) * wb_ref[2 * j + 1])
                acc = term if acc is None else acc + term
            kidx = ks + c * LANES + lane
            sc_ref[t * chunks_per_tile + c] = jnp.where(kidx < limit, acc, jnp.nan)
        return carry

    lax.fori_loop(0, n_tiles, score_tile, 0)

    def ordered_to_f32(key):
        return lax.bitcast_convert_type(jnp.where(key < 0, key ^ jnp.int32(0x7FFFFFFF), key), F32)

    def count_rows(pred):
        def count_tile(t, cnt):
            for c in range(chunks_per_tile):
                ci = t * chunks_per_tile + c
                cnt = cnt + jnp.where(pred(sc_ref[ci], ci * LANES + lane), 1, 0)
            return cnt

        cnt = lax.fori_loop(0, n_tiles, count_tile, jnp.zeros((qb, LANES), I32))
        return jnp.broadcast_to(jnp.sum(cnt, axis=1, keepdims=True), (qb, LANES))

    def bisect(i, carry):
        lo, cnt_lo = carry
        cand_key = lo + lax.shift_left(jnp.int32(1), 31 - i)
        cand = ordered_to_f32(cand_key)
        total = count_rows(lambda s, _: s >= cand)
        take = total >= topk
        return jnp.where(take, cand_key, lo), jnp.where(take, total, cnt_lo)

    lo, cnt_lo = lax.fori_loop(0, 32, bisect, (jnp.full((qb, LANES), INT_MIN, I32),
                                               jnp.zeros((qb, LANES), I32)))
    unmoved = lo == jnp.int32(INT_MIN)
    tau = jnp.where(unmoved, -jnp.inf, ordered_to_f32(lo))
    excess = jnp.where(unmoved, 0, cnt_lo - topk)

    @pl.when(jnp.max(excess) > 0)
    def _resolve_surplus():
        idx_bits = (n_keys_pad - 1).bit_length()

        def kept(s, kidx, vstar, jstar):
            return (s >= tau) & ((s > vstar) | ((s == vstar) & (kidx < jstar)))

        def drop_step(state):
            exc, vstar, jstar = state

            def min_tile(t, vm):
                for c in range(chunks_per_tile):
                    ci = t * chunks_per_tile + c
                    s = sc_ref[ci]
                    vm = jnp.minimum(vm, jnp.where(kept(s, ci * LANES + lane, vstar, jstar), s, jnp.inf))
                return vm

            vm = lax.fori_loop(0, n_tiles, min_tile, jnp.full((qb, LANES), jnp.inf, F32))
            vmin = jnp.broadcast_to(jnp.min(vm, axis=1, keepdims=True), (qb, LANES))

            def tied(s, kidx):
                return kept(s, kidx, vstar, jstar) & (s == vmin)

            cmin = count_rows(tied)
            active = exc > 0
            drop_all = active & (exc >= cmin)
            partial = active & (exc < cmin)
            keep = cmin - exc

            def idx_bisect(b, j0):
                cand_j = j0 + lax.shift_left(jnp.int32(1), idx_bits - 1 - b)
                below = count_rows(lambda s, kidx: tied(s, kidx) & (kidx < cand_j))
                return jnp.where(below < keep, cand_j, j0)

            j0 = lax.fori_loop(0, idx_bits, idx_bisect, jnp.zeros((qb, LANES), I32))
            vstar = jnp.where(active, vmin, vstar)
            jstar = jnp.where(drop_all, 0, jnp.where(partial, j0 + 1, jstar))
            exc = jnp.where(drop_all, exc - cmin, jnp.where(partial, 0, exc))
            return exc, vstar, jstar

        _, vstar, jstar = lax.while_loop(
            lambda state: jnp.max(state[0]) > 0, drop_step,
            (excess, tau, jnp.full((qb, LANES), 2 ** 30, I32)))

        def rewrite_tile(t, carry):
            for c in range(chunks_per_tile):
                ci = t * chunks_per_tile + c
                s = sc_ref[ci]
                dropped = (s >= tau) & jnp.logical_not(kept(s, ci * LANES + lane, vstar, jstar))
                sc_ref[ci] = jnp.where(dropped, jnp.nan, s)
            return carry

        lax.fori_loop(0, n_tiles, rewrite_tile, 0)

    def masked_logits(t, kv):
        ks = pl.multiple_of(t * tk, tk)
        bias = jnp.concatenate(
            [jnp.where(sc_ref[t * chunks_per_tile + c] >= tau, 0.0, MASKED_LOGIT)
             for c in range(chunks_per_tile)], axis=1)
        s = _dot_nt(qs_ref[kv], kb_ref[pl.ds(ks, tk), kv * HEAD_DIM:(kv + 1) * HEAD_DIM])
        return (s.reshape(HEADS_PER_KV, qb, tk) + bias[None]).reshape(HEADS_PER_KV * qb, tk)

    def attend_tile(t, carry):
        ks = pl.multiple_of(t * tk, tk)
        for kv in range(N_KV_HEADS):
            s = masked_logits(t, kv)
            m = m_ref[kv]
            l = l_ref[kv]
            ps = []
            for c in range(chunks_per_tile):
                e = jnp.exp2(s[:, c * LANES:(c + 1) * LANES] - m)
                l = l + e
                ps.append(e.astype(BF16))
            l_ref[kv] = l
            p = jnp.concatenate(ps, axis=1)
            acc_ref[kv] += jnp.dot(p, vb_ref[pl.ds(ks, tk), kv * HEAD_DIM:(kv + 1) * HEAD_DIM],
                                   preferred_element_type=F32)
        return carry

    def attend():
        l_ref[...] = jnp.zeros(l_ref.shape, F32)
        acc_ref[...] = jnp.zeros(acc_ref.shape, F32)
        lax.fori_loop(0, n_tiles, attend_tile, 0)

    @pl.when(blk == 0)
    def _key_norms():
        for kv in range(N_KV_HEADS):
            def norm_tile(t, mx):
                ks = pl.multiple_of(t * tk, tk)
                k = kb_ref[pl.ds(ks, tk), kv * HEAD_DIM:(kv + 1) * HEAD_DIM].astype(F32)
                return jnp.maximum(mx, jnp.sum(k * k, axis=1, keepdims=True))

            mx = lax.fori_loop(0, n_keys_pad // tk, norm_tile, jnp.zeros((tk, 1), F32))
            kn_ref[kv] = jnp.broadcast_to(jnp.sqrt(jnp.max(mx, axis=0, keepdims=True)), kn_ref.shape[1:])

    for kv in range(N_KV_HEADS):
        qf = qs_ref[kv].astype(F32)
        qn = jnp.sqrt(jnp.sum(qf * qf, axis=1, keepdims=True))
        m_ref[kv] = jnp.broadcast_to(qn, m_ref.shape[1:]) * kn_ref[kv, 0:1, :]
    attend()

    l_min = jnp.min(jnp.sum(l_ref[...], axis=2))

    @pl.when(jnp.logical_not(l_min >= MIN_ROW_SUM))
    def _exact_shift():
        m_ref[...] = jnp.full(m_ref.shape, MASKED_LOGIT, F32)

        def max_tile(t, carry):
            for kv in range(N_KV_HEADS):
                s = masked_logits(t, kv)
                mx = m_ref[kv]
                for c in range(chunks_per_tile):
                    mx = jnp.maximum(mx, s[:, c * LANES:(c + 1) * LANES])
                m_ref[kv] = mx
            return carry

        lax.fori_loop(0, n_tiles, max_tile, 0)
        for kv in range(N_KV_HEADS):
            m_ref[kv] = jnp.broadcast_to(jnp.max(m_ref[kv], axis=1, keepdims=True), m_ref.shape[1:])
        attend()

    for kv in range(N_KV_HEADS):
        o = acc_ref[kv] / jnp.sum(l_ref[kv], axis=1, keepdims=True)
        for hh in range(HEADS_PER_KV):
            c0 = (kv * HEADS_PER_KV + hh) * HEAD_DIM
            gate = ga_ref[:, c0:c0 + HEAD_DIM].astype(F32)
            o_ref[:, c0:c0 + HEAD_DIM] = (o[hh * qb:(hh + 1) * qb] * gate).astype(BF16)


def _dsa(q, iq, iw, ga, kb, vb, iklo, ikhi, *, qb, causal, n_keys):
    b, s, _ = q.shape
    lk = kb.shape[1]
    tk = KEY_TILE
    assert lk % tk == 0 and s % qb == 0
    topk = min(TOPK_MAX, n_keys // 4)
    kern = functools.partial(_dsa_kernel, qb=qb, tk=tk, topk=topk, causal=causal, n_keys=n_keys,
                             n_keys_pad=lk)

    def q_spec(width):
        return pl.BlockSpec((None, qb, width), lambda bi, i: (bi, i, 0))

    def kv_spec(width):
        return pl.BlockSpec((None, lk, width), lambda bi, i: (bi, 0, 0))

    return pl.pallas_call(
        kern,
        out_shape=jax.ShapeDtypeStruct((b, s, D_ATTN), BF16),
        grid=(b, s // qb),
        in_specs=[q_spec(D_ATTN), q_spec(N_IDX_HEADS * IDX_DIM), q_spec(N_IDX_HEADS), q_spec(D_ATTN),
                  kv_spec(KV_DIM), kv_spec(KV_DIM), kv_spec(LANES), kv_spec(LANES)],
        out_specs=q_spec(D_ATTN),
        scratch_shapes=[
            pltpu.VMEM((lk // LANES, qb, LANES), F32),
            pltpu.VMEM((N_IDX_HEADS, qb, LANES), F32),
            pltpu.VMEM((IDX_PAIRS * qb, LANES), BF16),
            pltpu.VMEM((N_KV_HEADS, HEADS_PER_KV * qb, HEAD_DIM), BF16),
            pltpu.VMEM((N_KV_HEADS, HEADS_PER_KV * qb, LANES), F32),
            pltpu.VMEM((N_KV_HEADS, HEADS_PER_KV * qb, LANES), F32),
            pltpu.VMEM((N_KV_HEADS, HEADS_PER_KV * qb, HEAD_DIM), F32),
            pltpu.VMEM((N_KV_HEADS, 8, LANES), F32),
        ],
        compiler_params=pltpu.CompilerParams(dimension_semantics=("arbitrary", "arbitrary"),
                                             vmem_limit_bytes=V7X_VMEM_LIMIT_BYTES),
        name="dsa",
    )(q, iq, iw, ga, kb, vb, iklo, ikhi)


def _ssm_prep_kernel(lre_c_ref, lim_c_ref, lre_r_ref, lim_r_ref, ldt_ref, btr_ref, bti_ref,
                     ctr_ref, cti_ref, kflat_ref, mtr_ref, mti_ref, emr_ref, emi_ref,
                     a64r_ref, a64i_ref, a16r_ref, a16i_ref):
    hp = lax.Precision.HIGHEST
    dt = jnp.exp(ldt_ref[...])
    n_lag = CHUNK
    width = n_lag * SSM_GROUP

    ldr_c = lre_c_ref[...] * dt
    ldi_c = lim_c_ref[...] * dt
    lag = (lax.broadcasted_iota(I32, (1, width), 1) // SSM_GROUP).astype(F32)
    mag = jnp.exp(ldr_c * lag)
    pr = mag * jnp.cos(ldi_c * lag)
    pi = mag * jnp.sin(ldi_c * lag)
    ctr = ctr_ref[...]
    cti = cti_ref[...]
    qr = pr * ctr - pi * cti
    qi = pr * cti + pi * ctr

    lre_r = lre_r_ref[...]
    lim_r = lim_r_ref[...]
    ldr_r = lre_r * dt
    ldi_r = lim_r * dt
    lbr = jnp.exp(ldr_r) * jnp.cos(ldi_r)
    lbi = jnp.exp(ldr_r) * jnp.sin(ldi_r)
    den = lre_r * lre_r + lim_r * lim_r
    nr = lbr - 1.0
    fr = (nr * lre_r + lbi * lim_r) / den
    fi = (lbi * lre_r - nr * lim_r) / den
    btr = btr_ref[...]
    bti = bti_ref[...]
    bbr = fr * btr - fi * bti
    bbi = fr * bti + fi * btr

    kflat_ref[...] = (jnp.dot(bbr, qr, precision=hp, preferred_element_type=F32)
                      - jnp.dot(bbi, qi, precision=hp, preferred_element_type=F32))

    lbr_c = jnp.exp(ldr_c) * jnp.cos(ldi_c)
    lbi_c = jnp.exp(ldr_c) * jnp.sin(ldi_c)
    emr_ref[...] = (lbr_c * qr - lbi_c * qi).astype(BF16)
    emi_ref[...] = (-(lbr_c * qi + lbi_c * qr)).astype(BF16)

    back = (n_lag - 1 - lax.broadcasted_iota(I32, (n_lag, 1), 0)).astype(F32)
    bmag = jnp.exp(ldr_r * back)
    bpr = bmag * jnp.cos(ldi_r * back)
    bpi = bmag * jnp.sin(ldi_r * back)
    for t in range(n_lag):
        wr = bpr[t:t + 1, :]
        wi = bpi[t:t + 1, :]
        rows = slice(t * SSM_GROUP, (t + 1) * SSM_GROUP)
        mtr_ref[rows, :] = (wr * bbr - wi * bbi).astype(BF16)
        mti_ref[rows, :] = (wr * bbi + wi * bbr).astype(BF16)

    for steps, ar_ref, ai_ref in ((float(CHUNK), a64r_ref, a64i_ref), (16.0, a16r_ref, a16i_ref)):
        amag = jnp.exp(ldr_r * steps)
        ar_ref[...] = amag * jnp.cos(ldi_r * steps)
        ai_ref[...] = amag * jnp.sin(ldi_r * steps)


def _ssm_prep(lambda_re, lambda_im, log_dt, b_re, b_im, c_re, c_im):
    g, p = lambda_re.shape
    width = CHUNK * SSM_GROUP
    lre_c = lambda_re.reshape(g, p, 1)
    lim_c = lambda_im.reshape(g, p, 1)
    lre_r = lambda_re.reshape(g, 1, p)
    lim_r = lambda_im.reshape(g, 1, p)
    ldt = log_dt.reshape(g, 1, 1)
    btr = jnp.swapaxes(b_re, 1, 2)
    bti = jnp.swapaxes(b_im, 1, 2)
    ctr = jnp.tile(jnp.swapaxes(c_re, 1, 2), (1, 1, CHUNK))
    cti = jnp.tile(jnp.swapaxes(c_im, 1, 2), (1, 1, CHUNK))

    def gspec(*shape):
        return pl.BlockSpec((None,) + shape, lambda i: (i,) + (0,) * len(shape))

    out_defs = [((SSM_GROUP, width), F32), ((width, p), BF16), ((width, p), BF16),
                ((p, width), BF16), ((p, width), BF16),
                ((1, p), F32), ((1, p), F32), ((1, p), F32), ((1, p), F32)]
    return pl.pallas_call(
        _ssm_prep_kernel,
        out_shape=[jax.ShapeDtypeStruct((g,) + s, dt) for s, dt in out_defs],
        grid=(g,),
        in_specs=[gspec(p, 1), gspec(p, 1), gspec(1, p), gspec(1, p), gspec(1, 1),
                  gspec(SSM_GROUP, p), gspec(SSM_GROUP, p), gspec(p, width), gspec(p, width)],
        out_specs=[gspec(*s) for s, _ in out_defs],
        compiler_params=pltpu.CompilerParams(dimension_semantics=("arbitrary",)),
        name="ssm_prep",
    )(lre_c, lim_c, lre_r, lim_r, ldt, btr, bti, ctr, cti)


def _ssm_kernel(x_ref, kflat_ref, mtr_ref, mti_ref, emr_ref, emi_ref, ar_ref, ai_ref,
                h0r_ref, h0i_ref, dvec_ref, y_ref, hr_ref, hi_ref,
                toep_ref, gr_ref, gi_ref, hpr_ref, hpi_ref, *, t_len, n_seq, n_chunks):
    width = t_len * SSM_GROUP
    kflat = kflat_ref[:, :width]
    lane = lax.broadcasted_iota(I32, (SSM_GROUP, width), 1)
    toep_ref[0:SSM_GROUP, :] = kflat.astype(BF16)
    for t in range(1, t_len):
        shifted = pltpu.roll(kflat, t * SSM_GROUP, axis=1)
        toep_ref[t * SSM_GROUP:(t + 1) * SSM_GROUP, :] = jnp.where(
            lane >= t * SSM_GROUP, shifted, 0.0).astype(BF16)

    x = x_ref[...]
    gr_ref[...] = jnp.dot(x, mtr_ref[...], preferred_element_type=F32)
    gi_ref[...] = jnp.dot(x, mti_ref[...], preferred_element_type=F32)

    ar = ar_ref[...]
    ai = ai_ref[...]

    def chunk_step(n, carry):
        h_r, h_i = carry
        rows = pl.ds(n, n_seq, stride=n_chunks)
        hpr_ref[rows, :] = h_r
        hpi_ref[rows, :] = h_i
        g_r = gr_ref[rows, :]
        g_i = gi_ref[rows, :]
        return (ar * h_r - ai * h_i + g_r, ar * h_i + ai * h_r + g_i)

    if n_chunks == 1:
        h_r, h_i = chunk_step(0, (h0r_ref[...], h0i_ref[...]))
    else:
        h_r, h_i = lax.fori_loop(0, n_chunks, chunk_step, (h0r_ref[...], h0i_ref[...]))
    hr_ref[...] = h_r
    hi_ref[...] = h_i

    y = jnp.dot(x, toep_ref[...], preferred_element_type=F32)
    y = y + jnp.dot(hpr_ref[...].astype(BF16), emr_ref[:, :width], preferred_element_type=F32)
    y = y + jnp.dot(hpi_ref[...].astype(BF16), emi_ref[:, :width], preferred_element_type=F32)
    y = y + x.astype(F32) * dvec_ref[...]
    y_ref[...] = y.astype(BF16)


def _ssm(xg, prep, h0r, h0i, dvec, *, t_len, n_seq, n_chunks):
    kflat, mtr, mti, emr, emi, a64r, a64i, a16r, a16i = prep
    g, rows, width = xg.shape
    p = SSM_STATE
    full = CHUNK * SSM_GROUP
    if t_len == CHUNK:
        ar, ai = a64r, a64i
        mt_block = 0
    else:
        assert t_len == 16
        ar, ai = a16r, a16i
        mt_block = (full - width) // width
    kern = functools.partial(_ssm_kernel, t_len=t_len, n_seq=n_seq, n_chunks=n_chunks)

    def gspec(*shape):
        return pl.BlockSpec((None,) + shape, lambda i: (i,) + (0,) * len(shape))

    mt_spec = pl.BlockSpec((None, width, p), lambda i: (i, mt_block, 0))
    return pl.pallas_call(
        kern,
        out_shape=[jax.ShapeDtypeStruct((g, rows, width), BF16),
                   jax.ShapeDtypeStruct((g, n_seq, p), F32),
                   jax.ShapeDtypeStruct((g, n_seq, p), F32)],
        grid=(g,),
        in_specs=[gspec(rows, width), gspec(SSM_GROUP, full), mt_spec, mt_spec,
                  gspec(p, full), gspec(p, full), gspec(1, p), gspec(1, p),
                  gspec(n_seq, p), gspec(n_seq, p), gspec(1, width)],
        out_specs=[gspec(rows, width), gspec(n_seq, p), gspec(n_seq, p)],
        scratch_shapes=[pltpu.VMEM((width, width), BF16),
                        pltpu.VMEM((rows, p), F32), pltpu.VMEM((rows, p), F32),
                        pltpu.VMEM((rows, p), F32), pltpu.VMEM((rows, p), F32)],
        compiler_params=pltpu.CompilerParams(dimension_semantics=("arbitrary",)),
        name="ssm",
    )(xg, kflat, mtr, mti, emr, emi, ar, ai, h0r, h0i, dvec)


def _out_kernel(a_ref, ys_ref, gs_ref, x_ref, gate_ref, wglu_ref, bglu_ref, woa_ref, wos_ref,
                gpost_ref, o_ref):
    y = ys_ref[...].astype(F32)
    y = 0.5 * y * (1.0 + jnp.tanh(math.sqrt(2.0 / math.pi) * (y + 0.044715 * (y * y * y))))
    z = jnp.dot(y.astype(BF16), wglu_ref[...], preferred_element_type=F32) + bglu_ref[...]
    y = y * jax.nn.sigmoid(z) * gs_ref[...].astype(F32)
    out = (jnp.dot(a_ref[...], woa_ref[...], preferred_element_type=F32)
           + jnp.dot(y.astype(BF16), wos_ref[...], preferred_element_type=F32))
    ms = jnp.mean(out * out, axis=-1, keepdims=True)
    normed = out * lax.rsqrt(ms + RMS_EPS) * gpost_ref[...]
    o_ref[...] = x_ref[...] + gate_ref[...] * normed


def _out_proj(a, ys, gs, x2d, gate, w_glu, b_glu, w_out, g_post, *, tm, rows_per_mod):
    n, d = x2d.shape
    if rows_per_mod is None:
        mod_spec = pl.BlockSpec((tm, d), lambda i: (i, 0))
    else:
        tiles_per_mod = rows_per_mod // tm
        mod_spec = pl.BlockSpec((None, 1, d), lambda i: (i // tiles_per_mod, 0, 0))

    def row_spec(width):
        return pl.BlockSpec((tm, width), lambda i: (i, 0))

    wglu = w_glu.astype(BF16)
    woa = w_out[:D_ATTN].astype(BF16)
    wos = w_out[D_ATTN:].astype(BF16)
    return pl.pallas_call(
        _out_kernel,
        out_shape=jax.ShapeDtypeStruct((n, d), F32),
        grid=(n // tm,),
        in_specs=[row_spec(D_ATTN), row_spec(D_SSM), row_spec(D_SSM), row_spec(d), mod_spec,
                  _const_spec(wglu.shape), _const_spec((1, D_SSM)), _const_spec(woa.shape),
                  _const_spec(wos.shape), _const_spec((1, d))],
        out_specs=row_spec(d),
        compiler_params=pltpu.CompilerParams(dimension_semantics=("arbitrary",),
                                             vmem_limit_bytes=V7X_VMEM_LIMIT_BYTES),
        name="out_proj",
    )(a, ys, gs, x2d, gate, wglu, b_glu.reshape(1, D_SSM), woa, wos, g_post.reshape(1, d))


def _to_groups(u2d, n_seq, n_chunks, t_len):
    u5 = u2d.reshape(n_seq, n_chunks, t_len, N_SSM_GROUPS, SSM_GROUP)
    return u5.transpose(3, 0, 1, 2, 4).reshape(N_SSM_GROUPS, n_seq * n_chunks, t_len * SSM_GROUP)


def _from_groups(yg, n_seq, n_chunks, t_len):
    y5 = yg.reshape(N_SSM_GROUPS, n_seq, n_chunks, t_len, SSM_GROUP)
    return y5.transpose(1, 2, 3, 0, 4).reshape(n_seq * n_chunks * t_len, D_SSM)


def _pad_rows(a, rows):
    return jnp.pad(a, ((0, 0), (0, rows - a.shape[1]), (0, 0)))


def _layer(x, mod, past, g_pre, g_post, weights, prep, d_skip, w_glu, b_glu, w_out):
    bsz, s, d = x.shape
    n = bsz * s
    x2d = x.reshape(n, d)
    shift, scale, gate = mod[:, :d], mod[:, d:2 * d], mod[:, 2 * d:]
    if past is None:
        tm = 256
        rows_per_mod = s
        mods = [m.reshape(bsz, 1, d) for m in (scale, shift, gate)]
    else:
        tm = n
        rows_per_mod = None
        mods = [jnp.repeat(m, s, axis=0) for m in (scale, shift, gate)]
    scale_m, shift_m, gate_m = mods

    (q, k32, kb, v32, vb, iq, ik32, iklo, ikhi, iw, ga, u, gs) = _proj(
        x2d, scale_m, shift_m, g_pre.reshape(1, d), weights, tm=tm, rows_per_mod=rows_per_mod)

    def seq(a):
        return a.reshape(bsz, s, a.shape[-1])

    if past is None:
        attn = _dsa(seq(q), seq(iq), seq(iw), seq(ga), seq(kb), seq(vb), seq(iklo), seq(ikhi),
                    qb=Q_BLOCK, causal=True, n_keys=s)
        t_len = CHUNK if s % CHUNK == 0 else s
        h0r = jnp.zeros((N_SSM_GROUPS, bsz, SSM_STATE), F32)
        h0i = h0r
    else:
        ck, cv, cik, sre, sim = past
        past_len = ck.shape[1]
        n_keys = past_len + s
        lk = -(-n_keys // KEY_TILE) * KEY_TILE
        k_all = jnp.concatenate([ck.reshape(bsz, past_len, KV_DIM).astype(BF16), seq(kb)], axis=1)
        v_all = jnp.concatenate([cv.reshape(bsz, past_len, KV_DIM).astype(BF16), seq(vb)], axis=1)
        cik_b = cik.astype(BF16)
        zeros = jnp.zeros_like(cik_b)
        iklo_all = jnp.concatenate([jnp.concatenate([cik_b, zeros], axis=-1), seq(iklo)], axis=1)
        ikhi_all = jnp.concatenate([jnp.concatenate([zeros, cik_b], axis=-1), seq(ikhi)], axis=1)
        attn = _dsa(seq(q), seq(iq), seq(iw), seq(ga), _pad_rows(k_all, lk), _pad_rows(v_all, lk),
                    _pad_rows(iklo_all, lk), _pad_rows(ikhi_all, lk),
                    qb=s, causal=False, n_keys=n_keys)
        t_len = CHUNK if s % CHUNK == 0 else s
        h0r = jnp.swapaxes(sre.astype(F32), 0, 1)
        h0i = jnp.swapaxes(sim.astype(F32), 0, 1)

    n_chunks = s // t_len
    xg = _to_groups(u, bsz, n_chunks, t_len)
    dvec = jnp.tile(d_skip.astype(F32).reshape(N_SSM_GROUPS, 1, SSM_GROUP), (1, 1, t_len))
    rows = bsz * n_chunks
    rows_pad = -(-rows // 16) * 16
    xg = _pad_rows(xg, rows_pad) if rows_pad != rows else xg
    if rows_pad != rows:
        assert n_chunks == 1
        h0r = _pad_rows(h0r, rows_pad)
        h0i = _pad_rows(h0i, rows_pad)
    yg, hr, hi = _ssm(xg, prep, h0r, h0i, dvec, t_len=t_len,
                      n_seq=rows_pad // n_chunks, n_chunks=n_chunks)
    ys = _from_groups(yg[:, :rows], bsz, n_chunks, t_len)
    hr = jnp.swapaxes(hr[:, :bsz], 0, 1)
    hi = jnp.swapaxes(hi[:, :bsz], 0, 1)

    y = _out_proj(attn.reshape(n, D_ATTN), ys, gs, x2d, gate_m, w_glu, b_glu, w_out, g_post,
                  tm=tm, rows_per_mod=rows_per_mod)
    k_out = k32.reshape(bsz, s, N_KV_HEADS, HEAD_DIM)
    v_out = v32.reshape(bsz, s, N_KV_HEADS, HEAD_DIM)
    ik_out = ik32.reshape(bsz, s, IDX_DIM)
    return y.reshape(bsz, s, d), (k_out, v_out, ik_out, hr, hi)


def kernel(x_prompt, x_sample, c_prompt, c_sample, cache_k, cache_v, cache_idx_k, state_ssm_re,
           state_ssm_im, w_ada, b_ada, g_pre, g_post, w_in, lambda_re, lambda_im, log_dt, b_re, b_im,
           c_re, c_im, d_skip, w_glu, b_glu, w_out):
    depth = w_ada.shape[0]
    bp = c_prompt.shape[0]
    bs = c_sample.shape[0]
    yp, ys = x_prompt, x_sample
    outs_p = [[] for _ in range(5)]
    outs_s = [[] for _ in range(5)]
    c_all = jnp.concatenate([c_prompt, c_sample], axis=0)
    c_rows = -(-c_all.shape[0] // 8) * 8
    c_all = jnp.pad(c_all, ((0, c_rows - c_all.shape[0]), (0, 0)))
    for l in range(depth):
        mod = _adaln(c_all, w_ada[l], b_ada[l].reshape(1, -1))
        weights = _split_w_in(w_in[l])
        prep = _ssm_prep(lambda_re[l], lambda_im[l], log_dt[l], b_re[l], b_im[l], c_re[l], c_im[l])
        common = (g_pre[l], g_post[l], weights, prep, d_skip[l], w_glu[l], b_glu[l], w_out[l])
        yp, new_p = _layer(yp, mod[:bp], None, *common)
        past = (cache_k[l], cache_v[l], cache_idx_k[l], state_ssm_re[l], state_ssm_im[l])
        ys, new_s = _layer(ys, mod[bp:bp + bs], past, *common)
        for acc, val in zip(outs_p, new_p):
            acc.append(val)
        for acc, val in zip(outs_s, new_s):
            acc.append(val)
    return (yp, ys) + tuple(jnp.stack(a) for a in outs_p) + tuple(jnp.stack(a) for a in outs_s)
```

```python
import functools
import math

import jax
import jax.numpy as jnp
from jax import lax
from jax.experimental import pallas as pl
from jax.experimental.pallas import tpu as pltpu

F32 = jnp.float32
BF16 = jnp.bfloat16
I32 = jnp.int32

CHUNK = 64
Q_BLOCK = 128
D_ATTN = 1024
N_HEADS = 8
N_KV_HEADS = 2
HEAD_DIM = 128
KV_DIM = N_KV_HEADS * HEAD_DIM
HEADS_PER_KV = N_HEADS // N_KV_HEADS
N_IDX_HEADS = 16
IDX_DIM = 64
IDX_PAIRS = N_IDX_HEADS // 2
TOPK_MAX = 256
D_SSM = 1024
SSM_GROUP = 16
N_SSM_GROUPS = D_SSM // SSM_GROUP
SSM_STATE = 64
RMS_EPS = 1e-6
IN_SIZES = (D_ATTN, KV_DIM, KV_DIM, N_IDX_HEADS * IDX_DIM, IDX_DIM, N_IDX_HEADS, D_ATTN, D_SSM, D_SSM)

LANES = 128
V7X_VMEM_LIMIT_BYTES = 56 * 1024 * 1024

INT_MIN = -(2 ** 31)
MASKED_LOGIT = -1e30
MIN_ROW_SUM = 2.0 ** -100
KEY_TILE = 512
COARSE_BITS = 16


def _const_spec(shape):
    nd = len(shape)
    return pl.BlockSpec(shape, lambda *_: (0,) * nd, pipeline_mode=pl.Buffered(1))


def _dot_nt(a, b):
    return lax.dot_general(a, b, (((1,), (1,)), ((), ())), preferred_element_type=F32)


def _adaln_kernel(c_ref, w_ref, b_ref, o_ref):
    o_ref[...] = jnp.dot(c_ref[...], w_ref[...], preferred_element_type=F32) + b_ref[...]


def _adaln(c, w_ada, b_ada):
    rows, d = c.shape
    n_out = w_ada.shape[1]
    tn = 512
    return pl.pallas_call(
        _adaln_kernel,
        out_shape=jax.ShapeDtypeStruct((rows, n_out), F32),
        grid=(n_out // tn,),
        in_specs=[pl.BlockSpec((rows, d), lambda j: (0, 0)),
                  pl.BlockSpec((d, tn), lambda j: (0, j)),
                  pl.BlockSpec((1, tn), lambda j: (0, j))],
        out_specs=pl.BlockSpec((rows, tn), lambda j: (0, j)),
        compiler_params=pltpu.CompilerParams(dimension_semantics=("arbitrary",)),
        name="adaln",
    )(c, w_ada, b_ada)


def _proj_kernel(x_ref, scale_ref, shift_ref, g_ref,
                 wq_ref, wk_ref, wv_ref, wiq_ref, wik_ref, wiw_ref, wza_ref, wu_ref, wzs_ref,
                 q_ref, k_ref, kb_ref, v_ref, vb_ref, iq_ref, ik_ref, iklo_ref, ikhi_ref,
                 iw_ref, ga_ref, u_ref, gs_ref):
    x = x_ref[...]
    ms = jnp.mean(x * x, axis=-1, keepdims=True)
    y = x * lax.rsqrt(ms + RMS_EPS) * g_ref[...]
    h = (y * (1.0 + scale_ref[...]) + shift_ref[...]).astype(BF16)

    def mm(w_ref):
        return jnp.dot(h, w_ref[...], preferred_element_type=F32)

    q_ref[...] = (mm(wq_ref) * (HEAD_DIM ** -0.5 * math.log2(math.e))).astype(BF16)
    zk = mm(wk_ref)
    k_ref[...] = zk
    kb_ref[...] = zk.astype(BF16)
    zv = mm(wv_ref)
    v_ref[...] = zv
    vb_ref[...] = zv.astype(BF16)
    iq_ref[...] = (mm(wiq_ref) * (IDX_DIM ** -0.5)).astype(BF16)
    zik = mm(wik_ref)
    ik_ref[...] = zik[:, :IDX_DIM]
    lane = lax.broadcasted_iota(I32, zik.shape, 1)
    iklo_ref[...] = jnp.where(lane < IDX_DIM, zik, 0.0).astype(BF16)
    ikhi_ref[...] = jnp.where(lane >= IDX_DIM, zik, 0.0).astype(BF16)
    iw_ref[...] = mm(wiw_ref)[:, :N_IDX_HEADS] * (N_IDX_HEADS ** -0.5)
    za = mm(wza_ref)
    ga_ref[...] = (za * jax.nn.sigmoid(za)).astype(BF16)
    u_ref[...] = mm(wu_ref).astype(BF16)
    zs = mm(wzs_ref)
    gs_ref[...] = (zs * jax.nn.sigmoid(zs)).astype(BF16)


def _split_w_in(w_in):
    offs = [0]
    for s in IN_SIZES:
        offs.append(offs[-1] + s)
    cols = [w_in[:, offs[i]:offs[i + 1]].astype(BF16) for i in range(len(IN_SIZES))]
    wq, wk, wv, wiq, wik, wiw, wza, wu, wzs = cols
    wik2 = jnp.concatenate([wik, wik], axis=1)
    wiw_p = jnp.pad(wiw, ((0, 0), (0, LANES - N_IDX_HEADS)))
    return (wq, wk, wv, wiq, wik2, wiw_p, wza, wu, wzs)


def _proj(x2d, scale, shift, g_pre, weights, *, tm, rows_per_mod):
    n, d = x2d.shape
    if rows_per_mod is None:
        mod_spec = pl.BlockSpec((tm, d), lambda i: (i, 0))
    else:
        tiles_per_mod = rows_per_mod // tm
        mod_spec = pl.BlockSpec((None, 1, d), lambda i: (i // tiles_per_mod, 0, 0))

    def row_spec(width):
        return pl.BlockSpec((tm, width), lambda i: (i, 0))

    out_defs = [
        (D_ATTN, BF16), (KV_DIM, F32), (KV_DIM, BF16), (KV_DIM, F32), (KV_DIM, BF16),
        (N_IDX_HEADS * IDX_DIM, BF16), (IDX_DIM, F32), (LANES, BF16), (LANES, BF16),
        (N_IDX_HEADS, F32), (D_ATTN, BF16), (D_SSM, BF16), (D_SSM, BF16)]
    return pl.pallas_call(
        _proj_kernel,
        out_shape=[jax.ShapeDtypeStruct((n, w), dt) for w, dt in out_defs],
        grid=(n // tm,),
        in_specs=[row_spec(d), mod_spec, mod_spec, _const_spec((1, d))]
                 + [_const_spec(w.shape) for w in weights],
        out_specs=[row_spec(w) for w, _ in out_defs],
        compiler_params=pltpu.CompilerParams(dimension_semantics=("arbitrary",),
                                             vmem_limit_bytes=V7X_VMEM_LIMIT_BYTES),
        name="in_proj",
    )(x2d, scale, shift, g_pre, *weights)


def _dsa_kernel(q_ref, iq_ref, iw_ref, ga_ref, kb_ref, vb_ref, iklo_ref, ikhi_ref, o_ref,
                sc_ref, wb_ref, iqs_ref, qs_ref, m_ref, l_ref, acc_ref, kn_ref,
                *, qb, tk, topk, causal, n_keys, n_keys_pad):
    blk = pl.program_id(1)
    chunks_per_tile = tk // LANES
    row = lax.broadcasted_iota(I32, (qb, LANES), 0)
    if causal:
        n_vis = (blk + 1) * qb
        limit = (((blk * qb + row) // CHUNK) + 1) * CHUNK
        n_tiles = (n_vis + tk - 1) // tk
    else:
        limit = jnp.full((qb, LANES), n_keys, I32)
        n_tiles = (n_keys + tk - 1) // tk

    for j in range(IDX_PAIRS):
        iqs_ref[j * qb:(j + 1) * qb, :] = iq_ref[:, j * LANES:(j + 1) * LANES]
    for kv in range(N_KV_HEADS):
        for hh in range(HEADS_PER_KV):
            c0 = (kv * HEADS_PER_KV + hh) * HEAD_DIM
            qs_ref[kv, hh * qb:(hh + 1) * qb, :] = q_ref[:, c0:c0 + HEAD_DIM]
    iw = iw_ref[...]
    for h in range(N_IDX_HEADS):
        wb_ref[h] = jnp.broadcast_to(iw[:, h:h + 1], (qb, LANES))

    lane = lax.broadcasted_iota(I32, (qb, LANES), 1)

    def score_tile(t, carry):
        ks = pl.multiple_of(t * tk, tk)
        iqs = iqs_ref[...]
        s_lo = _dot_nt(iqs, iklo_ref[pl.ds(ks, tk), :])
        s_hi = _dot_nt(iqs, ikhi_ref[pl.ds(ks, tk), :])
        for c in range(chunks_per_tile):
            cs = slice(c * LANES, (c + 1) * LANES)
            acc = None
            for j in range(IDX_PAIRS):
                rs = slice(j * qb, (j + 1) * qb)
                term = (jnp.maximum(s_lo[rs, cs], 0.0) * wb_ref[2 * j]
                        + jnp.maximum(s_hi[rs, cs], 0.0) * wb_ref[2 * j + 1])
                acc = term if acc is None else acc + term
            kidx = ks + c * LANES + lane
            sc_ref[t * chunks_per_tile + c] = jnp.where(kidx < limit, acc, jnp.nan)
        return carry

    lax.fori_loop(0, n_tiles, score_tile, 0)

    def ordered_to_f32(key):
        return lax.bitcast_convert_type(jnp.where(key < 0, key ^ jnp.int32(0x7FFFFFFF), key), F32)

    def lane_total(cnt):
        return jnp.broadcast_to(jnp.sum(cnt, axis=1, keepdims=True), (qb, LANES))

    def count_rows(pred):
        def count_tile(t, cnt):
            for c in range(chunks_per_tile):
                ci = t * chunks_per_tile + c
                cnt = cnt + jnp.where(pred(sc_ref[ci], ci * LANES + lane), 1.0, 0.0)
            return cnt

        return lane_total(lax.fori_loop(0, n_tiles, count_tile, jnp.zeros((qb, LANES), F32)))

    def bisect(i, carry):
        lo, cnt_lo = carry
        cand_key = lo + lax.shift_left(jnp.int32(1), 31 - i)
        cand = ordered_to_f32(cand_key)
        total = count_rows(lambda s, _: s >= cand)
        take = total >= topk
        return jnp.where(take, cand_key, lo), jnp.where(take, total, cnt_lo)

    lo, cnt_lo = lax.fori_loop(0, COARSE_BITS, bisect, (jnp.full((qb, LANES), INT_MIN, I32),
                                                        jnp.zeros((qb, LANES), F32)))
    moved = lo != jnp.int32(INT_MIN)

    lo_f = ordered_to_f32(lo)
    hi_f = ordered_to_f32(lo + jnp.int32(1 << (32 - COARSE_BITS)))

    def bucket_tile(t, carry):
        b1, b2, b3, pop = carry
        for c in range(chunks_per_tile):
            s = sc_ref[t * chunks_per_tile + c]
            inb = (s >= lo_f) & jnp.logical_not(s >= hi_f)
            v = jnp.where(inb, s, -jnp.inf)
            pop = pop + jnp.where(inb, 1.0, 0.0)
            b1, v = jnp.maximum(b1, v), jnp.minimum(b1, v)
            b2, v = jnp.maximum(b2, v), jnp.minimum(b2, v)
            b3 = jnp.maximum(b3, v)
        return b1, b2, b3, pop

    ninf = jnp.full((qb, LANES), -jnp.inf, F32)
    b1, b2, b3, pop = lax.fori_loop(0, n_tiles, bucket_tile,
                                    (ninf, ninf, ninf, jnp.zeros((qb, LANES), F32)))
    n_bucket = lane_total(pop)
    n_above = cnt_lo - n_bucket
    want = topk - n_above
    lane_overflow = jnp.max(jnp.where(moved, pop, 0.0)) > 3.0

    def fine_from_lanes():
        def fine(i, carry):
            lo2, cnt2 = carry
            cand_key = lo2 + lax.shift_left(jnp.int32(1), 31 - i)
            cand = ordered_to_f32(cand_key)
            total = lane_total(jnp.where(b1 >= cand, 1.0, 0.0) + jnp.where(b2 >= cand, 1.0, 0.0)
                               + jnp.where(b3 >= cand, 1.0, 0.0))
            take = moved & (total >= want)
            return jnp.where(take, cand_key, lo2), jnp.where(take, total, cnt2)

        lo2, cnt2 = lax.fori_loop(COARSE_BITS, 32, fine, (lo, n_bucket))
        return lo2, n_above + cnt2

    def fine_from_lanes_transposed():
        def rows_to_lanes(x):
            return x.T[:8]

        cands = jnp.concatenate([b1.T, b2.T, b3.T], axis=0).reshape(3 * LANES // 8, 8, qb)
        lo_t = lax.bitcast_convert_type(rows_to_lanes(lax.bitcast_convert_type(lo, F32)), I32)
        want_t = rows_to_lanes(want)
        moved_t = lo_t != jnp.int32(INT_MIN)

        def fine(i, carry):
            lo2, cnt2 = carry
            cand_key = lo2 + lax.shift_left(jnp.int32(1), 31 - i)
            cand = ordered_to_f32(cand_key)
            part = jnp.sum(jnp.where(cands >= cand[None], 1.0, 0.0), axis=0)
            total = jnp.broadcast_to(jnp.sum(part, axis=0, keepdims=True), (8, qb))
            take = moved_t & (total >= want_t)
            return jnp.where(take, cand_key, lo2), jnp.where(take, total, cnt2)

        lo2, cnt2 = lax.fori_loop(COARSE_BITS, 32, fine, (lo_t, rows_to_lanes(n_bucket)))

        def lanes_to_rows(x):
            return jnp.broadcast_to(x[:1], (LANES, qb)).T

        lo2 = lax.bitcast_convert_type(lanes_to_rows(lax.bitcast_convert_type(lo2, F32)), I32)
        return lo2, n_above + lanes_to_rows(cnt2)

    def fine_from_all():
        return lax.fori_loop(COARSE_BITS, 32, bisect, (lo, cnt_lo))

    lo, cnt_lo = lax.cond(lane_overflow, fine_from_all,
                          fine_from_lanes_transposed if qb == LANES else fine_from_lanes)
    unmoved = jnp.logical_not(moved)
    tau = jnp.where(unmoved, -jnp.inf, ordered_to_f32(lo))
    excess = jnp.where(unmoved, 0.0, cnt_lo - topk)

    @pl.when(jnp.max(excess) > 0)
    def _resolve_surplus():
        idx_bits = (n_keys_pad - 1).bit_length()

        def kept(s, kidx, vstar, jstar):
            return (s >= tau) & ((s > vstar) | ((s == vstar) & (kidx < jstar)))

        def drop_step(state):
            exc, vstar, jstar = state

            def min_tile(t, vm):
                for c in range(chunks_per_tile):
                    ci = t * chunks_per_tile + c
                    s = sc_ref[ci]
                    vm = jnp.minimum(vm, jnp.where(kept(s, ci * LANES + lane, vstar, jstar), s, jnp.inf))
                return vm

            vm = lax.fori_loop(0, n_tiles, min_tile, jnp.full((qb, LANES), jnp.inf, F32))
            vmin = jnp.broadcast_to(jnp.min(vm, axis=1, keepdims=True), (qb, LANES))

            def tied(s, kidx):
                return kept(s, kidx, vstar, jstar) & (s == vmin)

            cmin = count_rows(tied)
            active = exc > 0
            drop_all = active & (exc >= cmin)
            partial = active & (exc < cmin)
            keep = cmin - exc

            def idx_bisect(b, j0):
                cand_j = j0 + lax.shift_left(jnp.int32(1), idx_bits - 1 - b)
                below = count_rows(lambda s, kidx: tied(s, kidx) & (kidx < cand_j))
                return jnp.where(below < keep, cand_j, j0)

            j0 = lax.fori_loop(0, idx_bits, idx_bisect, jnp.zeros((qb, LANES), I32))
            vstar = jnp.where(active, vmin, vstar)
            jstar = jnp.where(drop_all, 0, jnp.where(partial, j0 + 1, jstar))
            exc = jnp.where(drop_all, exc - cmin, jnp.where(partial, 0, exc))
            return exc, vstar, jstar

        _, vstar, jstar = lax.while_loop(
            lambda state: jnp.max(state[0]) > 0, drop_step,
            (excess, tau, jnp.full((qb, LANES), 2 ** 30, I32)))

        def rewrite_tile(t, carry):
            for c in range(chunks_per_tile):
                ci = t * chunks_per_tile + c
                s = sc_ref[ci]
                dropped = (s >= tau) & jnp.logical_not(kept(s, ci * LANES + lane, vstar, jstar))
                sc_ref[ci] = jnp.where(dropped, jnp.nan, s)
            return carry

        lax.fori_loop(0, n_tiles, rewrite_tile, 0)

    def masked_logits(t, kv):
        ks = pl.multiple_of(t * tk, tk)
        bias = jnp.concatenate(
            [jnp.where(sc_ref[t * chunks_per_tile + c] >= tau, 0.0, MASKED_LOGIT)
             for c in range(chunks_per_tile)], axis=1)
        s = _dot_nt(qs_ref[kv], kb_ref[pl.ds(ks, tk), kv * HEAD_DIM:(kv + 1) * HEAD_DIM])
        return (s.reshape(HEADS_PER_KV, qb, tk) + bias[None]).reshape(HEADS_PER_KV * qb, tk)

    def attend_tile(t, carry):
        ks = pl.multiple_of(t * tk, tk)
        for kv in range(N_KV_HEADS):
            s = masked_logits(t, kv)
            m = m_ref[kv]
            l = l_ref[kv]
            ps = []
            for c in range(chunks_per_tile):
                e = jnp.exp2(s[:, c * LANES:(c + 1) * LANES] - m)
                l = l + e
                ps.append(e.astype(BF16))
            l_ref[kv] = l
            p = jnp.concatenate(ps, axis=1)
            acc_ref[kv] += jnp.dot(p, vb_ref[pl.ds(ks, tk), kv * HEAD_DIM:(kv + 1) * HEAD_DIM],
                                   preferred_element_type=F32)
        return carry

    def attend():
        l_ref[...] = jnp.zeros(l_ref.shape, F32)
        acc_ref[...] = jnp.zeros(acc_ref.shape, F32)
        lax.fori_loop(0, n_tiles, attend_tile, 0)

    @pl.when(blk == 0)
    def _key_norms():
        for kv in range(N_KV_HEADS):
            def norm_tile(t, mx):
                ks = pl.multiple_of(t * tk, tk)
                k = kb_ref[pl.ds(ks, tk), kv * HEAD_DIM:(kv + 1) * HEAD_DIM].astype(F32)
                return jnp.maximum(mx, jnp.sum(k * k, axis=1, keepdims=True))

            mx = lax.fori_loop(0, n_keys_pad // tk, norm_tile, jnp.zeros((tk, 1), F32))
            kn_ref[kv] = jnp.broadcast_to(jnp.sqrt(jnp.max(mx, axis=0, keepdims=True)), kn_ref.shape[1:])

    for kv in range(N_KV_HEADS):
        qf = qs_ref[kv].astype(F32)
        qn = jnp.sqrt(jnp.sum(qf * qf, axis=1, keepdims=True))
        m_ref[kv] = jnp.broadcast_to(qn, m_ref.shape[1:]) * kn_ref[kv, 0:1, :]
    attend()

    l_min = jnp.min(jnp.sum(l_ref[...], axis=2))

    @pl.when(jnp.logical_not(l_min >= MIN_ROW_SUM))
    def _exact_shift():
        m_ref[...] = jnp.full(m_ref.shape, MASKED_LOGIT, F32)

        def max_tile(t, carry):
            for kv in range(N_KV_HEADS):
                s = masked_logits(t, kv)
                mx = m_ref[kv]
                for c in range(chunks_per_tile):
                    mx = jnp.maximum(mx, s[:, c * LANES:(c + 1) * LANES])
                m_ref[kv] = mx
            return carry

        lax.fori_loop(0, n_tiles, max_tile, 0)
        for kv in range(N_KV_HEADS):
            m_ref[kv] = jnp.broadcast_to(jnp.max(m_ref[kv], axis=1, keepdims=True), m_ref.shape[1:])
        attend()

    for kv in range(N_KV_HEADS):
        o = acc_ref[kv] / jnp.sum(l_ref[kv], axis=1, keepdims=True)
        for hh in range(HEADS_PER_KV):
            c0 = (kv * HEADS_PER_KV + hh) * HEAD_DIM
            gate = ga_ref[:, c0:c0 + HEAD_DIM].astype(F32)
            o_ref[:, c0:c0 + HEAD_DIM] = (o[hh * qb:(hh + 1) * qb] * gate).astype(BF16)


def _dsa(q, iq, iw, ga, kb, vb, iklo, ikhi, *, qb, causal, n_keys):
    b, s, _ = q.shape
    lk = kb.shape[1]
    tk = KEY_TILE
    assert lk % tk == 0 and s % qb == 0
    topk = min(TOPK_MAX, n_keys // 4)
    kern = functools.partial(_dsa_kernel, qb=qb, tk=tk, topk=topk, causal=causal, n_keys=n_keys,
                             n_keys_pad=lk)

    def q_spec(width):
        return pl.BlockSpec((None, qb, width), lambda bi, i: (bi, i, 0))

    def kv_spec(width):
        return pl.BlockSpec((None, lk, width), lambda bi, i: (bi, 0, 0))

    return pl.pallas_call(
        kern,
        out_shape=jax.ShapeDtypeStruct((b, s, D_ATTN), BF16),
        grid=(b, s // qb),
        in_specs=[q_spec(D_ATTN), q_spec(N_IDX_HEADS * IDX_DIM), q_spec(N_IDX_HEADS), q_spec(D_ATTN),
                  kv_spec(KV_DIM), kv_spec(KV_DIM), kv_spec(LANES), kv_spec(LANES)],
        out_specs=q_spec(D_ATTN),
        scratch_shapes=[
            pltpu.VMEM((lk // LANES, qb, LANES), F32),
            pltpu.VMEM((N_IDX_HEADS, qb, LANES), F32),
            pltpu.VMEM((IDX_PAIRS * qb, LANES), BF16),
            pltpu.VMEM((N_KV_HEADS, HEADS_PER_KV * qb, HEAD_DIM), BF16),
            pltpu.VMEM((N_KV_HEADS, HEADS_PER_KV * qb, LANES), F32),
            pltpu.VMEM((N_KV_HEADS, HEADS_PER_KV * qb, LANES), F32),
            pltpu.VMEM((N_KV_HEADS, HEADS_PER_KV * qb, HEAD_DIM), F32),
            pltpu.VMEM((N_KV_HEADS, 8, LANES), F32),
        ],
        compiler_params=pltpu.CompilerParams(dimension_semantics=("arbitrary", "arbitrary"),
                                             vmem_limit_bytes=V7X_VMEM_LIMIT_BYTES),
        name="dsa",
    )(q, iq, iw, ga, kb, vb, iklo, ikhi)


def _ssm_prep_kernel(lre_c_ref, lim_c_ref, lre_r_ref, lim_r_ref, ldt_ref, btr_ref, bti_ref,
                     ctr_ref, cti_ref, kflat_ref, mtr_ref, mti_ref, emr_ref, emi_ref,
                     a64r_ref, a64i_ref, a16r_ref, a16i_ref):
    hp = lax.Precision.HIGHEST
    dt = jnp.exp(ldt_ref[...])
    n_lag = CHUNK
    width = n_lag * SSM_GROUP

    ldr_c = lre_c_ref[...] * dt
    ldi_c = lim_c_ref[...] * dt
    lag = (lax.broadcasted_iota(I32, (1, width), 1) // SSM_GROUP).astype(F32)
    mag = jnp.exp(ldr_c * lag)
    pr = mag * jnp.cos(ldi_c * lag)
    pi = mag * jnp.sin(ldi_c * lag)
    ctr = ctr_ref[...]
    cti = cti_ref[...]
    qr = pr * ctr - pi * cti
    qi = pr * cti + pi * ctr

    lre_r = lre_r_ref[...]
    lim_r = lim_r_ref[...]
    ldr_r = lre_r * dt
    ldi_r = lim_r * dt
    lbr = jnp.exp(ldr_r) * jnp.cos(ldi_r)
    lbi = jnp.exp(ldr_r) * jnp.sin(ldi_r)
    den = lre_r * lre_r + lim_r * lim_r
    nr = lbr - 1.0
    fr = (nr * lre_r + lbi * lim_r) / den
    fi = (lbi * lre_r - nr * lim_r) / den
    btr = btr_ref[...]
    bti = bti_ref[...]
    bbr = fr * btr - fi * bti
    bbi = fr * bti + fi * btr

    kflat_ref[...] = (jnp.dot(bbr, qr, precision=hp, preferred_element_type=F32)
                      - jnp.dot(bbi, qi, precision=hp, preferred_element_type=F32))

    lbr_c = jnp.exp(ldr_c) * jnp.cos(ldi_c)
    lbi_c = jnp.exp(ldr_c) * jnp.sin(ldi_c)
    emr_ref[...] = (lbr_c * qr - lbi_c * qi).astype(BF16)
    emi_ref[...] = (-(lbr_c * qi + lbi_c * qr)).astype(BF16)

    back = (n_lag - 1 - lax.broadcasted_iota(I32, (n_lag, 1), 0)).astype(F32)
    bmag = jnp.exp(ldr_r * back)
    bpr = bmag * jnp.cos(ldi_r * back)
    bpi = bmag * jnp.sin(ldi_r * back)
    for t in range(n_lag):
        wr = bpr[t:t + 1, :]
        wi = bpi[t:t + 1, :]
        rows = slice(t * SSM_GROUP, (t + 1) * SSM_GROUP)
        mtr_ref[rows, :] = (wr * bbr - wi * bbi).astype(BF16)
        mti_ref[rows, :] = (wr * bbi + wi * bbr).astype(BF16)

    for steps, ar_ref, ai_ref in ((float(CHUNK), a64r_ref, a64i_ref), (16.0, a16r_ref, a16i_ref)):
        amag = jnp.exp(ldr_r * steps)
        ar_ref[...] = amag * jnp.cos(ldi_r * steps)
        ai_ref[...] = amag * jnp.sin(ldi_r * steps)


def _ssm_prep(lambda_re, lambda_im, log_dt, b_re, b_im, c_re, c_im):
    g, p = lambda_re.shape
    width = CHUNK * SSM_GROUP
    lre_c = lambda_re.reshape(g, p, 1)
    lim_c = lambda_im.reshape(g, p, 1)
    lre_r = lambda_re.reshape(g, 1, p)
    lim_r = lambda_im.reshape(g, 1, p)
    ldt = log_dt.reshape(g, 1, 1)
    btr = jnp.swapaxes(b_re, 1, 2)
    bti = jnp.swapaxes(b_im, 1, 2)
    ctr = jnp.tile(jnp.swapaxes(c_re, 1, 2), (1, 1, CHUNK))
    cti = jnp.tile(jnp.swapaxes(c_im, 1, 2), (1, 1, CHUNK))

    def gspec(*shape):
        return pl.BlockSpec((None,) + shape, lambda i: (i,) + (0,) * len(shape))

    out_defs = [((SSM_GROUP, width), F32), ((width, p), BF16), ((width, p), BF16),
                ((p, width), BF16), ((p, width), BF16),
                ((1, p), F32), ((1, p), F32), ((1, p), F32), ((1, p), F32)]
    return pl.pallas_call(
        _ssm_prep_kernel,
        out_shape=[jax.ShapeDtypeStruct((g,) + s, dt) for s, dt in out_defs],
        grid=(g,),
        in_specs=[gspec(p, 1), gspec(p, 1), gspec(1, p), gspec(1, p), gspec(1, 1),
                  gspec(SSM_GROUP, p), gspec(SSM_GROUP, p), gspec(p, width), gspec(p, width)],
        out_specs=[gspec(*s) for s, _ in out_defs],
        compiler_params=pltpu.CompilerParams(dimension_semantics=("arbitrary",)),
        name="ssm_prep",
    )(lre_c, lim_c, lre_r, lim_r, ldt, btr, bti, ctr, cti)


def _ssm_kernel(x_ref, kflat_ref, mtr_ref, mti_ref, emr_ref, emi_ref, ar_ref, ai_ref,
                h0r_ref, h0i_ref, dvec_ref, y_ref, hr_ref, hi_ref,
                toep_ref, gr_ref, gi_ref, hpr_ref, hpi_ref, *, t_len, n_seq, n_chunks):
    width = t_len * SSM_GROUP
    kflat = kflat_ref[:, :width]
    lane = lax.broadcasted_iota(I32, (SSM_GROUP, width), 1)
    toep_ref[0:SSM_GROUP, :] = kflat.astype(BF16)
    for t in range(1, t_len):
        shifted = pltpu.roll(kflat, t * SSM_GROUP, axis=1)
        toep_ref[t * SSM_GROUP:(t + 1) * SSM_GROUP, :] = jnp.where(
            lane >= t * SSM_GROUP, shifted, 0.0).astype(BF16)

    x = x_ref[...]
    gr_ref[...] = jnp.dot(x, mtr_ref[...], preferred_element_type=F32)
    gi_ref[...] = jnp.dot(x, mti_ref[...], preferred_element_type=F32)

    ar = ar_ref[...]
    ai = ai_ref[...]

    def chunk_step(n, carry):
        h_r, h_i = carry
        rows = pl.ds(n, n_seq, stride=n_chunks)
        hpr_ref[rows, :] = h_r
        hpi_ref[rows, :] = h_i
        g_r = gr_ref[rows, :]
        g_i = gi_ref[rows, :]
        return (ar * h_r - ai * h_i + g_r, ar * h_i + ai * h_r + g_i)

    if n_chunks == 1:
        h_r, h_i = chunk_step(0, (h0r_ref[...], h0i_ref[...]))
    else:
        h_r, h_i = lax.fori_loop(0, n_chunks, chunk_step, (h0r_ref[...], h0i_ref[...]))
    hr_ref[...] = h_r
    hi_ref[...] = h_i

    y = jnp.dot(x, toep_ref[...], preferred_element_type=F32)
    y = y + jnp.dot(hpr_ref[...].astype(BF16), emr_ref[:, :width], preferred_element_type=F32)
    y = y + jnp.dot(hpi_ref[...].astype(BF16), emi_ref[:, :width], preferred_element_type=F32)
    y = y + x.astype(F32) * dvec_ref[...]
    y_ref[...] = y.astype(BF16)


def _ssm(xg, prep, h0r, h0i, dvec, *, t_len, n_seq, n_chunks):
    kflat, mtr, mti, emr, emi, a64r, a64i, a16r, a16i = prep
    g, rows, width = xg.shape
    p = SSM_STATE
    full = CHUNK * SSM_GROUP
    if t_len == CHUNK:
        ar, ai = a64r, a64i
        mt_block = 0
    else:
        assert t_len == 16
        ar, ai = a16r, a16i
        mt_block = (full - width) // width
    kern = functools.partial(_ssm_kernel, t_len=t_len, n_seq=n_seq, n_chunks=n_chunks)

    def gspec(*shape):
        return pl.BlockSpec((None,) + shape, lambda i: (i,) + (0,) * len(shape))

    mt_spec = pl.BlockSpec((None, width, p), lambda i: (i, mt_block, 0))
    return pl.pallas_call(
        kern,
        out_shape=[jax.ShapeDtypeStruct((g, rows, width), BF16),
                   jax.ShapeDtypeStruct((g, n_seq, p), F32),
                   jax.ShapeDtypeStruct((g, n_seq, p), F32)],
        grid=(g,),
        in_specs=[gspec(rows, width), gspec(SSM_GROUP, full), mt_spec, mt_spec,
                  gspec(p, full), gspec(p, full), gspec(1, p), gspec(1, p),
                  gspec(n_seq, p), gspec(n_seq, p), gspec(1, width)],
        out_specs=[gspec(rows, width), gspec(n_seq, p), gspec(n_seq, p)],
        scratch_shapes=[pltpu.VMEM((width, width), BF16),
                        pltpu.VMEM((rows, p), F32), pltpu.VMEM((rows, p), F32),
                        pltpu.VMEM((rows, p), F32), pltpu.VMEM((rows, p), F32)],
        compiler_params=pltpu.CompilerParams(dimension_semantics=("arbitrary",)),
        name="ssm",
    )(xg, kflat, mtr, mti, emr, emi, ar, ai, h0r, h0i, dvec)


def _out_kernel(a_ref, ys_ref, gs_ref, x_ref, gate_ref, wglu_ref, bglu_ref, woa_ref, wos_ref,
                gpost_ref, o_ref):
    y = ys_ref[...].astype(F32)
    y = 0.5 * y * (1.0 + jnp.tanh(math.sqrt(2.0 / math.pi) * (y + 0.044715 * (y * y * y))))
    z = jnp.dot(y.astype(BF16), wglu_ref[...], preferred_element_type=F32) + bglu_ref[...]
    y = y * jax.nn.sigmoid(z) * gs_ref[...].astype(F32)
    out = (jnp.dot(a_ref[...], woa_ref[...], preferred_element_type=F32)
           + jnp.dot(y.astype(BF16), wos_ref[...], preferred_element_type=F32))
    ms = jnp.mean(out * out, axis=-1, keepdims=True)
    normed = out * lax.rsqrt(ms + RMS_EPS) * gpost_ref[...]
    o_ref[...] = x_ref[...] + gate_ref[...] * normed


def _out_proj(a, ys, gs, x2d, gate, w_glu, b_glu, w_out, g_post, *, tm, rows_per_mod):
    n, d = x2d.shape
    if rows_per_mod is None:
        mod_spec = pl.BlockSpec((tm, d), lambda i: (i, 0))
    else:
        tiles_per_mod = rows_per_mod // tm
        mod_spec = pl.BlockSpec((None, 1, d), lambda i: (i // tiles_per_mod, 0, 0))

    def row_spec(width):
        return pl.BlockSpec((tm, width), lambda i: (i, 0))

    wglu = w_glu.astype(BF16)
    woa = w_out[:D_ATTN].astype(BF16)
    wos = w_out[D_ATTN:].astype(BF16)
    return pl.pallas_call(
        _out_kernel,
        out_shape=jax.ShapeDtypeStruct((n, d), F32),
        grid=(n // tm,),
        in_specs=[row_spec(D_ATTN), row_spec(D_SSM), row_spec(D_SSM), row_spec(d), mod_spec,
                  _const_spec(wglu.shape), _const_spec((1, D_SSM)), _const_spec(woa.shape),
                  _const_spec(wos.shape), _const_spec((1, d))],
        out_specs=row_spec(d),
        compiler_params=pltpu.CompilerParams(dimension_semantics=("arbitrary",),
                                             vmem_limit_bytes=V7X_VMEM_LIMIT_BYTES),
        name="out_proj",
    )(a, ys, gs, x2d, gate, wglu, b_glu.reshape(1, D_SSM), woa, wos, g_post.reshape(1, d))


def _to_groups(u2d, n_seq, n_chunks, t_len):
    u5 = u2d.reshape(n_seq, n_chunks, t_len, N_SSM_GROUPS, SSM_GROUP)
    return u5.transpose(3, 0, 1, 2, 4).reshape(N_SSM_GROUPS, n_seq * n_chunks, t_len * SSM_GROUP)


def _from_groups(yg, n_seq, n_chunks, t_len):
    y5 = yg.reshape(N_SSM_GROUPS, n_seq, n_chunks, t_len, SSM_GROUP)
    return y5.transpose(1, 2, 3, 0, 4).reshape(n_seq * n_chunks * t_len, D_SSM)


def _pad_rows(a, rows):
    return jnp.pad(a, ((0, 0), (0, rows - a.shape[1]), (0, 0)))


def _layer(x, mod, past, g_pre, g_post, weights, prep, d_skip, w_glu, b_glu, w_out):
    bsz, s, d = x.shape
    n = bsz * s
    x2d = x.reshape(n, d)
    shift, scale, gate = mod[:, :d], mod[:, d:2 * d], mod[:, 2 * d:]
    if past is None:
        tm = 256
        rows_per_mod = s
        mods = [m.reshape(bsz, 1, d) for m in (scale, shift, gate)]
    else:
        tm = n
        rows_per_mod = None
        mods = [jnp.repeat(m, s, axis=0) for m in (scale, shift, gate)]
    scale_m, shift_m, gate_m = mods

    (q, k32, kb, v32, vb, iq, ik32, iklo, ikhi, iw, ga, u, gs) = _proj(
        x2d, scale_m, shift_m, g_pre.reshape(1, d), weights, tm=tm, rows_per_mod=rows_per_mod)

    def seq(a):
        return a.reshape(bsz, s, a.shape[-1])

    if past is None:
        attn = _dsa(seq(q), seq(iq), seq(iw), seq(ga), seq(kb), seq(vb), seq(iklo), seq(ikhi),
                    qb=Q_BLOCK, causal=True, n_keys=s)
        t_len = CHUNK if s % CHUNK == 0 else s
        h0r = jnp.zeros((N_SSM_GROUPS, bsz, SSM_STATE), F32)
        h0i = h0r
    else:
        ck, cv, cik, sre, sim = past
        past_len = ck.shape[1]
        n_keys = past_len + s
        lk = -(-n_keys // KEY_TILE) * KEY_TILE
        k_all = jnp.concatenate([ck.reshape(bsz, past_len, KV_DIM).astype(BF16), seq(kb)], axis=1)
        v_all = jnp.concatenate([cv.reshape(bsz, past_len, KV_DIM).astype(BF16), seq(vb)], axis=1)
        cik_b = cik.astype(BF16)
        zeros = jnp.zeros_like(cik_b)
        iklo_all = jnp.concatenate([jnp.concatenate([cik_b, zeros], axis=-1), seq(iklo)], axis=1)
        ikhi_all = jnp.concatenate([jnp.concatenate([zeros, cik_b], axis=-1), seq(ikhi)], axis=1)
        attn = _dsa(seq(q), seq(iq), seq(iw), seq(ga), _pad_rows(k_all, lk), _pad_rows(v_all, lk),
                    _pad_rows(iklo_all, lk), _pad_rows(ikhi_all, lk),
                    qb=s, causal=False, n_keys=n_keys)
        t_len = CHUNK if s % CHUNK == 0 else s
        h0r = jnp.swapaxes(sre.astype(F32), 0, 1)
        h0i = jnp.swapaxes(sim.astype(F32), 0, 1)

    n_chunks = s // t_len
    xg = _to_groups(u, bsz, n_chunks, t_len)
    dvec = jnp.tile(d_skip.astype(F32).reshape(N_SSM_GROUPS, 1, SSM_GROUP), (1, 1, t_len))
    rows = bsz * n_chunks
    rows_pad = -(-rows // 16) * 16
    xg = _pad_rows(xg, rows_pad) if rows_pad != rows else xg
    if rows_pad != rows:
        assert n_chunks == 1
        h0r = _pad_rows(h0r, rows_pad)
        h0i = _pad_rows(h0i, rows_pad)
    yg, hr, hi = _ssm(xg, prep, h0r, h0i, dvec, t_len=t_len,
                      n_seq=rows_pad // n_chunks, n_chunks=n_chunks)
    ys = _from_groups(yg[:, :rows], bsz, n_chunks, t_len)
    hr = jnp.swapaxes(hr[:, :bsz], 0, 1)
    hi = jnp.swapaxes(hi[:, :bsz], 0, 1)

    y = _out_proj(attn.reshape(n, D_ATTN), ys, gs, x2d, gate_m, w_glu, b_glu, w_out, g_post,
                  tm=tm, rows_per_mod=rows_per_mod)
    k_out = k32.reshape(bsz, s, N_KV_HEADS, HEAD_DIM)
    v_out = v32.reshape(bsz, s, N_KV_HEADS, HEAD_DIM)
    ik_out = ik32.reshape(bsz, s, IDX_DIM)
    return y.reshape(bsz, s, d), (k_out, v_out, ik_out, hr, hi)


def kernel(x_prompt, x_sample, c_prompt, c_sample, cache_k, cache_v, cache_idx_k, state_ssm_re,
           state_ssm_im, w_ada, b_ada, g_pre, g_post, w_in, lambda_re, lambda_im, log_dt, b_re, b_im,
           c_re, c_im, d_skip, w_glu, b_glu, w_out):
    depth = w_ada.shape[0]
    bp = c_prompt.shape[0]
    bs = c_sample.shape[0]
    yp, ys = x_prompt, x_sample
    outs_p = [[] for _ in range(5)]
    outs_s = [[] for _ in range(5)]
    c_all = jnp.concatenate([c_prompt, c_sample], axis=0)
    c_rows = -(-c_all.shape[0] // 8) * 8
    c_all = jnp.pad(c_all, ((0, c_rows - c_all.shape[0]), (0, 0)))
    for l in range(depth):
        mod = _adaln(c_all, w_ada[l], b_ada[l].reshape(1, -1))
        weights = _split_w_in(w_in[l])
        prep = _ssm_prep(lambda_re[l], lambda_im[l], log_dt[l], b_re[l], b_im[l], c_re[l], c_im[l])
        common = (g_pre[l], g_post[l], weights, prep, d_skip[l], w_glu[l], b_glu[l], w_out[l])
        yp, new_p = _layer(yp, mod[:bp], None, *common)
        past = (cache_k[l], cache_v[l], cache_idx_k[l], state_ssm_re[l], state_ssm_im[l])
        ys, new_s = _layer(ys, mod[bp:bp + bs], past, *common)
        for acc, val in zip(outs_p, new_p):
            acc.append(val)
        for acc, val in zip(outs_s, new_s):
            acc.append(val)
    return (yp, ys) + tuple(jnp.stack(a) for a in outs_p) + tuple(jnp.stack(a) for a in outs_s)
```

```python
import functools
import math

import jax
import jax.numpy as jnp
from jax import lax
from jax.experimental import pallas as pl
from jax.experimental.pallas import tpu as pltpu

F32 = jnp.float32
BF16 = jnp.bfloat16
I32 = jnp.int32

CHUNK = 64
Q_BLOCK = 128
D_ATTN = 1024
N_HEADS = 8
N_KV_HEADS = 2
HEAD_DIM = 128
KV_DIM = N_KV_HEADS * HEAD_DIM
HEADS_PER_KV = N_HEADS // N_KV_HEADS
N_IDX_HEADS = 16
IDX_DIM = 64
IDX_PAIRS = N_IDX_HEADS // 2
TOPK_MAX = 256
D_SSM = 1024
SSM_GROUP = 16
N_SSM_GROUPS = D_SSM // SSM_GROUP
SSM_STATE = 64
RMS_EPS = 1e-6
IN_SIZES = (D_ATTN, KV_DIM, KV_DIM, N_IDX_HEADS * IDX_DIM, IDX_DIM, N_IDX_HEADS, D_ATTN, D_SSM, D_SSM)

LANES = 128
V7X_VMEM_LIMIT_BYTES = 56 * 1024 * 1024

INT_MIN = -(2 ** 31)
MASKED_LOGIT = -1e30
MIN_ROW_SUM = 2.0 ** -100
KEY_TILE = 512
COARSE_STEPS = 9
FINE_STEPS = 12


def _const_spec(shape):
    nd = len(shape)
    return pl.BlockSpec(shape, lambda *_: (0,) * nd, pipeline_mode=pl.Buffered(1))


def _dot_nt(a, b):
    return lax.dot_general(a, b, (((1,), (1,)), ((), ())), preferred_element_type=F32)


def _adaln_kernel(c_ref, w_ref, b_ref, o_ref):
    o_ref[...] = jnp.dot(c_ref[...], w_ref[...], preferred_element_type=F32) + b_ref[...]


def _adaln(c, w_ada, b_ada):
    rows, d = c.shape
    n_out = w_ada.shape[1]
    tn = 512
    return pl.pallas_call(
        _adaln_kernel,
        out_shape=jax.ShapeDtypeStruct((rows, n_out), F32),
        grid=(n_out // tn,),
        in_specs=[pl.BlockSpec((rows, d), lambda j: (0, 0)),
                  pl.BlockSpec((d, tn), lambda j: (0, j)),
                  pl.BlockSpec((1, tn), lambda j: (0, j))],
        out_specs=pl.BlockSpec((rows, tn), lambda j: (0, j)),
        compiler_params=pltpu.CompilerParams(dimension_semantics=("arbitrary",)),
        name="adaln",
    )(c, w_ada, b_ada)


def _proj_kernel(x_ref, scale_ref, shift_ref, g_ref,
                 wq_ref, wk_ref, wv_ref, wiq_ref, wik_ref, wiw_ref, wza_ref, wu_ref, wzs_ref,
                 q_ref, k_ref, kb_ref, v_ref, vb_ref, iq_ref, ik_ref, iklo_ref, ikhi_ref,
                 iw_ref, ga_ref, u_ref, gs_ref):
    x = x_ref[...]
    ms = jnp.mean(x * x, axis=-1, keepdims=True)
    y = x * lax.rsqrt(ms + RMS_EPS) * g_ref[...]
    h = (y * (1.0 + scale_ref[...]) + shift_ref[...]).astype(BF16)

    def mm(w_ref):
        return jnp.dot(h, w_ref[...], preferred_element_type=F32)

    q_ref[...] = (mm(wq_ref) * (HEAD_DIM ** -0.5 * math.log2(math.e))).astype(BF16)
    zk = mm(wk_ref)
    k_ref[...] = zk
    kb_ref[...] = zk.astype(BF16)
    zv = mm(wv_ref)
    v_ref[...] = zv
    vb_ref[...] = zv.astype(BF16)
    iq_ref[...] = (mm(wiq_ref) * (IDX_DIM ** -0.5)).astype(BF16)
    zik = mm(wik_ref)
    ik_ref[...] = zik[:, :IDX_DIM]
    lane = lax.broadcasted_iota(I32, zik.shape, 1)
    iklo_ref[...] = jnp.where(lane < IDX_DIM, zik, 0.0).astype(BF16)
    ikhi_ref[...] = jnp.where(lane >= IDX_DIM, zik, 0.0).astype(BF16)
    iw_ref[...] = mm(wiw_ref)[:, :N_IDX_HEADS] * (N_IDX_HEADS ** -0.5)
    za = mm(wza_ref)
    ga_ref[...] = (za * jax.nn.sigmoid(za)).astype(BF16)
    u_ref[...] = mm(wu_ref).astype(BF16)
    zs = mm(wzs_ref)
    gs_ref[...] = (zs * jax.nn.sigmoid(zs)).astype(BF16)


def _split_w_in(w_in):
    offs = [0]
    for s in IN_SIZES:
        offs.append(offs[-1] + s)
    cols = [w_in[:, offs[i]:offs[i + 1]].astype(BF16) for i in range(len(IN_SIZES))]
    wq, wk, wv, wiq, wik, wiw, wza, wu, wzs = cols
    wik2 = jnp.concatenate([wik, wik], axis=1)
    wiw_p = jnp.pad(wiw, ((0, 0), (0, LANES - N_IDX_HEADS)))
    return (wq, wk, wv, wiq, wik2, wiw_p, wza, wu, wzs)


def _proj(x2d, scale, shift, g_pre, weights, *, tm, rows_per_mod):
    n, d = x2d.shape
    if rows_per_mod is None:
        mod_spec = pl.BlockSpec((tm, d), lambda i: (i, 0))
    else:
        tiles_per_mod = rows_per_mod // tm
        mod_spec = pl.BlockSpec((None, 1, d), lambda i: (i // tiles_per_mod, 0, 0))

    def row_spec(width):
        return pl.BlockSpec((tm, width), lambda i: (i, 0))

    out_defs = [
        (D_ATTN, BF16), (KV_DIM, F32), (KV_DIM, BF16), (KV_DIM, F32), (KV_DIM, BF16),
        (N_IDX_HEADS * IDX_DIM, BF16), (IDX_DIM, F32), (LANES, BF16), (LANES, BF16),
        (N_IDX_HEADS, F32), (D_ATTN, BF16), (D_SSM, BF16), (D_SSM, BF16)]
    return pl.pallas_call(
        _proj_kernel,
        out_shape=[jax.ShapeDtypeStruct((n, w), dt) for w, dt in out_defs],
        grid=(n // tm,),
        in_specs=[row_spec(d), mod_spec, mod_spec, _const_spec((1, d))]
                 + [_const_spec(w.shape) for w in weights],
        out_specs=[row_spec(w) for w, _ in out_defs],
        compiler_params=pltpu.CompilerParams(dimension_semantics=("arbitrary",),
                                             vmem_limit_bytes=V7X_VMEM_LIMIT_BYTES),
        name="in_proj",
    )(x2d, scale, shift, g_pre, *weights)


def _dsa_kernel(q_ref, iq_ref, iw_ref, ga_ref, kb_ref, vb_ref, iklo_ref, ikhi_ref, o_ref,
                sc_ref, wb_ref, iqs_ref, qs_ref, m_ref, l_ref, acc_ref, kn_ref,
                *, qb, tk, topk, causal, n_keys, n_keys_pad):
    blk = pl.program_id(1)
    chunks_per_tile = tk // LANES
    row = lax.broadcasted_iota(I32, (qb, LANES), 0)
    if causal:
        n_vis = (blk + 1) * qb
        limit = (((blk * qb + row) // CHUNK) + 1) * CHUNK
        n_tiles = (n_vis + tk - 1) // tk
    else:
        limit = jnp.full((qb, LANES), n_keys, I32)
        n_tiles = (n_keys + tk - 1) // tk

    for j in range(IDX_PAIRS):
        iqs_ref[j * qb:(j + 1) * qb, :] = iq_ref[:, j * LANES:(j + 1) * LANES]
    for kv in range(N_KV_HEADS):
        for hh in range(HEADS_PER_KV):
            c0 = (kv * HEADS_PER_KV + hh) * HEAD_DIM
            qs_ref[kv, hh * qb:(hh + 1) * qb, :] = q_ref[:, c0:c0 + HEAD_DIM]
    iw = iw_ref[...]
    for h in range(N_IDX_HEADS):
        wb_ref[h] = jnp.broadcast_to(iw[:, h:h + 1], (qb, LANES))

    lane = lax.broadcasted_iota(I32, (qb, LANES), 1)

    def score_tile(t, carry):
        m1, m2 = carry
        ks = pl.multiple_of(t * tk, tk)
        iqs = iqs_ref[...]
        s_lo = _dot_nt(iqs, iklo_ref[pl.ds(ks, tk), :])
        s_hi = _dot_nt(iqs, ikhi_ref[pl.ds(ks, tk), :])
        for c in range(chunks_per_tile):
            cs = slice(c * LANES, (c + 1) * LANES)
            acc = None
            for j in range(IDX_PAIRS):
                rs = slice(j * qb, (j + 1) * qb)
                term = (jnp.maximum(s_lo[rs, cs], 0.0) * wb_ref[2 * j]
                        + jnp.maximum(s_hi[rs, cs], 0.0) * wb_ref[2 * j + 1])
                acc = term if acc is None else acc + term
            visible = (ks + c * LANES + lane) < limit
            sc_ref[t * chunks_per_tile + c] = jnp.where(visible, acc, jnp.nan)
            v = jnp.where(visible, acc, -jnp.inf)
            m1, v = jnp.maximum(m1, v), jnp.minimum(m1, v)
            m2 = jnp.maximum(m2, v)
        return m1, m2

    ninf = jnp.full((qb, LANES), -jnp.inf, F32)
    m1, m2 = lax.fori_loop(0, n_tiles, score_tile, (ninf, ninf))

    def lane_fold(x, op):
        return jnp.broadcast_to(op(x, axis=1, keepdims=True), (qb, LANES))

    def count_rows(pred):
        def count_tile(t, cnt):
            for c in range(chunks_per_tile):
                ci = t * chunks_per_tile + c
                cnt = cnt + jnp.where(pred(sc_ref[ci], ci * LANES + lane), 1.0, 0.0)
            return cnt

        return lane_fold(lax.fori_loop(0, n_tiles, count_tile, jnp.zeros((qb, LANES), F32)), jnp.sum)

    def key_bisection():
        def ordered_to_f32(key):
            return lax.bitcast_convert_type(jnp.where(key < 0, key ^ jnp.int32(0x7FFFFFFF), key), F32)

        def bisect(i, carry):
            lo, cnt_lo = carry
            cand_key = lo + lax.shift_left(jnp.int32(1), 31 - i)
            cand = ordered_to_f32(cand_key)
            total = count_rows(lambda s, _: s >= cand)
            take = total >= topk
            return jnp.where(take, cand_key, lo), jnp.where(take, total, cnt_lo)

        lo, cnt_lo = lax.fori_loop(0, 32, bisect, (jnp.full((qb, LANES), INT_MIN, I32),
                                                   jnp.zeros((qb, LANES), F32)))
        return ordered_to_f32(lo), cnt_lo

    few = limit < topk

    bracketed = limit >= 2 * LANES
    top = m2 if topk > LANES else m1
    lo_f = jnp.where(bracketed, lane_fold(m2, jnp.min), 0.0)
    hi_f = jnp.where(bracketed, lane_fold(top, jnp.max), 0.0)
    hi_f = hi_f + (jnp.abs(hi_f) * 2.0 ** -20 + 1e-30)

    def halve(_, carry):
        lo_f, hi_f = carry
        mid = lo_f + (hi_f - lo_f) * 0.5
        take = count_rows(lambda s, _: s >= mid) >= topk
        return jnp.where(take, mid, lo_f), jnp.where(take, hi_f, mid)

    lo_f, hi_f = lax.fori_loop(0, COARSE_STEPS, halve, (lo_f, hi_f))

    def bucket_tile(t, carry):
        b1, b2, b3, pop, above = carry
        for c in range(chunks_per_tile):
            s = sc_ref[t * chunks_per_tile + c]
            ge_hi = s >= hi_f
            inb = (s >= lo_f) & jnp.logical_not(ge_hi)
            v = jnp.where(inb, s, -jnp.inf)
            pop = pop + jnp.where(inb, 1.0, 0.0)
            above = above + jnp.where(ge_hi, 1.0, 0.0)
            b1, v = jnp.maximum(b1, v), jnp.minimum(b1, v)
            b2, v = jnp.maximum(b2, v), jnp.minimum(b2, v)
            b3 = jnp.maximum(b3, v)
        return b1, b2, b3, pop, above

    zeros = jnp.zeros((qb, LANES), F32)
    b1, b2, b3, pop, above = lax.fori_loop(0, n_tiles, bucket_tile, (ninf, ninf, ninf, zeros, zeros))
    n_above = lane_fold(above, jnp.sum)
    want = topk - n_above

    transposed = qb == LANES
    if transposed:
        cands = jnp.concatenate([b1.T, b2.T, b3.T], axis=0)
        want_q = want.T[:1]
        axis = 0
    else:
        cands = jnp.concatenate([b1, b2, b3], axis=1)
        want_q = want[:, :1]
        axis = 1

    def walk(_, carry):
        prev, covered, tau_q, cnt_q = carry
        cur = jnp.max(jnp.where(cands < prev, cands, -jnp.inf), axis=axis, keepdims=True)
        covered_new = covered + jnp.sum(jnp.where(cands == cur, 1.0, 0.0), axis=axis, keepdims=True)
        found = (covered < want_q) & (covered_new >= want_q) & (cur > -jnp.inf)
        return cur, covered_new, jnp.where(found, cur, tau_q), jnp.where(found, covered_new, cnt_q)

    nan_q = jnp.full(want_q.shape, jnp.nan, F32)
    _, _, tau_q, cnt_q = lax.fori_loop(
        0, FINE_STEPS, walk,
        (jnp.full(want_q.shape, jnp.inf, F32), jnp.zeros(want_q.shape, F32), nan_q, nan_q))
    if transposed:
        tau_fast = jnp.broadcast_to(tau_q, (LANES, qb)).T
        cnt_fast = n_above + jnp.broadcast_to(cnt_q, (LANES, qb)).T
    else:
        tau_fast = jnp.broadcast_to(tau_q, (qb, LANES))
        cnt_fast = n_above + jnp.broadcast_to(cnt_q, (qb, LANES))

    bad = jnp.logical_not(few) & (jnp.logical_not(bracketed) | (pop > 3.0) | (want < 1.0)
                                  | jnp.logical_not(tau_fast == tau_fast))
    tau, cnt_tau = lax.cond(jnp.max(jnp.where(bad, 1.0, 0.0)) > 0.0, key_bisection,
                            lambda: (tau_fast, cnt_fast))
    tau = jnp.where(few, -jnp.inf, tau)
    excess = jnp.where(few, 0.0, cnt_tau - topk)

    @pl.when(jnp.max(excess) > 0)
    def _resolve_surplus():
        idx_bits = (n_keys_pad - 1).bit_length()

        def kept(s, kidx, vstar, jstar):
            return (s >= tau) & ((s > vstar) | ((s == vstar) & (kidx < jstar)))

        def drop_step(state):
            exc, vstar, jstar = state

            def min_tile(t, vm):
                for c in range(chunks_per_tile):
                    ci = t * chunks_per_tile + c
                    s = sc_ref[ci]
                    vm = jnp.minimum(vm, jnp.where(kept(s, ci * LANES + lane, vstar, jstar), s, jnp.inf))
                return vm

            vm = lax.fori_loop(0, n_tiles, min_tile, jnp.full((qb, LANES), jnp.inf, F32))
            vmin = jnp.broadcast_to(jnp.min(vm, axis=1, keepdims=True), (qb, LANES))

            def tied(s, kidx):
                return kept(s, kidx, vstar, jstar) & (s == vmin)

            cmin = count_rows(tied)
            active = exc > 0
            drop_all = active & (exc >= cmin)
            partial = active & (exc < cmin)
            keep = cmin - exc

            def idx_bisect(b, j0):
                cand_j = j0 + lax.shift_left(jnp.int32(1), idx_bits - 1 - b)
                below = count_rows(lambda s, kidx: tied(s, kidx) & (kidx < cand_j))
                return jnp.where(below < keep, cand_j, j0)

            j0 = lax.fori_loop(0, idx_bits, idx_bisect, jnp.zeros((qb, LANES), I32))
            vstar = jnp.where(active, vmin, vstar)
            jstar = jnp.where(drop_all, 0, jnp.where(partial, j0 + 1, jstar))
            exc = jnp.where(drop_all, exc - cmin, jnp.where(partial, 0, exc))
            return exc, vstar, jstar

        _, vstar, jstar = lax.while_loop(
            lambda state: jnp.max(state[0]) > 0, drop_step,
            (excess, tau, jnp.full((qb, LANES), 2 ** 30, I32)))

        def rewrite_tile(t, carry):
            for c in range(chunks_per_tile):
                ci = t * chunks_per_tile + c
                s = sc_ref[ci]
                dropped = (s >= tau) & jnp.logical_not(kept(s, ci * LANES + lane, vstar, jstar))
                sc_ref[ci] = jnp.where(dropped, jnp.nan, s)
            return carry

        lax.fori_loop(0, n_tiles, rewrite_tile, 0)

    def masked_logits(t, kv):
        ks = pl.multiple_of(t * tk, tk)
        bias = jnp.concatenate(
            [jnp.where(sc_ref[t * chunks_per_tile + c] >= tau, 0.0, MASKED_LOGIT)
             for c in range(chunks_per_tile)], axis=1)
        s = _dot_nt(qs_ref[kv], kb_ref[pl.ds(ks, tk), kv * HEAD_DIM:(kv + 1) * HEAD_DIM])
        return (s.reshape(HEADS_PER_KV, qb, tk) + bias[None]).reshape(HEADS_PER_KV * qb, tk)

    def attend_tile(t, carry):
        ks = pl.multiple_of(t * tk, tk)
        for kv in range(N_KV_HEADS):
            s = masked_logits(t, kv)
            m = m_ref[kv]
            l = l_ref[kv]
            ps = []
            for c in range(chunks_per_tile):
                e = jnp.exp2(s[:, c * LANES:(c + 1) * LANES] - m)
                l = l + e
                ps.append(e.astype(BF16))
            l_ref[kv] = l
            p = jnp.concatenate(ps, axis=1)
            acc_ref[kv] += jnp.dot(p, vb_ref[pl.ds(ks, tk), kv * HEAD_DIM:(kv + 1) * HEAD_DIM],
                                   preferred_element_type=F32)
        return carry

    def attend():
        l_ref[...] = jnp.zeros(l_ref.shape, F32)
        acc_ref[...] = jnp.zeros(acc_ref.shape, F32)
        lax.fori_loop(0, n_tiles, attend_tile, 0)

    @pl.when(blk == 0)
    def _key_norms():
        for kv in range(N_KV_HEADS):
            def norm_tile(t, mx):
                ks = pl.multiple_of(t * tk, tk)
                k = kb_ref[pl.ds(ks, tk), kv * HEAD_DIM:(kv + 1) * HEAD_DIM].astype(F32)
                return jnp.maximum(mx, jnp.sum(k * k, axis=1, keepdims=True))

            mx = lax.fori_loop(0, n_keys_pad // tk, norm_tile, jnp.zeros((tk, 1), F32))
            kn_ref[kv] = jnp.broadcast_to(jnp.sqrt(jnp.max(mx, axis=0, keepdims=True)), kn_ref.shape[1:])

    for kv in range(N_KV_HEADS):
        qf = qs_ref[kv].astype(F32)
        qn = jnp.sqrt(jnp.sum(qf * qf, axis=1, keepdims=True))
        m_ref[kv] = jnp.broadcast_to(qn, m_ref.shape[1:]) * kn_ref[kv, 0:1, :]
    attend()

    l_min = jnp.min(jnp.sum(l_ref[...], axis=2))

    @pl.when(jnp.logical_not(l_min >= MIN_ROW_SUM))
    def _exact_shift():
        m_ref[...] = jnp.full(m_ref.shape, MASKED_LOGIT, F32)

        def max_tile(t, carry):
            for kv in range(N_KV_HEADS):
                s = masked_logits(t, kv)
                mx = m_ref[kv]
                for c in range(chunks_per_tile):
                    mx = jnp.maximum(mx, s[:, c * LANES:(c + 1) * LANES])
                m_ref[kv] = mx
            return carry

        lax.fori_loop(0, n_tiles, max_tile, 0)
        for kv in range(N_KV_HEADS):
            m_ref[kv] = jnp.broadcast_to(jnp.max(m_ref[kv], axis=1, keepdims=True), m_ref.shape[1:])
        attend()

    for kv in range(N_KV_HEADS):
        o = acc_ref[kv] / jnp.sum(l_ref[kv], axis=1, keepdims=True)
        for hh in range(HEADS_PER_KV):
            c0 = (kv * HEADS_PER_KV + hh) * HEAD_DIM
            gate = ga_ref[:, c0:c0 + HEAD_DIM].astype(F32)
            o_ref[:, c0:c0 + HEAD_DIM] = (o[hh * qb:(hh + 1) * qb] * gate).astype(BF16)


def _dsa(q, iq, iw, ga, kb, vb, iklo, ikhi, *, qb, causal, n_keys):
    b, s, _ = q.shape
    lk = kb.shape[1]
    tk = KEY_TILE
    assert lk % tk == 0 and s % qb == 0
    topk = min(TOPK_MAX, n_keys // 4)
    kern = functools.partial(_dsa_kernel, qb=qb, tk=tk, topk=topk, causal=causal, n_keys=n_keys,
                             n_keys_pad=lk)

    def q_spec(width):
        return pl.BlockSpec((None, qb, width), lambda bi, i: (bi, i, 0))

    def kv_spec(width):
        return pl.BlockSpec((None, lk, width), lambda bi, i: (bi, 0, 0))

    return pl.pallas_call(
        kern,
        out_shape=jax.ShapeDtypeStruct((b, s, D_ATTN), BF16),
        grid=(b, s // qb),
        in_specs=[q_spec(D_ATTN), q_spec(N_IDX_HEADS * IDX_DIM), q_spec(N_IDX_HEADS), q_spec(D_ATTN),
                  kv_spec(KV_DIM), kv_spec(KV_DIM), kv_spec(LANES), kv_spec(LANES)],
        out_specs=q_spec(D_ATTN),
        scratch_shapes=[
            pltpu.VMEM((lk // LANES, qb, LANES), F32),
            pltpu.VMEM((N_IDX_HEADS, qb, LANES), F32),
            pltpu.VMEM((IDX_PAIRS * qb, LANES), BF16),
            pltpu.VMEM((N_KV_HEADS, HEADS_PER_KV * qb, HEAD_DIM), BF16),
            pltpu.VMEM((N_KV_HEADS, HEADS_PER_KV * qb, LANES), F32),
            pltpu.VMEM((N_KV_HEADS, HEADS_PER_KV * qb, LANES), F32),
            pltpu.VMEM((N_KV_HEADS, HEADS_PER_KV * qb, HEAD_DIM), F32),
            pltpu.VMEM((N_KV_HEADS, 8, LANES), F32),
        ],
        compiler_params=pltpu.CompilerParams(dimension_semantics=("arbitrary", "arbitrary"),
                                             vmem_limit_bytes=V7X_VMEM_LIMIT_BYTES),
        name="dsa",
    )(q, iq, iw, ga, kb, vb, iklo, ikhi)


def _ssm_prep_kernel(lre_c_ref, lim_c_ref, lre_r_ref, lim_r_ref, ldt_ref, btr_ref, bti_ref,
                     ctr_ref, cti_ref, kflat_ref, mtr_ref, mti_ref, emr_ref, emi_ref,
                     a64r_ref, a64i_ref, a16r_ref, a16i_ref):
    hp = lax.Precision.HIGHEST
    dt = jnp.exp(ldt_ref[...])
    n_lag = CHUNK
    width = n_lag * SSM_GROUP

    ldr_c = lre_c_ref[...] * dt
    ldi_c = lim_c_ref[...] * dt
    lag = lax.broadcasted_iota(I32, (1, n_lag), 1).astype(F32)
    mag = jnp.exp(ldr_c * lag)
    pr_lag = mag * jnp.cos(ldi_c * lag)
    pi_lag = mag * jnp.sin(ldi_c * lag)
    col = lax.broadcasted_iota(I32, (n_lag, width), 1)
    pick_lag = (col // SSM_GROUP == lax.broadcasted_iota(I32, (n_lag, width), 0)).astype(F32)
    col = lax.broadcasted_iota(I32, (SSM_GROUP, width), 1)
    pick_ch = (col % SSM_GROUP == lax.broadcasted_iota(I32, (SSM_GROUP, width), 0)).astype(F32)
    pr = jnp.dot(pr_lag, pick_lag, precision=hp, preferred_element_type=F32)
    pi = jnp.dot(pi_lag, pick_lag, precision=hp, preferred_element_type=F32)
    ctr = jnp.dot(ctr_ref[...], pick_ch, precision=hp, preferred_element_type=F32)
    cti = jnp.dot(cti_ref[...], pick_ch, precision=hp, preferred_element_type=F32)
    qr = pr * ctr - pi * cti
    qi = pr * cti + pi * ctr

    lre_r = lre_r_ref[...]
    lim_r = lim_r_ref[...]
    ldr_r = lre_r * dt
    ldi_r = lim_r * dt
    lbr = jnp.exp(ldr_r) * jnp.cos(ldi_r)
    lbi = jnp.exp(ldr_r) * jnp.sin(ldi_r)
    den = lre_r * lre_r + lim_r * lim_r
    nr = lbr - 1.0
    fr = (nr * lre_r + lbi * lim_r) / den
    fi = (lbi * lre_r - nr * lim_r) / den
    btr = btr_ref[...]
    bti = bti_ref[...]
    bbr = fr * btr - fi * bti
    bbi = fr * bti + fi * btr

    kflat_ref[...] = (jnp.dot(bbr, qr, precision=hp, preferred_element_type=F32)
                      - jnp.dot(bbi, qi, precision=hp, preferred_element_type=F32))

    lbr_c = jnp.exp(ldr_c) * jnp.cos(ldi_c)
    lbi_c = jnp.exp(ldr_c) * jnp.sin(ldi_c)
    emr_ref[...] = (lbr_c * qr - lbi_c * qi).astype(BF16)
    emi_ref[...] = (-(lbr_c * qi + lbi_c * qr)).astype(BF16)

    back = (n_lag - 1 - lax.broadcasted_iota(I32, (n_lag, 1), 0)).astype(F32)
    bmag = jnp.exp(ldr_r * back)
    bpr = bmag * jnp.cos(ldi_r * back)
    bpi = bmag * jnp.sin(ldi_r * back)
    for t in range(n_lag):
        wr = bpr[t:t + 1, :]
        wi = bpi[t:t + 1, :]
        rows = slice(t * SSM_GROUP, (t + 1) * SSM_GROUP)
        mtr_ref[rows, :] = (wr * bbr - wi * bbi).astype(BF16)
        mti_ref[rows, :] = (wr * bbi + wi * bbr).astype(BF16)

    for steps, ar_ref, ai_ref in ((float(CHUNK), a64r_ref, a64i_ref), (16.0, a16r_ref, a16i_ref)):
        amag = jnp.exp(ldr_r * steps)
        ar_ref[...] = amag * jnp.cos(ldi_r * steps)
        ai_ref[...] = amag * jnp.sin(ldi_r * steps)


def _ssm_prep(lambda_re, lambda_im, log_dt, b_re, b_im, c_re, c_im):
    g, p = lambda_re.shape
    width = CHUNK * SSM_GROUP
    lre_c = lambda_re.reshape(g, p, 1)
    lim_c = lambda_im.reshape(g, p, 1)
    lre_r = lambda_re.reshape(g, 1, p)
    lim_r = lambda_im.reshape(g, 1, p)
    ldt = log_dt.reshape(g, 1, 1)
    btr = jnp.swapaxes(b_re, 1, 2)
    bti = jnp.swapaxes(b_im, 1, 2)
    ctr = jnp.swapaxes(c_re, 1, 2)
    cti = jnp.swapaxes(c_im, 1, 2)

    def gspec(*shape):
        return pl.BlockSpec((None,) + shape, lambda i: (i,) + (0,) * len(shape))

    out_defs = [((SSM_GROUP, width), F32), ((width, p), BF16), ((width, p), BF16),
                ((p, width), BF16), ((p, width), BF16),
                ((1, p), F32), ((1, p), F32), ((1, p), F32), ((1, p), F32)]
    return pl.pallas_call(
        _ssm_prep_kernel,
        out_shape=[jax.ShapeDtypeStruct((g,) + s, dt) for s, dt in out_defs],
        grid=(g,),
        in_specs=[gspec(p, 1), gspec(p, 1), gspec(1, p), gspec(1, p), gspec(1, 1),
                  gspec(SSM_GROUP, p), gspec(SSM_GROUP, p), gspec(p, SSM_GROUP), gspec(p, SSM_GROUP)],
        out_specs=[gspec(*s) for s, _ in out_defs],
        compiler_params=pltpu.CompilerParams(dimension_semantics=("arbitrary",)),
        name="ssm_prep",
    )(lre_c, lim_c, lre_r, lim_r, ldt, btr, bti, ctr, cti)


def _ssm_kernel(x_ref, kflat_ref, mtr_ref, mti_ref, emr_ref, emi_ref, ar_ref, ai_ref,
                h0r_ref, h0i_ref, dvec_ref, y_ref, hr_ref, hi_ref,
                toep_ref, gr_ref, gi_ref, hpr_ref, hpi_ref, *, t_len, n_seq, n_chunks):
    width = t_len * SSM_GROUP
    kflat = kflat_ref[:, :width]
    lane = lax.broadcasted_iota(I32, (SSM_GROUP, width), 1)
    toep_ref[0:SSM_GROUP, :] = kflat.astype(BF16)
    for t in range(1, t_len):
        shifted = pltpu.roll(kflat, t * SSM_GROUP, axis=1)
        toep_ref[t * SSM_GROUP:(t + 1) * SSM_GROUP, :] = jnp.where(
            lane >= t * SSM_GROUP, shifted, 0.0).astype(BF16)

    x = x_ref[...]
    gr_ref[...] = jnp.dot(x, mtr_ref[...], preferred_element_type=F32)
    gi_ref[...] = jnp.dot(x, mti_ref[...], preferred_element_type=F32)

    ar = ar_ref[...]
    ai = ai_ref[...]

    def chunk_step(n, carry):
        h_r, h_i = carry
        rows = pl.ds(n, n_seq, stride=n_chunks)
        hpr_ref[rows, :] = h_r
        hpi_ref[rows, :] = h_i
        g_r = gr_ref[rows, :]
        g_i = gi_ref[rows, :]
        return (ar * h_r - ai * h_i + g_r, ar * h_i + ai * h_r + g_i)

    if n_chunks == 1:
        h_r, h_i = chunk_step(0, (h0r_ref[...], h0i_ref[...]))
    else:
        h_r, h_i = lax.fori_loop(0, n_chunks, chunk_step, (h0r_ref[...], h0i_ref[...]))
    hr_ref[...] = h_r
    hi_ref[...] = h_i

    y = jnp.dot(x, toep_ref[...], preferred_element_type=F32)
    y = y + jnp.dot(hpr_ref[...].astype(BF16), emr_ref[:, :width], preferred_element_type=F32)
    y = y + jnp.dot(hpi_ref[...].astype(BF16), emi_ref[:, :width], preferred_element_type=F32)
    y = y + x.astype(F32) * dvec_ref[...]
    y_ref[...] = y.astype(BF16)


def _ssm(xg, prep, h0r, h0i, dvec, *, t_len, n_seq, n_chunks):
    kflat, mtr, mti, emr, emi, a64r, a64i, a16r, a16i = prep
    g, rows, width = xg.shape
    p = SSM_STATE
    full = CHUNK * SSM_GROUP
    if t_len == CHUNK:
        ar, ai = a64r, a64i
        mt_block = 0
    else:
        assert t_len == 16
        ar, ai = a16r, a16i
        mt_block = (full - width) // width
    kern = functools.partial(_ssm_kernel, t_len=t_len, n_seq=n_seq, n_chunks=n_chunks)

    def gspec(*shape):
        return pl.BlockSpec((None,) + shape, lambda i: (i,) + (0,) * len(shape))

    mt_spec = pl.BlockSpec((None, width, p), lambda i: (i, mt_block, 0))
    return pl.pallas_call(
        kern,
        out_shape=[jax.ShapeDtypeStruct((g, rows, width), BF16),
                   jax.ShapeDtypeStruct((g, n_seq, p), F32),
                   jax.ShapeDtypeStruct((g, n_seq, p), F32)],
        grid=(g,),
        in_specs=[gspec(rows, width), gspec(SSM_GROUP, full), mt_spec, mt_spec,
                  gspec(p, full), gspec(p, full), gspec(1, p), gspec(1, p),
                  gspec(n_seq, p), gspec(n_seq, p), gspec(1, width)],
        out_specs=[gspec(rows, width), gspec(n_seq, p), gspec(n_seq, p)],
        scratch_shapes=[pltpu.VMEM((width, width), BF16),
                        pltpu.VMEM((rows, p), F32), pltpu.VMEM((rows, p), F32),
                        pltpu.VMEM((rows, p), F32), pltpu.VMEM((rows, p), F32)],
        compiler_params=pltpu.CompilerParams(dimension_semantics=("arbitrary",)),
        name="ssm",
    )(xg, kflat, mtr, mti, emr, emi, ar, ai, h0r, h0i, dvec)


def _out_kernel(a_ref, ys_ref, gs_ref, x_ref, gate_ref, wglu_ref, bglu_ref, woa_ref, wos_ref,
                gpost_ref, o_ref):
    y = ys_ref[...].astype(F32)
    y = 0.5 * y * (1.0 + jnp.tanh(math.sqrt(2.0 / math.pi) * (y + 0.044715 * (y * y * y))))
    z = jnp.dot(y.astype(BF16), wglu_ref[...], preferred_element_type=F32) + bglu_ref[...]
    y = y * jax.nn.sigmoid(z) * gs_ref[...].astype(F32)
    out = (jnp.dot(a_ref[...], woa_ref[...], preferred_element_type=F32)
           + jnp.dot(y.astype(BF16), wos_ref[...], preferred_element_type=F32))
    ms = jnp.mean(out * out, axis=-1, keepdims=True)
    normed = out * lax.rsqrt(ms + RMS_EPS) * gpost_ref[...]
    o_ref[...] = x_ref[...] + gate_ref[...] * normed


def _out_proj(a, ys, gs, x2d, gate, w_glu, b_glu, w_out, g_post, *, tm, rows_per_mod):
    n, d = x2d.shape
    if rows_per_mod is None:
        mod_spec = pl.BlockSpec((tm, d), lambda i: (i, 0))
    else:
        tiles_per_mod = rows_per_mod // tm
        mod_spec = pl.BlockSpec((None, 1, d), lambda i: (i // tiles_per_mod, 0, 0))

    def row_spec(width):
        return pl.BlockSpec((tm, width), lambda i: (i, 0))

    wglu = w_glu.astype(BF16)
    woa = w_out[:D_ATTN].astype(BF16)
    wos = w_out[D_ATTN:].astype(BF16)
    return pl.pallas_call(
        _out_kernel,
        out_shape=jax.ShapeDtypeStruct((n, d), F32),
        grid=(n // tm,),
        in_specs=[row_spec(D_ATTN), row_spec(D_SSM), row_spec(D_SSM), row_spec(d), mod_spec,
                  _const_spec(wglu.shape), _const_spec((1, D_SSM)), _const_spec(woa.shape),
                  _const_spec(wos.shape), _const_spec((1, d))],
        out_specs=row_spec(d),
        compiler_params=pltpu.CompilerParams(dimension_semantics=("arbitrary",),
                                             vmem_limit_bytes=V7X_VMEM_LIMIT_BYTES),
        name="out_proj",
    )(a, ys, gs, x2d, gate, wglu, b_glu.reshape(1, D_SSM), woa, wos, g_post.reshape(1, d))


def _to_groups(u2d, n_seq, n_chunks, t_len):
    u5 = u2d.reshape(n_seq, n_chunks, t_len, N_SSM_GROUPS, SSM_GROUP)
    return u5.transpose(3, 0, 1, 2, 4).reshape(N_SSM_GROUPS, n_seq * n_chunks, t_len * SSM_GROUP)


def _from_groups(yg, n_seq, n_chunks, t_len):
    y5 = yg.reshape(N_SSM_GROUPS, n_seq, n_chunks, t_len, SSM_GROUP)
    return y5.transpose(1, 2, 3, 0, 4).reshape(n_seq * n_chunks * t_len, D_SSM)


def _pad_rows(a, rows):
    return jnp.pad(a, ((0, 0), (0, rows - a.shape[1]), (0, 0)))


def _layer(x, mod, past, g_pre, g_post, weights, prep, d_skip, w_glu, b_glu, w_out):
    bsz, s, d = x.shape
    n = bsz * s
    x2d = x.reshape(n, d)
    shift, scale, gate = mod[:, :d], mod[:, d:2 * d], mod[:, 2 * d:]
    if past is None:
        tm = 256
        rows_per_mod = s
        mods = [m.reshape(bsz, 1, d) for m in (scale, shift, gate)]
    else:
        tm = n
        rows_per_mod = None
        mods = [jnp.repeat(m, s, axis=0) for m in (scale, shift, gate)]
    scale_m, shift_m, gate_m = mods

    (q, k32, kb, v32, vb, iq, ik32, iklo, ikhi, iw, ga, u, gs) = _proj(
        x2d, scale_m, shift_m, g_pre.reshape(1, d), weights, tm=tm, rows_per_mod=rows_per_mod)

    def seq(a):
        return a.reshape(bsz, s, a.shape[-1])

    if past is None:
        attn = _dsa(seq(q), seq(iq), seq(iw), seq(ga), seq(kb), seq(vb), seq(iklo), seq(ikhi),
                    qb=Q_BLOCK, causal=True, n_keys=s)
        t_len = CHUNK if s % CHUNK == 0 else s
        h0r = jnp.zeros((N_SSM_GROUPS, bsz, SSM_STATE), F32)
        h0i = h0r
    else:
        ck, cv, cik, sre, sim = past
        past_len = ck.shape[1]
        n_keys = past_len + s
        lk = -(-n_keys // KEY_TILE) * KEY_TILE
        k_all = jnp.concatenate([ck.reshape(bsz, past_len, KV_DIM).astype(BF16), seq(kb)], axis=1)
        v_all = jnp.concatenate([cv.reshape(bsz, past_len, KV_DIM).astype(BF16), seq(vb)], axis=1)
        cik_b = cik.astype(BF16)
        zeros = jnp.zeros_like(cik_b)
        iklo_all = jnp.concatenate([jnp.concatenate([cik_b, zeros], axis=-1), seq(iklo)], axis=1)
        ikhi_all = jnp.concatenate([jnp.concatenate([zeros, cik_b], axis=-1), seq(ikhi)], axis=1)
        attn = _dsa(seq(q), seq(iq), seq(iw), seq(ga), _pad_rows(k_all, lk), _pad_rows(v_all, lk),
                    _pad_rows(iklo_all, lk), _pad_rows(ikhi_all, lk),
                    qb=s, causal=False, n_keys=n_keys)
        t_len = CHUNK if s % CHUNK == 0 else s
        h0r = jnp.swapaxes(sre.astype(F32), 0, 1)
        h0i = jnp.swapaxes(sim.astype(F32), 0, 1)

    n_chunks = s // t_len
    xg = _to_groups(u, bsz, n_chunks, t_len)
    dvec = jnp.tile(d_skip.astype(F32).reshape(N_SSM_GROUPS, 1, SSM_GROUP), (1, 1, t_len))
    rows = bsz * n_chunks
    rows_pad = -(-rows // 16) * 16
    xg = _pad_rows(xg, rows_pad) if rows_pad != rows else xg
    if rows_pad != rows:
        assert n_chunks == 1
        h0r = _pad_rows(h0r, rows_pad)
        h0i = _pad_rows(h0i, rows_pad)
    yg, hr, hi = _ssm(xg, prep, h0r, h0i, dvec, t_len=t_len,
                      n_seq=rows_pad // n_chunks, n_chunks=n_chunks)
    ys = _from_groups(yg[:, :rows], bsz, n_chunks, t_len)
    hr = jnp.swapaxes(hr[:, :bsz], 0, 1)
    hi = jnp.swapaxes(hi[:, :bsz], 0, 1)

    y = _out_proj(attn.reshape(n, D_ATTN), ys, gs, x2d, gate_m, w_glu, b_glu, w_out, g_post,
                  tm=tm, rows_per_mod=rows_per_mod)
    k_out = k32.reshape(bsz, s, N_KV_HEADS, HEAD_DIM)
    v_out = v32.reshape(bsz, s, N_KV_HEADS, HEAD_DIM)
    ik_out = ik32.reshape(bsz, s, IDX_DIM)
    return y.reshape(bsz, s, d), (k_out, v_out, ik_out, hr, hi)


def kernel(x_prompt, x_sample, c_prompt, c_sample, cache_k, cache_v, cache_idx_k, state_ssm_re,
           state_ssm_im, w_ada, b_ada, g_pre, g_post, w_in, lambda_re, lambda_im, log_dt, b_re, b_im,
           c_re, c_im, d_skip, w_glu, b_glu, w_out):
    depth = w_ada.shape[0]
    bp = c_prompt.shape[0]
    bs = c_sample.shape[0]
    yp, ys = x_prompt, x_sample
    outs_p = [[] for _ in range(5)]
    outs_s = [[] for _ in range(5)]
    c_all = jnp.concatenate([c_prompt, c_sample], axis=0)
    c_rows = -(-c_all.shape[0] // 8) * 8
    c_all = jnp.pad(c_all, ((0, c_rows - c_all.shape[0]), (0, 0)))
    for l in range(depth):
        mod = _adaln(c_all, w_ada[l], b_ada[l].reshape(1, -1))
        weights = _split_w_in(w_in[l])
        prep = _ssm_prep(lambda_re[l], lambda_im[l], log_dt[l], b_re[l], b_im[l], c_re[l], c_im[l])
        common = (g_pre[l], g_post[l], weights, prep, d_skip[l], w_glu[l], b_glu[l], w_out[l])
        yp, new_p = _layer(yp, mod[:bp], None, *common)
        past = (cache_k[l], cache_v[l], cache_idx_k[l], state_ssm_re[l], state_ssm_im[l])
        ys, new_s = _layer(ys, mod[bp:bp + bs], past, *common)
        for acc, val in zip(outs_p, new_p):
            acc.append(val)
        for acc, val in zip(outs_s, new_s):
            acc.append(val)
    return (yp, ys) + tuple(jnp.stack(a) for a in outs_p) + tuple(jnp.stack(a) for a in outs_s)
```

```python
import functools
import math

import jax
import jax.numpy as jnp
from jax import lax
from jax.experimental import pallas as pl
from jax.experimental.pallas import tpu as pltpu

F32 = jnp.float32
BF16 = jnp.bfloat16
I32 = jnp.int32

CHUNK = 64
Q_BLOCK = 128
D_ATTN = 1024
N_HEADS = 8
N_KV_HEADS = 2
HEAD_DIM = 128
KV_DIM = N_KV_HEADS * HEAD_DIM
HEADS_PER_KV = N_HEADS // N_KV_HEADS
N_IDX_HEADS = 16
IDX_DIM = 64
IDX_PAIRS = N_IDX_HEADS // 2
TOPK_MAX = 256
D_SSM = 1024
SSM_GROUP = 16
N_SSM_GROUPS = D_SSM // SSM_GROUP
SSM_STATE = 64
RMS_EPS = 1e-6
IN_SIZES = (D_ATTN, KV_DIM, KV_DIM, N_IDX_HEADS * IDX_DIM, IDX_DIM, N_IDX_HEADS, D_ATTN, D_SSM, D_SSM)

LANES = 128
V7X_VMEM_LIMIT_BYTES = 56 * 1024 * 1024

INT_MIN = -(2 ** 31)
MASKED_LOGIT = -1e30
MIN_ROW_SUM = 2.0 ** -100
KEY_TILE = 512
SCORE_KEYS = 256
COARSE_STEPS = 9
FINE_STEPS = 12


def _const_spec(shape):
    nd = len(shape)
    return pl.BlockSpec(shape, lambda *_: (0,) * nd, pipeline_mode=pl.Buffered(1))


def _dot_nt(a, b):
    return lax.dot_general(a, b, (((1,), (1,)), ((), ())), preferred_element_type=F32)


def _adaln_kernel(c_ref, w_ref, b_ref, o_ref):
    o_ref[...] = jnp.dot(c_ref[...], w_ref[...], preferred_element_type=F32) + b_ref[...]


def _adaln(c, w_ada, b_ada):
    rows, d = c.shape
    n_out = w_ada.shape[1]
    tn = 512
    return pl.pallas_call(
        _adaln_kernel,
        out_shape=jax.ShapeDtypeStruct((rows, n_out), F32),
        grid=(n_out // tn,),
        in_specs=[pl.BlockSpec((rows, d), lambda j: (0, 0)),
                  pl.BlockSpec((d, tn), lambda j: (0, j)),
                  pl.BlockSpec((1, tn), lambda j: (0, j))],
        out_specs=pl.BlockSpec((rows, tn), lambda j: (0, j)),
        compiler_params=pltpu.CompilerParams(dimension_semantics=("arbitrary",)),
        name="adaln",
    )(c, w_ada, b_ada)


def _proj_kernel(x_ref, scale_ref, shift_ref, g_ref,
                 wq_ref, wk_ref, wv_ref, wiq_ref, wik_ref, wiw_ref, wza_ref, wu_ref, wzs_ref,
                 q_ref, k_ref, kb_ref, v_ref, vb_ref, iq_ref, ik_ref, iklo_ref, ikhi_ref,
                 iw_ref, ga_ref, u_ref, gs_ref):
    x = x_ref[...]
    ms = jnp.mean(x * x, axis=-1, keepdims=True)
    y = x * lax.rsqrt(ms + RMS_EPS) * g_ref[...]
    h = (y * (1.0 + scale_ref[...]) + shift_ref[...]).astype(BF16)

    def mm(w_ref):
        return jnp.dot(h, w_ref[...], preferred_element_type=F32)

    q_ref[...] = (mm(wq_ref) * (HEAD_DIM ** -0.5 * math.log2(math.e))).astype(BF16)
    rows = x.shape[0]
    zk = mm(wk_ref)
    zv = mm(wv_ref)
    for hd in range(N_KV_HEADS):
        k_ref[pl.ds(hd, rows, stride=N_KV_HEADS), :] = zk[:, hd * HEAD_DIM:(hd + 1) * HEAD_DIM]
        v_ref[pl.ds(hd, rows, stride=N_KV_HEADS), :] = zv[:, hd * HEAD_DIM:(hd + 1) * HEAD_DIM]
    kb_ref[...] = zk.astype(BF16)
    vb_ref[...] = zv.astype(BF16)
    iq_ref[...] = (mm(wiq_ref) * (IDX_DIM ** -0.5)).astype(BF16)
    zik = mm(wik_ref)
    ik_ref[...] = zik[:, :IDX_DIM]
    lane = lax.broadcasted_iota(I32, zik.shape, 1)
    iklo_ref[...] = jnp.where(lane < IDX_DIM, zik, 0.0).astype(BF16)
    ikhi_ref[...] = jnp.where(lane >= IDX_DIM, zik, 0.0).astype(BF16)
    iw_ref[...] = mm(wiw_ref)[:, :N_IDX_HEADS] * (N_IDX_HEADS ** -0.5)
    za = mm(wza_ref)
    ga_ref[...] = (za * jax.nn.sigmoid(za)).astype(BF16)
    u_ref[...] = mm(wu_ref).astype(BF16)
    zs = mm(wzs_ref)
    gs_ref[...] = (zs * jax.nn.sigmoid(zs)).astype(BF16)


def _split_w_in(w_in):
    offs = [0]
    for s in IN_SIZES:
        offs.append(offs[-1] + s)
    cols = [w_in[:, offs[i]:offs[i + 1]].astype(BF16) for i in range(len(IN_SIZES))]
    wq, wk, wv, wiq, wik, wiw, wza, wu, wzs = cols
    wik2 = jnp.concatenate([wik, wik], axis=1)
    wiw_p = jnp.pad(wiw, ((0, 0), (0, LANES - N_IDX_HEADS)))
    return (wq, wk, wv, wiq, wik2, wiw_p, wza, wu, wzs)


def _proj(x2d, scale, shift, g_pre, weights, *, tm, rows_per_mod):
    n, d = x2d.shape
    if rows_per_mod is None:
        mod_spec = pl.BlockSpec((tm, d), lambda i: (i, 0))
    else:
        tiles_per_mod = rows_per_mod // tm
        mod_spec = pl.BlockSpec((None, 1, d), lambda i: (i // tiles_per_mod, 0, 0))

    def row_spec(width):
        return pl.BlockSpec((tm, width), lambda i: (i, 0))

    out_defs = [
        (1, D_ATTN, BF16), (N_KV_HEADS, HEAD_DIM, F32), (1, KV_DIM, BF16), (N_KV_HEADS, HEAD_DIM, F32),
        (1, KV_DIM, BF16), (1, N_IDX_HEADS * IDX_DIM, BF16), (1, IDX_DIM, F32), (1, LANES, BF16),
        (1, LANES, BF16), (1, N_IDX_HEADS, F32), (1, D_ATTN, BF16), (1, D_SSM, BF16), (1, D_SSM, BF16)]
    return pl.pallas_call(
        _proj_kernel,
        out_shape=[jax.ShapeDtypeStruct((r * n, w), dt) for r, w, dt in out_defs],
        grid=(n // tm,),
        in_specs=[row_spec(d), mod_spec, mod_spec, _const_spec((1, d))]
                 + [_const_spec(w.shape) for w in weights],
        out_specs=[pl.BlockSpec((r * tm, w), lambda i: (i, 0)) for r, w, _ in out_defs],
        compiler_params=pltpu.CompilerParams(dimension_semantics=("arbitrary",),
                                             vmem_limit_bytes=V7X_VMEM_LIMIT_BYTES),
        name="in_proj",
    )(x2d, scale, shift, g_pre, *weights)


def _dsa_kernel(q_ref, iq_ref, iw_ref, ga_ref, kb_ref, vb_ref, iklo_ref, ikhi_ref, o_ref,
                sc_ref, wb_ref, iqs_ref, qs_ref, m_ref, l_ref, acc_ref, kn_ref,
                *, qb, tk, topk, causal, n_keys, n_keys_pad):
    blk = pl.program_id(1)
    chunks_per_tile = tk // LANES
    row = lax.broadcasted_iota(I32, (qb, LANES), 0)
    if causal:
        n_vis = (blk + 1) * qb
        limit = (((blk * qb + row) // CHUNK) + 1) * CHUNK
        n_tiles = (n_vis + tk - 1) // tk
    else:
        limit = jnp.full((qb, LANES), n_keys, I32)
        n_tiles = (n_keys + tk - 1) // tk

    for j in range(IDX_PAIRS):
        iqs_ref[j * qb:(j + 1) * qb, :] = iq_ref[:, j * LANES:(j + 1) * LANES]
    for kv in range(N_KV_HEADS):
        for hh in range(HEADS_PER_KV):
            c0 = (kv * HEADS_PER_KV + hh) * HEAD_DIM
            qs_ref[kv, hh * qb:(hh + 1) * qb, :] = q_ref[:, c0:c0 + HEAD_DIM]
    iw = iw_ref[...]
    for h in range(N_IDX_HEADS):
        wb_ref[h] = jnp.broadcast_to(iw[:, h:h + 1], (qb, LANES))

    lane = lax.broadcasted_iota(I32, (qb, LANES), 1)

    def score_keys(first_chunk, n_matmuls, carry):
        m1, m2 = carry
        iqs = iqs_ref[...]
        for part in range(n_matmuls):
            chunk0 = first_chunk + part * (SCORE_KEYS // LANES)
            k0 = pl.multiple_of(chunk0 * LANES, SCORE_KEYS)
            s_lo = _dot_nt(iqs, iklo_ref[pl.ds(k0, SCORE_KEYS), :])
            s_hi = _dot_nt(iqs, ikhi_ref[pl.ds(k0, SCORE_KEYS), :])
            for c in range(SCORE_KEYS // LANES):
                cs = slice(c * LANES, (c + 1) * LANES)
                acc = None
                for j in range(IDX_PAIRS):
                    rs = slice(j * qb, (j + 1) * qb)
                    term = (jnp.maximum(s_lo[rs, cs], 0.0) * wb_ref[2 * j]
                            + jnp.maximum(s_hi[rs, cs], 0.0) * wb_ref[2 * j + 1])
                    acc = term if acc is None else acc + term
                ci = chunk0 + c
                visible = (ci * LANES + lane) < limit
                sc_ref[ci] = jnp.where(visible, acc, jnp.nan)
                v = jnp.where(visible, acc, -jnp.inf)
                m1, v = jnp.maximum(m1, v), jnp.minimum(m1, v)
                m2 = jnp.maximum(m2, v)
        return m1, m2

    def over_tile_pairs(step, carry):
        carry = lax.fori_loop(0, n_tiles // 2, lambda t, c: step(2 * t, 2, c), carry)
        return lax.cond(n_tiles % 2 == 1, lambda c: step(n_tiles - 1, 1, c), lambda c: c, carry)

    ninf = jnp.full((qb, LANES), -jnp.inf, F32)
    m1, m2 = over_tile_pairs(
        lambda t0, nt, c: score_keys(t0 * chunks_per_tile, nt * tk // SCORE_KEYS, c), (ninf, ninf))

    def lane_fold(x, op):
        return jnp.broadcast_to(op(x, axis=1, keepdims=True), (qb, LANES))

    def count_rows(pred):
        def count_tile(t, cnt):
            for c in range(chunks_per_tile):
                ci = t * chunks_per_tile + c
                cnt = cnt + jnp.where(pred(sc_ref[ci], ci * LANES + lane), 1.0, 0.0)
            return cnt

        return lane_fold(lax.fori_loop(0, n_tiles, count_tile, jnp.zeros((qb, LANES), F32)), jnp.sum)

    def key_bisection():
        def ordered_to_f32(key):
            return lax.bitcast_convert_type(jnp.where(key < 0, key ^ jnp.int32(0x7FFFFFFF), key), F32)

        def bisect(i, carry):
            lo, cnt_lo = carry
            cand_key = lo + lax.shift_left(jnp.int32(1), 31 - i)
            cand = ordered_to_f32(cand_key)
            total = count_rows(lambda s, _: s >= cand)
            take = total >= topk
            return jnp.where(take, cand_key, lo), jnp.where(take, total, cnt_lo)

        lo, cnt_lo = lax.fori_loop(0, 32, bisect, (jnp.full((qb, LANES), INT_MIN, I32),
                                                   jnp.zeros((qb, LANES), F32)))
        return ordered_to_f32(lo), cnt_lo

    few = limit < topk

    bracketed = limit >= 2 * LANES
    top = m2 if topk > LANES else m1
    lo_f = jnp.where(bracketed, lane_fold(m2, jnp.min), 0.0)
    hi_f = jnp.where(bracketed, lane_fold(top, jnp.max), 0.0)
    hi_f = hi_f + (jnp.abs(hi_f) * 2.0 ** -20 + 1e-30)

    cnt_hi = lane_fold(jnp.where(m1 >= hi_f, 1.0, 0.0), jnp.sum)

    def halve(_, carry):
        lo_f, hi_f, cnt_hi = carry
        mid = lo_f + (hi_f - lo_f) * 0.5
        total = count_rows(lambda s, _: s >= mid)
        take = total >= topk
        return jnp.where(take, mid, lo_f), jnp.where(take, hi_f, mid), jnp.where(take, cnt_hi, total)

    lo_f, hi_f, n_above = lax.fori_loop(0, COARSE_STEPS, halve, (lo_f, hi_f, cnt_hi))

    def bucket_tile(t, carry):
        b1, b2, b3, pop = carry
        for c in range(chunks_per_tile):
            s = sc_ref[t * chunks_per_tile + c]
            inb = (s >= lo_f) & jnp.logical_not(s >= hi_f)
            v = jnp.where(inb, s, -jnp.inf)
            pop = pop + jnp.where(inb, 1.0, 0.0)
            b1, v = jnp.maximum(b1, v), jnp.minimum(b1, v)
            b2, v = jnp.maximum(b2, v), jnp.minimum(b2, v)
            b3 = jnp.maximum(b3, v)
        return b1, b2, b3, pop

    b1, b2, b3, pop = lax.fori_loop(0, n_tiles, bucket_tile,
                                    (ninf, ninf, ninf, jnp.zeros((qb, LANES), F32)))
    want = topk - n_above

    transposed = qb == LANES
    if transposed:
        cands = jnp.concatenate([b1.T, b2.T, b3.T], axis=0)
        want_q = want.T[:1]
        axis = 0
    else:
        cands = jnp.concatenate([b1, b2, b3], axis=1)
        want_q = want[:, :1]
        axis = 1

    def walk(_, carry):
        prev, covered, tau_q, cnt_q = carry
        cur = jnp.max(jnp.where(cands < prev, cands, -jnp.inf), axis=axis, keepdims=True)
        covered_new = covered + jnp.sum(jnp.where(cands == cur, 1.0, 0.0), axis=axis, keepdims=True)
        found = (covered < want_q) & (covered_new >= want_q) & (cur > -jnp.inf)
        return cur, covered_new, jnp.where(found, cur, tau_q), jnp.where(found, covered_new, cnt_q)

    nan_q = jnp.full(want_q.shape, jnp.nan, F32)
    _, _, tau_q, cnt_q = lax.fori_loop(
        0, FINE_STEPS, walk,
        (jnp.full(want_q.shape, jnp.inf, F32), jnp.zeros(want_q.shape, F32), nan_q, nan_q))
    if transposed:
        tau_fast = jnp.broadcast_to(tau_q, (LANES, qb)).T
        cnt_fast = n_above + jnp.broadcast_to(cnt_q, (LANES, qb)).T
    else:
        tau_fast = jnp.broadcast_to(tau_q, (qb, LANES))
        cnt_fast = n_above + jnp.broadcast_to(cnt_q, (qb, LANES))

    bad = jnp.logical_not(few) & (jnp.logical_not(bracketed) | (pop > 3.0) | (want < 1.0)
                                  | jnp.logical_not(tau_fast == tau_fast))
    tau, cnt_tau = lax.cond(jnp.max(jnp.where(bad, 1.0, 0.0)) > 0.0, key_bisection,
                            lambda: (tau_fast, cnt_fast))
    tau = jnp.where(few, -jnp.inf, tau)
    excess = jnp.where(few, 0.0, cnt_tau - topk)

    @pl.when(jnp.max(excess) > 0)
    def _resolve_surplus():
        idx_bits = (n_keys_pad - 1).bit_length()

        def kept(s, kidx, vstar, jstar):
            return (s >= tau) & ((s > vstar) | ((s == vstar) & (kidx < jstar)))

        def drop_step(state):
            exc, vstar, jstar = state

            def min_tile(t, vm):
                for c in range(chunks_per_tile):
                    ci = t * chunks_per_tile + c
                    s = sc_ref[ci]
                    vm = jnp.minimum(vm, jnp.where(kept(s, ci * LANES + lane, vstar, jstar), s, jnp.inf))
                return vm

            vm = lax.fori_loop(0, n_tiles, min_tile, jnp.full((qb, LANES), jnp.inf, F32))
            vmin = jnp.broadcast_to(jnp.min(vm, axis=1, keepdims=True), (qb, LANES))

            def tied(s, kidx):
                return kept(s, kidx, vstar, jstar) & (s == vmin)

            cmin = count_rows(tied)
            active = exc > 0
            drop_all = active & (exc >= cmin)
            partial = active & (exc < cmin)
            keep = cmin - exc

            def idx_bisect(b, j0):
                cand_j = j0 + lax.shift_left(jnp.int32(1), idx_bits - 1 - b)
                below = count_rows(lambda s, kidx: tied(s, kidx) & (kidx < cand_j))
                return jnp.where(below < keep, cand_j, j0)

            j0 = lax.fori_loop(0, idx_bits, idx_bisect, jnp.zeros((qb, LANES), I32))
            vstar = jnp.where(active, vmin, vstar)
            jstar = jnp.where(drop_all, 0, jnp.where(partial, j0 + 1, jstar))
            exc = jnp.where(drop_all, exc - cmin, jnp.where(partial, 0, exc))
            return exc, vstar, jstar

        _, vstar, jstar = lax.while_loop(
            lambda state: jnp.max(state[0]) > 0, drop_step,
            (excess, tau, jnp.full((qb, LANES), 2 ** 30, I32)))

        def rewrite_tile(t, carry):
            for c in range(chunks_per_tile):
                ci = t * chunks_per_tile + c
                s = sc_ref[ci]
                dropped = (s >= tau) & jnp.logical_not(kept(s, ci * LANES + lane, vstar, jstar))
                sc_ref[ci] = jnp.where(dropped, jnp.nan, s)
            return carry

        lax.fori_loop(0, n_tiles, rewrite_tile, 0)

    def masked_logits(t0, nt, kv):
        ks = pl.multiple_of(t0 * tk, tk)
        bias = jnp.concatenate(
            [jnp.where(sc_ref[t0 * chunks_per_tile + c] >= tau, 0.0, MASKED_LOGIT)
             for c in range(nt * chunks_per_tile)], axis=1)
        s = _dot_nt(qs_ref[kv], kb_ref[pl.ds(ks, nt * tk), kv * HEAD_DIM:(kv + 1) * HEAD_DIM])
        return (s.reshape(HEADS_PER_KV, qb, nt * tk) + bias[None]).reshape(HEADS_PER_KV * qb, nt * tk)

    def attend_tiles(t0, nt, carry):
        ks = pl.multiple_of(t0 * tk, tk)
        for kv in range(N_KV_HEADS):
            s = masked_logits(t0, nt, kv)
            m = m_ref[kv]
            l = l_ref[kv]
            ps = []
            for c in range(nt * chunks_per_tile):
                e = jnp.exp2(s[:, c * LANES:(c + 1) * LANES] - m)
                l = l + e
                ps.append(e.astype(BF16))
            l_ref[kv] = l
            p = jnp.concatenate(ps, axis=1)
            acc_ref[kv] += jnp.dot(p, vb_ref[pl.ds(ks, nt * tk), kv * HEAD_DIM:(kv + 1) * HEAD_DIM],
                                   preferred_element_type=F32)
        return carry

    def attend():
        l_ref[...] = jnp.zeros(l_ref.shape, F32)
        acc_ref[...] = jnp.zeros(acc_ref.shape, F32)
        over_tile_pairs(attend_tiles, 0)

    @pl.when(blk == 0)
    def _key_norms():
        for kv in range(N_KV_HEADS):
            def norm_tile(t, mx):
                ks = pl.multiple_of(t * tk, tk)
                k = kb_ref[pl.ds(ks, tk), kv * HEAD_DIM:(kv + 1) * HEAD_DIM].astype(F32)
                return jnp.maximum(mx, jnp.sum(k * k, axis=1, keepdims=True))

            mx = lax.fori_loop(0, n_keys_pad // tk, norm_tile, jnp.zeros((tk, 1), F32))
            kn_ref[kv] = jnp.broadcast_to(jnp.sqrt(jnp.max(mx, axis=0, keepdims=True)), kn_ref.shape[1:])

    for kv in range(N_KV_HEADS):
        qf = qs_ref[kv].astype(F32)
        qn = jnp.sqrt(jnp.sum(qf * qf, axis=1, keepdims=True))
        m_ref[kv] = jnp.broadcast_to(qn, m_ref.shape[1:]) * kn_ref[kv, 0:1, :]
    attend()

    l_min = jnp.min(jnp.sum(l_ref[...], axis=2))

    @pl.when(jnp.logical_not(l_min >= MIN_ROW_SUM))
    def _exact_shift():
        m_ref[...] = jnp.full(m_ref.shape, MASKED_LOGIT, F32)

        def max_tile(t, carry):
            for kv in range(N_KV_HEADS):
                s = masked_logits(t, 1, kv)
                mx = m_ref[kv]
                for c in range(chunks_per_tile):
                    mx = jnp.maximum(mx, s[:, c * LANES:(c + 1) * LANES])
                m_ref[kv] = mx
            return carry

        lax.fori_loop(0, n_tiles, max_tile, 0)
        for kv in range(N_KV_HEADS):
            m_ref[kv] = jnp.broadcast_to(jnp.max(m_ref[kv], axis=1, keepdims=True), m_ref.shape[1:])
        attend()

    for kv in range(N_KV_HEADS):
        o = acc_ref[kv] / jnp.sum(l_ref[kv], axis=1, keepdims=True)
        for hh in range(HEADS_PER_KV):
            c0 = (kv * HEADS_PER_KV + hh) * HEAD_DIM
            gate = ga_ref[:, c0:c0 + HEAD_DIM].astype(F32)
            o_ref[:, c0:c0 + HEAD_DIM] = (o[hh * qb:(hh + 1) * qb] * gate).astype(BF16)


def _dsa(q, iq, iw, ga, kb, vb, iklo, ikhi, *, qb, causal, n_keys):
    b, s, _ = q.shape
    lk = kb.shape[1]
    tk = KEY_TILE
    assert lk % tk == 0 and s % qb == 0
    topk = min(TOPK_MAX, n_keys // 4)
    kern = functools.partial(_dsa_kernel, qb=qb, tk=tk, topk=topk, causal=causal, n_keys=n_keys,
                             n_keys_pad=lk)

    def q_spec(width):
        return pl.BlockSpec((None, qb, width), lambda bi, i: (bi, i, 0))

    def kv_spec(width):
        return pl.BlockSpec((None, lk, width), lambda bi, i: (bi, 0, 0))

    return pl.pallas_call(
        kern,
        out_shape=jax.ShapeDtypeStruct((b, s, D_ATTN), BF16),
        grid=(b, s // qb),
        in_specs=[q_spec(D_ATTN), q_spec(N_IDX_HEADS * IDX_DIM), q_spec(N_IDX_HEADS), q_spec(D_ATTN),
                  kv_spec(KV_DIM), kv_spec(KV_DIM), kv_spec(LANES), kv_spec(LANES)],
        out_specs=q_spec(D_ATTN),
        scratch_shapes=[
            pltpu.VMEM((lk // LANES, qb, LANES), F32),
            pltpu.VMEM((N_IDX_HEADS, qb, LANES), F32),
            pltpu.VMEM((IDX_PAIRS * qb, LANES), BF16),
            pltpu.VMEM((N_KV_HEADS, HEADS_PER_KV * qb, HEAD_DIM), BF16),
            pltpu.VMEM((N_KV_HEADS, HEADS_PER_KV * qb, LANES), F32),
            pltpu.VMEM((N_KV_HEADS, HEADS_PER_KV * qb, LANES), F32),
            pltpu.VMEM((N_KV_HEADS, HEADS_PER_KV * qb, HEAD_DIM), F32),
            pltpu.VMEM((N_KV_HEADS, 8, LANES), F32),
        ],
        compiler_params=pltpu.CompilerParams(dimension_semantics=("arbitrary", "arbitrary"),
                                             vmem_limit_bytes=V7X_VMEM_LIMIT_BYTES),
        name="dsa",
    )(q, iq, iw, ga, kb, vb, iklo, ikhi)


def _ssm_prep_kernel(lre_c_ref, lim_c_ref, lre_r_ref, lim_r_ref, ldt_ref, btr_ref, bti_ref,
                     ctr_ref, cti_ref, kflat_ref, mtr_ref, mti_ref, emr_ref, emi_ref,
                     a64r_ref, a64i_ref, a16r_ref, a16i_ref):
    hp = lax.Precision.HIGHEST
    dt = jnp.exp(ldt_ref[...])
    n_lag = CHUNK
    width = n_lag * SSM_GROUP

    ldr_c = lre_c_ref[...] * dt
    ldi_c = lim_c_ref[...] * dt
    lag = lax.broadcasted_iota(I32, (1, n_lag), 1).astype(F32)
    mag = jnp.exp(ldr_c * lag)
    pr_lag = mag * jnp.cos(ldi_c * lag)
    pi_lag = mag * jnp.sin(ldi_c * lag)
    col = lax.broadcasted_iota(I32, (n_lag, width), 1)
    pick_lag = (col // SSM_GROUP == lax.broadcasted_iota(I32, (n_lag, width), 0)).astype(F32)
    col = lax.broadcasted_iota(I32, (SSM_GROUP, width), 1)
    pick_ch = (col % SSM_GROUP == lax.broadcasted_iota(I32, (SSM_GROUP, width), 0)).astype(F32)
    pr = jnp.dot(pr_lag, pick_lag, precision=hp, preferred_element_type=F32)
    pi = jnp.dot(pi_lag, pick_lag, precision=hp, preferred_element_type=F32)
    ctr = jnp.dot(ctr_ref[...], pick_ch, precision=hp, preferred_element_type=F32)
    cti = jnp.dot(cti_ref[...], pick_ch, precision=hp, preferred_element_type=F32)
    qr = pr * ctr - pi * cti
    qi = pr * cti + pi * ctr

    lre_r = lre_r_ref[...]
    lim_r = lim_r_ref[...]
    ldr_r = lre_r * dt
    ldi_r = lim_r * dt
    lbr = jnp.exp(ldr_r) * jnp.cos(ldi_r)
    lbi = jnp.exp(ldr_r) * jnp.sin(ldi_r)
    den = lre_r * lre_r + lim_r * lim_r
    nr = lbr - 1.0
    fr = (nr * lre_r + lbi * lim_r) / den
    fi = (lbi * lre_r - nr * lim_r) / den
    btr = btr_ref[...]
    bti = bti_ref[...]
    bbr = fr * btr - fi * bti
    bbi = fr * bti + fi * btr

    kflat_ref[...] = (jnp.dot(bbr, qr, precision=hp, preferred_element_type=F32)
                      - jnp.dot(bbi, qi, precision=hp, preferred_element_type=F32))

    lbr_c = jnp.exp(ldr_c) * jnp.cos(ldi_c)
    lbi_c = jnp.exp(ldr_c) * jnp.sin(ldi_c)
    emr_ref[...] = (lbr_c * qr - lbi_c * qi).astype(BF16)
    emi_ref[...] = (-(lbr_c * qi + lbi_c * qr)).astype(BF16)

    back = (n_lag - 1 - lax.broadcasted_iota(I32, (n_lag, 1), 0)).astype(F32)
    bmag = jnp.exp(ldr_r * back)
    bpr = bmag * jnp.cos(ldi_r * back)
    bpi = bmag * jnp.sin(ldi_r * back)
    for t in range(n_lag):
        wr = bpr[t:t + 1, :]
        wi = bpi[t:t + 1, :]
        rows = slice(t * SSM_GROUP, (t + 1) * SSM_GROUP)
        mtr_ref[rows, :] = (wr * bbr - wi * bbi).astype(BF16)
        mti_ref[rows, :] = (wr * bbi + wi * bbr).astype(BF16)

    for steps, ar_ref, ai_ref in ((float(CHUNK), a64r_ref, a64i_ref), (16.0, a16r_ref, a16i_ref)):
        amag = jnp.exp(ldr_r * steps)
        ar_ref[...] = amag * jnp.cos(ldi_r * steps)
        ai_ref[...] = amag * jnp.sin(ldi_r * steps)


def _ssm_prep(lambda_re, lambda_im, log_dt, b_re, b_im, c_re, c_im):
    g, p = lambda_re.shape
    width = CHUNK * SSM_GROUP
    lre_c = lambda_re.reshape(g, p, 1)
    lim_c = lambda_im.reshape(g, p, 1)
    lre_r = lambda_re.reshape(g, 1, p)
    lim_r = lambda_im.reshape(g, 1, p)
    ldt = log_dt.reshape(g, 1, 1)
    btr = jnp.swapaxes(b_re, 1, 2)
    bti = jnp.swapaxes(b_im, 1, 2)
    ctr = jnp.swapaxes(c_re, 1, 2)
    cti = jnp.swapaxes(c_im, 1, 2)

    def gspec(*shape):
        return pl.BlockSpec((None,) + shape, lambda i: (i,) + (0,) * len(shape))

    out_defs = [((SSM_GROUP, width), F32), ((width, p), BF16), ((width, p), BF16),
                ((p, width), BF16), ((p, width), BF16),
                ((1, p), F32), ((1, p), F32), ((1, p), F32), ((1, p), F32)]
    return pl.pallas_call(
        _ssm_prep_kernel,
        out_shape=[jax.ShapeDtypeStruct((g,) + s, dt) for s, dt in out_defs],
        grid=(g,),
        in_specs=[gspec(p, 1), gspec(p, 1), gspec(1, p), gspec(1, p), gspec(1, 1),
                  gspec(SSM_GROUP, p), gspec(SSM_GROUP, p), gspec(p, SSM_GROUP), gspec(p, SSM_GROUP)],
        out_specs=[gspec(*s) for s, _ in out_defs],
        compiler_params=pltpu.CompilerParams(dimension_semantics=("arbitrary",)),
        name="ssm_prep",
    )(lre_c, lim_c, lre_r, lim_r, ldt, btr, bti, ctr, cti)


def _ssm_kernel(x_ref, kflat_ref, mtr_ref, mti_ref, emr_ref, emi_ref, ar_ref, ai_ref,
                h0r_ref, h0i_ref, dvec_ref, y_ref, hr_ref, hi_ref,
                toep_ref, gr_ref, gi_ref, hpr_ref, hpi_ref, *, t_len, n_seq, n_chunks):
    width = t_len * SSM_GROUP
    kflat = kflat_ref[:, :width]
    lane = lax.broadcasted_iota(I32, (SSM_GROUP, width), 1)
    toep_ref[0:SSM_GROUP, :] = kflat.astype(BF16)
    for t in range(1, t_len):
        shifted = pltpu.roll(kflat, t * SSM_GROUP, axis=1)
        toep_ref[t * SSM_GROUP:(t + 1) * SSM_GROUP, :] = jnp.where(
            lane >= t * SSM_GROUP, shifted, 0.0).astype(BF16)

    x = x_ref[...]
    gr_ref[...] = jnp.dot(x, mtr_ref[...], preferred_element_type=F32)
    gi_ref[...] = jnp.dot(x, mti_ref[...], preferred_element_type=F32)

    ar = ar_ref[...]
    ai = ai_ref[...]

    def chunk_step(n, carry):
        h_r, h_i = carry
        rows = pl.ds(n, n_seq, stride=n_chunks)
        hpr_ref[rows, :] = h_r
        hpi_ref[rows, :] = h_i
        g_r = gr_ref[rows, :]
        g_i = gi_ref[rows, :]
        return (ar * h_r - ai * h_i + g_r, ar * h_i + ai * h_r + g_i)

    if n_chunks == 1:
        h_r, h_i = chunk_step(0, (h0r_ref[...], h0i_ref[...]))
    else:
        h_r, h_i = lax.fori_loop(0, n_chunks, chunk_step, (h0r_ref[...], h0i_ref[...]))
    hr_ref[...] = h_r
    hi_ref[...] = h_i

    y = jnp.dot(x, toep_ref[...], preferred_element_type=F32)
    y = y + jnp.dot(hpr_ref[...].astype(BF16), emr_ref[:, :width], preferred_element_type=F32)
    y = y + jnp.dot(hpi_ref[...].astype(BF16), emi_ref[:, :width], preferred_element_type=F32)
    y = y + x.astype(F32) * dvec_ref[...]
    y_ref[...] = y.astype(BF16)


def _ssm(xg, prep, h0r, h0i, dvec, *, t_len, n_seq, n_chunks):
    kflat, mtr, mti, emr, emi, a64r, a64i, a16r, a16i = prep
    g, rows, width = xg.shape
    p = SSM_STATE
    full = CHUNK * SSM_GROUP
    if t_len == CHUNK:
        ar, ai = a64r, a64i
        mt_block = 0
    else:
        assert t_len == 16
        ar, ai = a16r, a16i
        mt_block = (full - width) // width
    kern = functools.partial(_ssm_kernel, t_len=t_len, n_seq=n_seq, n_chunks=n_chunks)

    def gspec(*shape):
        return pl.BlockSpec((None,) + shape, lambda i: (i,) + (0,) * len(shape))

    mt_spec = pl.BlockSpec((None, width, p), lambda i: (i, mt_block, 0))
    return pl.pallas_call(
        kern,
        out_shape=[jax.ShapeDtypeStruct((g, rows, width), BF16),
                   jax.ShapeDtypeStruct((g, n_seq, p), F32),
                   jax.ShapeDtypeStruct((g, n_seq, p), F32)],
        grid=(g,),
        in_specs=[gspec(rows, width), gspec(SSM_GROUP, full), mt_spec, mt_spec,
                  gspec(p, full), gspec(p, full), gspec(1, p), gspec(1, p),
                  gspec(n_seq, p), gspec(n_seq, p), gspec(1, width)],
        out_specs=[gspec(rows, width), gspec(n_seq, p), gspec(n_seq, p)],
        scratch_shapes=[pltpu.VMEM((width, width), BF16),
                        pltpu.VMEM((rows, p), F32), pltpu.VMEM((rows, p), F32),
                        pltpu.VMEM((rows, p), F32), pltpu.VMEM((rows, p), F32)],
        compiler_params=pltpu.CompilerParams(dimension_semantics=("arbitrary",)),
        name="ssm",
    )(xg, kflat, mtr, mti, emr, emi, ar, ai, h0r, h0i, dvec)


def _out_kernel(a_ref, ys_ref, gs_ref, x_ref, gate_ref, wglu_ref, bglu_ref, woa_ref, wos_ref,
                gpost_ref, o_ref):
    y = ys_ref[...].astype(F32)
    y = 0.5 * y * (1.0 + jnp.tanh(math.sqrt(2.0 / math.pi) * (y + 0.044715 * (y * y * y))))
    z = jnp.dot(y.astype(BF16), wglu_ref[...], preferred_element_type=F32) + bglu_ref[...]
    y = y * jax.nn.sigmoid(z) * gs_ref[...].astype(F32)
    out = (jnp.dot(a_ref[...], woa_ref[...], preferred_element_type=F32)
           + jnp.dot(y.astype(BF16), wos_ref[...], preferred_element_type=F32))
    ms = jnp.mean(out * out, axis=-1, keepdims=True)
    normed = out * lax.rsqrt(ms + RMS_EPS) * gpost_ref[...]
    o_ref[...] = x_ref[...] + gate_ref[...] * normed


def _out_proj(a, ys, gs, x2d, gate, w_glu, b_glu, w_out, g_post, *, tm, rows_per_mod):
    n, d = x2d.shape
    if rows_per_mod is None:
        mod_spec = pl.BlockSpec((tm, d), lambda i: (i, 0))
    else:
        tiles_per_mod = rows_per_mod // tm
        mod_spec = pl.BlockSpec((None, 1, d), lambda i: (i // tiles_per_mod, 0, 0))

    def row_spec(width):
        return pl.BlockSpec((tm, width), lambda i: (i, 0))

    wglu = w_glu.astype(BF16)
    woa = w_out[:D_ATTN].astype(BF16)
    wos = w_out[D_ATTN:].astype(BF16)
    return pl.pallas_call(
        _out_kernel,
        out_shape=jax.ShapeDtypeStruct((n, d), F32),
        grid=(n // tm,),
        in_specs=[row_spec(D_ATTN), row_spec(D_SSM), row_spec(D_SSM), row_spec(d), mod_spec,
                  _const_spec(wglu.shape), _const_spec((1, D_SSM)), _const_spec(woa.shape),
                  _const_spec(wos.shape), _const_spec((1, d))],
        out_specs=row_spec(d),
        compiler_params=pltpu.CompilerParams(dimension_semantics=("arbitrary",),
                                             vmem_limit_bytes=V7X_VMEM_LIMIT_BYTES),
        name="out_proj",
    )(a, ys, gs, x2d, gate, wglu, b_glu.reshape(1, D_SSM), woa, wos, g_post.reshape(1, d))


def _to_groups(u2d, n_seq, n_chunks, t_len):
    u5 = u2d.reshape(n_seq, n_chunks, t_len, N_SSM_GROUPS, SSM_GROUP)
    return u5.transpose(3, 0, 1, 2, 4).reshape(N_SSM_GROUPS, n_seq * n_chunks, t_len * SSM_GROUP)


def _from_groups(yg, n_seq, n_chunks, t_len):
    y5 = yg.reshape(N_SSM_GROUPS, n_seq, n_chunks, t_len, SSM_GROUP)
    return y5.transpose(1, 2, 3, 0, 4).reshape(n_seq * n_chunks * t_len, D_SSM)


def _pad_rows(a, rows):
    return jnp.pad(a, ((0, 0), (0, rows - a.shape[1]), (0, 0)))


def _layer(x, mod, past, g_pre, g_post, weights, prep, d_skip, w_glu, b_glu, w_out):
    bsz, s, d = x.shape
    n = bsz * s
    x2d = x.reshape(n, d)
    shift, scale, gate = mod[:, :d], mod[:, d:2 * d], mod[:, 2 * d:]
    if past is None:
        tm = 256
        rows_per_mod = s
        mods = [m.reshape(bsz, 1, d) for m in (scale, shift, gate)]
    else:
        tm = n
        rows_per_mod = None
        mods = [jnp.repeat(m, s, axis=0) for m in (scale, shift, gate)]
    scale_m, shift_m, gate_m = mods

    (q, k32, kb, v32, vb, iq, ik32, iklo, ikhi, iw, ga, u, gs) = _proj(
        x2d, scale_m, shift_m, g_pre.reshape(1, d), weights, tm=tm, rows_per_mod=rows_per_mod)

    def seq(a):
        return a.reshape(bsz, s, a.shape[-1])

    if past is None:
        attn = _dsa(seq(q), seq(iq), seq(iw), seq(ga), seq(kb), seq(vb), seq(iklo), seq(ikhi),
                    qb=Q_BLOCK, causal=True, n_keys=s)
        t_len = CHUNK if s % CHUNK == 0 else s
        h0r = jnp.zeros((N_SSM_GROUPS, bsz, SSM_STATE), F32)
        h0i = h0r
    else:
        ck, cv, cik, sre, sim = past
        past_len = ck.shape[1]
        n_keys = past_len + s
        lk = -(-n_keys // KEY_TILE) * KEY_TILE
        k_all = jnp.concatenate([ck.reshape(bsz, past_len, KV_DIM).astype(BF16), seq(kb)], axis=1)
        v_all = jnp.concatenate([cv.reshape(bsz, past_len, KV_DIM).astype(BF16), seq(vb)], axis=1)
        cik_b = cik.astype(BF16)
        zeros = jnp.zeros_like(cik_b)
        iklo_all = jnp.concatenate([jnp.concatenate([cik_b, zeros], axis=-1), seq(iklo)], axis=1)
        ikhi_all = jnp.concatenate([jnp.concatenate([zeros, cik_b], axis=-1), seq(ikhi)], axis=1)
        attn = _dsa(seq(q), seq(iq), seq(iw), seq(ga), _pad_rows(k_all, lk), _pad_rows(v_all, lk),
                    _pad_rows(iklo_all, lk), _pad_rows(ikhi_all, lk),
                    qb=s, causal=False, n_keys=n_keys)
        t_len = CHUNK if s % CHUNK == 0 else s
        h0r = jnp.swapaxes(sre.astype(F32), 0, 1)
        h0i = jnp.swapaxes(sim.astype(F32), 0, 1)

    n_chunks = s // t_len
    xg = _to_groups(u, bsz, n_chunks, t_len)
    dvec = jnp.tile(d_skip.astype(F32).reshape(N_SSM_GROUPS, 1, SSM_GROUP), (1, 1, t_len))
    rows = bsz * n_chunks
    rows_pad = -(-rows // 16) * 16
    xg = _pad_rows(xg, rows_pad) if rows_pad != rows else xg
    if rows_pad != rows:
        assert n_chunks == 1
        h0r = _pad_rows(h0r, rows_pad)
        h0i = _pad_rows(h0i, rows_pad)
    yg, hr, hi = _ssm(xg, prep, h0r, h0i, dvec, t_len=t_len,
                      n_seq=rows_pad // n_chunks, n_chunks=n_chunks)
    ys = _from_groups(yg[:, :rows], bsz, n_chunks, t_len)
    hr = jnp.swapaxes(hr[:, :bsz], 0, 1)
    hi = jnp.swapaxes(hi[:, :bsz], 0, 1)

    y = _out_proj(attn.reshape(n, D_ATTN), ys, gs, x2d, gate_m, w_glu, b_glu, w_out, g_post,
                  tm=tm, rows_per_mod=rows_per_mod)
    k_out = k32.reshape(bsz, s, N_KV_HEADS, HEAD_DIM)
    v_out = v32.reshape(bsz, s, N_KV_HEADS, HEAD_DIM)
    ik_out = ik32.reshape(bsz, s, IDX_DIM)
    return y.reshape(bsz, s, d), (k_out, v_out, ik_out, hr, hi)


def kernel(x_prompt, x_sample, c_prompt, c_sample, cache_k, cache_v, cache_idx_k, state_ssm_re,
           state_ssm_im, w_ada, b_ada, g_pre, g_post, w_in, lambda_re, lambda_im, log_dt, b_re, b_im,
           c_re, c_im, d_skip, w_glu, b_glu, w_out):
    depth = w_ada.shape[0]
    bp = c_prompt.shape[0]
    bs = c_sample.shape[0]
    yp, ys = x_prompt, x_sample
    outs_p = [[] for _ in range(5)]
    outs_s = [[] for _ in range(5)]
    c_all = jnp.concatenate([c_prompt, c_sample], axis=0)
    c_rows = -(-c_all.shape[0] // 8) * 8
    c_all = jnp.pad(c_all, ((0, c_rows - c_all.shape[0]), (0, 0)))
    for l in range(depth):
        mod = _adaln(c_all, w_ada[l], b_ada[l].reshape(1, -1))
        weights = _split_w_in(w_in[l])
        prep = _ssm_prep(lambda_re[l], lambda_im[l], log_dt[l], b_re[l], b_im[l], c_re[l], c_im[l])
        common = (g_pre[l], g_post[l], weights, prep, d_skip[l], w_glu[l], b_glu[l], w_out[l])
        yp, new_p = _layer(yp, mod[:bp], None, *common)
        past = (cache_k[l], cache_v[l], cache_idx_k[l], state_ssm_re[l], state_ssm_im[l])
        ys, new_s = _layer(ys, mod[bp:bp + bs], past, *common)
        for acc, val in zip(outs_p, new_p):
            acc.append(val)
        for acc, val in zip(outs_s, new_s):
            acc.append(val)
    return (yp, ys) + tuple(jnp.stack(a) for a in outs_p) + tuple(jnp.stack(a) for a in outs_s)
```

```python
import functools
import math

import jax
import jax.numpy as jnp
from jax import lax
from jax.experimental import pallas as pl
from jax.experimental.pallas import tpu as pltpu

F32 = jnp.float32
BF16 = jnp.bfloat16
I32 = jnp.int32

CHUNK = 64
Q_BLOCK = 128
D_ATTN = 1024
N_HEADS = 8
N_KV_HEADS = 2
HEAD_DIM = 128
KV_DIM = N_KV_HEADS * HEAD_DIM
HEADS_PER_KV = N_HEADS // N_KV_HEADS
N_IDX_HEADS = 16
IDX_DIM = 64
IDX_PAIRS = N_IDX_HEADS // 2
TOPK_MAX = 256
D_SSM = 1024
SSM_GROUP = 16
N_SSM_GROUPS = D_SSM // SSM_GROUP
SSM_STATE = 64
RMS_EPS = 1e-6
IN_SIZES = (D_ATTN, KV_DIM, KV_DIM, N_IDX_HEADS * IDX_DIM, IDX_DIM, N_IDX_HEADS, D_ATTN, D_SSM, D_SSM)

LANES = 128
V7X_VMEM_LIMIT_BYTES = 56 * 1024 * 1024

INT_MIN = -(2 ** 31)
MASKED_LOGIT = -1e30
MIN_ROW_SUM = 2.0 ** -100
KEY_TILE = 512
SCORE_KEYS = 256
SCAN_BLOCK = 8
COARSE_STEPS = 9
FINE_STEPS = 12


def _const_spec(shape):
    nd = len(shape)
    return pl.BlockSpec(shape, lambda *_: (0,) * nd, pipeline_mode=pl.Buffered(1))


def _dot_nt(a, b):
    return lax.dot_general(a, b, (((1,), (1,)), ((), ())), preferred_element_type=F32)


def _adaln_kernel(c_ref, w_ref, b_ref, o_ref):
    o_ref[...] = jnp.dot(c_ref[...], w_ref[...], preferred_element_type=F32) + b_ref[...]


def _adaln(c, w_ada, b_ada):
    rows, d = c.shape
    n_out = w_ada.shape[1]
    tn = 512
    return pl.pallas_call(
        _adaln_kernel,
        out_shape=jax.ShapeDtypeStruct((rows, n_out), F32),
        grid=(n_out // tn,),
        in_specs=[pl.BlockSpec((rows, d), lambda j: (0, 0)),
                  pl.BlockSpec((d, tn), lambda j: (0, j)),
                  pl.BlockSpec((1, tn), lambda j: (0, j))],
        out_specs=pl.BlockSpec((rows, tn), lambda j: (0, j)),
        compiler_params=pltpu.CompilerParams(dimension_semantics=("arbitrary",)),
        name="adaln",
    )(c, w_ada, b_ada)


def _proj_kernel(x_ref, scale_ref, shift_ref, g_ref,
                 wq_ref, wk_ref, wv_ref, wiq_ref, wik_ref, wiw_ref, wza_ref, wu_ref, wzs_ref,
                 q_ref, k_ref, kb_ref, v_ref, vb_ref, iq_ref, ik_ref, iklo_ref, ikhi_ref,
                 iw_ref, ga_ref, u_ref, gs_ref):
    x = x_ref[...]
    ms = jnp.mean(x * x, axis=-1, keepdims=True)
    y = x * lax.rsqrt(ms + RMS_EPS) * g_ref[...]
    h = (y * (1.0 + scale_ref[...]) + shift_ref[...]).astype(BF16)

    def mm(w_ref):
        return jnp.dot(h, w_ref[...], preferred_element_type=F32)

    q_ref[...] = (mm(wq_ref) * (HEAD_DIM ** -0.5 * math.log2(math.e))).astype(BF16)
    rows = x.shape[0]
    zk = mm(wk_ref)
    zv = mm(wv_ref)
    for hd in range(N_KV_HEADS):
        k_ref[pl.ds(hd, rows, stride=N_KV_HEADS), :] = zk[:, hd * HEAD_DIM:(hd + 1) * HEAD_DIM]
        v_ref[pl.ds(hd, rows, stride=N_KV_HEADS), :] = zv[:, hd * HEAD_DIM:(hd + 1) * HEAD_DIM]
    kb_ref[...] = zk.astype(BF16)
    vb_ref[...] = zv.astype(BF16)
    iq_ref[...] = (mm(wiq_ref) * (IDX_DIM ** -0.5)).astype(BF16)
    zik = mm(wik_ref)
    ik_ref[...] = zik[:, :IDX_DIM]
    lane = lax.broadcasted_iota(I32, zik.shape, 1)
    iklo_ref[...] = jnp.where(lane < IDX_DIM, zik, 0.0).astype(BF16)
    ikhi_ref[...] = jnp.where(lane >= IDX_DIM, zik, 0.0).astype(BF16)
    iw_ref[...] = mm(wiw_ref)[:, :N_IDX_HEADS] * (N_IDX_HEADS ** -0.5)
    za = mm(wza_ref)
    ga_ref[...] = (za * jax.nn.sigmoid(za)).astype(BF16)
    u_ref[...] = mm(wu_ref).astype(BF16)
    zs = mm(wzs_ref)
    gs_ref[...] = (zs * jax.nn.sigmoid(zs)).astype(BF16)


def _split_w_in(w_in):
    offs = [0]
    for s in IN_SIZES:
        offs.append(offs[-1] + s)
    cols = [w_in[:, offs[i]:offs[i + 1]].astype(BF16) for i in range(len(IN_SIZES))]
    wq, wk, wv, wiq, wik, wiw, wza, wu, wzs = cols
    wik2 = jnp.concatenate([wik, wik], axis=1)
    wiw_p = jnp.pad(wiw, ((0, 0), (0, LANES - N_IDX_HEADS)))
    return (wq, wk, wv, wiq, wik2, wiw_p, wza, wu, wzs)


def _proj(x2d, scale, shift, g_pre, weights, *, tm, rows_per_mod):
    n, d = x2d.shape
    if rows_per_mod is None:
        mod_spec = pl.BlockSpec((tm, d), lambda i: (i, 0))
    else:
        tiles_per_mod = rows_per_mod // tm
        mod_spec = pl.BlockSpec((None, 1, d), lambda i: (i // tiles_per_mod, 0, 0))

    def row_spec(width):
        return pl.BlockSpec((tm, width), lambda i: (i, 0))

    out_defs = [
        (1, D_ATTN, BF16), (N_KV_HEADS, HEAD_DIM, F32), (1, KV_DIM, BF16), (N_KV_HEADS, HEAD_DIM, F32),
        (1, KV_DIM, BF16), (1, N_IDX_HEADS * IDX_DIM, BF16), (1, IDX_DIM, F32), (1, LANES, BF16),
        (1, LANES, BF16), (1, N_IDX_HEADS, F32), (1, D_ATTN, BF16), (1, D_SSM, BF16), (1, D_SSM, BF16)]
    return pl.pallas_call(
        _proj_kernel,
        out_shape=[jax.ShapeDtypeStruct((r * n, w), dt) for r, w, dt in out_defs],
        grid=(n // tm,),
        in_specs=[row_spec(d), mod_spec, mod_spec, _const_spec((1, d))]
                 + [_const_spec(w.shape) for w in weights],
        out_specs=[pl.BlockSpec((r * tm, w), lambda i: (i, 0)) for r, w, _ in out_defs],
        compiler_params=pltpu.CompilerParams(dimension_semantics=("arbitrary",),
                                             vmem_limit_bytes=V7X_VMEM_LIMIT_BYTES),
        name="in_proj",
    )(x2d, scale, shift, g_pre, *weights)


def _dsa_kernel(q_ref, iq_ref, iw_ref, ga_ref, kb_ref, vb_ref, iklo_ref, ikhi_ref, o_ref,
                sc_ref, wb_ref, iqs_ref, qs_ref, m_ref, l_ref, acc_ref, kn_ref,
                *, qb, tk, topk, causal, n_keys, n_keys_pad):
    blk = pl.program_id(1)
    chunks_per_tile = tk // LANES
    row = lax.broadcasted_iota(I32, (qb, LANES), 0)
    if causal:
        n_vis = (blk + 1) * qb
        limit = (((blk * qb + row) // CHUNK) + 1) * CHUNK
        n_tiles = (n_vis + tk - 1) // tk
    else:
        limit = jnp.full((qb, LANES), n_keys, I32)
        n_tiles = (n_keys + tk - 1) // tk

    for j in range(IDX_PAIRS):
        iqs_ref[j * qb:(j + 1) * qb, :] = iq_ref[:, j * LANES:(j + 1) * LANES]
    for kv in range(N_KV_HEADS):
        for hh in range(HEADS_PER_KV):
            c0 = (kv * HEADS_PER_KV + hh) * HEAD_DIM
            qs_ref[kv, hh * qb:(hh + 1) * qb, :] = q_ref[:, c0:c0 + HEAD_DIM]
    iw = iw_ref[...]
    for h in range(N_IDX_HEADS):
        wb_ref[h] = jnp.broadcast_to(iw[:, h:h + 1], (qb, LANES))

    lane = lax.broadcasted_iota(I32, (qb, LANES), 1)

    def score_keys(first_chunk, n_matmuls, carry):
        m1, m2 = carry
        iqs = iqs_ref[...]
        for part in range(n_matmuls):
            chunk0 = first_chunk + part * (SCORE_KEYS // LANES)
            k0 = pl.multiple_of(chunk0 * LANES, SCORE_KEYS)
            s_lo = _dot_nt(iqs, iklo_ref[pl.ds(k0, SCORE_KEYS), :])
            s_hi = _dot_nt(iqs, ikhi_ref[pl.ds(k0, SCORE_KEYS), :])
            for c in range(SCORE_KEYS // LANES):
                cs = slice(c * LANES, (c + 1) * LANES)
                acc = None
                for j in range(IDX_PAIRS):
                    rs = slice(j * qb, (j + 1) * qb)
                    term = (jnp.maximum(s_lo[rs, cs], 0.0) * wb_ref[2 * j]
                            + jnp.maximum(s_hi[rs, cs], 0.0) * wb_ref[2 * j + 1])
                    acc = term if acc is None else acc + term
                ci = chunk0 + c
                visible = (ci * LANES + lane) < limit
                sc_ref[ci] = jnp.where(visible, acc, jnp.nan)
                v = jnp.where(visible, acc, -jnp.inf)
                m1, v = jnp.maximum(m1, v), jnp.minimum(m1, v)
                m2 = jnp.maximum(m2, v)
        return m1, m2

    def over_tile_pairs(step, carry):
        carry = lax.fori_loop(0, n_tiles // 2, lambda t, c: step(2 * t, 2, c), carry)
        return lax.cond(n_tiles % 2 == 1, lambda c: step(n_tiles - 1, 1, c), lambda c: c, carry)

    ninf = jnp.full((qb, LANES), -jnp.inf, F32)
    m1, m2 = over_tile_pairs(
        lambda t0, nt, c: score_keys(t0 * chunks_per_tile, nt * tk // SCORE_KEYS, c), (ninf, ninf))

    def lane_fold(x, op):
        return jnp.broadcast_to(op(x, axis=1, keepdims=True), (qb, LANES))

    def count_rows(pred):
        def count_tile(t, cnt):
            for c in range(chunks_per_tile):
                ci = t * chunks_per_tile + c
                cnt = cnt + jnp.where(pred(sc_ref[ci], ci * LANES + lane), 1.0, 0.0)
            return cnt

        return lane_fold(lax.fori_loop(0, n_tiles, count_tile, jnp.zeros((qb, LANES), F32)), jnp.sum)

    def key_bisection():
        def ordered_to_f32(key):
            return lax.bitcast_convert_type(jnp.where(key < 0, key ^ jnp.int32(0x7FFFFFFF), key), F32)

        def bisect(i, carry):
            lo, cnt_lo = carry
            cand_key = lo + lax.shift_left(jnp.int32(1), 31 - i)
            cand = ordered_to_f32(cand_key)
            total = count_rows(lambda s, _: s >= cand)
            take = total >= topk
            return jnp.where(take, cand_key, lo), jnp.where(take, total, cnt_lo)

        lo, cnt_lo = lax.fori_loop(0, 32, bisect, (jnp.full((qb, LANES), INT_MIN, I32),
                                                   jnp.zeros((qb, LANES), F32)))
        return ordered_to_f32(lo), cnt_lo

    few = limit < topk

    bracketed = limit >= 2 * LANES
    top = m2 if topk > LANES else m1
    lo_f = jnp.where(bracketed, lane_fold(m2, jnp.min), 0.0)
    hi_f = jnp.where(bracketed, lane_fold(top, jnp.max), 0.0)
    hi_f = hi_f + (jnp.abs(hi_f) * 2.0 ** -20 + 1e-30)

    cnt_hi = lane_fold(jnp.where(m1 >= hi_f, 1.0, 0.0), jnp.sum)

    def halve(_, carry):
        lo_f, hi_f, cnt_hi = carry
        mid = lo_f + (hi_f - lo_f) * 0.5
        total = count_rows(lambda s, _: s >= mid)
        take = total >= topk
        return jnp.where(take, mid, lo_f), jnp.where(take, hi_f, mid), jnp.where(take, cnt_hi, total)

    lo_f, hi_f, n_above = lax.fori_loop(0, COARSE_STEPS, halve, (lo_f, hi_f, cnt_hi))

    def bucket_rows(rs):
        lo_r, hi_r = lo_f[rs], hi_f[rs]

        def bucket_tile(t, carry):
            b1, b2, b3, pop = carry
            for c in range(chunks_per_tile):
                s = sc_ref[t * chunks_per_tile + c, rs, :]
                inb = (s >= lo_r) & jnp.logical_not(s >= hi_r)
                v = jnp.where(inb, s, -jnp.inf)
                pop = pop + jnp.where(inb, 1.0, 0.0)
                b1, v = jnp.maximum(b1, v), jnp.minimum(b1, v)
                b2, v = jnp.maximum(b2, v), jnp.minimum(b2, v)
                b3 = jnp.maximum(b3, v)
            return b1, b2, b3, pop

        return lax.fori_loop(0, n_tiles, bucket_tile, (ninf[rs], ninf[rs], ninf[rs], jnp.zeros_like(lo_r)))

    halves = [bucket_rows(slice(h * qb // 2, (h + 1) * qb // 2)) for h in range(2)]
    b1, b2, b3, pop = (jnp.concatenate([half[k] for half in halves], axis=0) for k in range(4))
    want = topk - n_above

    transposed = qb == LANES
    if transposed:
        cands = jnp.concatenate([b1.T, b2.T, b3.T], axis=0)
        want_q = want.T[:1]
        axis = 0
    else:
        cands = jnp.concatenate([b1, b2, b3], axis=1)
        want_q = want[:, :1]
        axis = 1

    def walk(_, carry):
        prev, covered, tau_q, cnt_q = carry
        cur = jnp.max(jnp.where(cands < prev, cands, -jnp.inf), axis=axis, keepdims=True)
        covered_new = covered + jnp.sum(jnp.where(cands == cur, 1.0, 0.0), axis=axis, keepdims=True)
        found = (covered < want_q) & (covered_new >= want_q) & (cur > -jnp.inf)
        return cur, covered_new, jnp.where(found, cur, tau_q), jnp.where(found, covered_new, cnt_q)

    nan_q = jnp.full(want_q.shape, jnp.nan, F32)
    _, _, tau_q, cnt_q = lax.fori_loop(
        0, FINE_STEPS, walk,
        (jnp.full(want_q.shape, jnp.inf, F32), jnp.zeros(want_q.shape, F32), nan_q, nan_q))
    if transposed:
        tau_fast = jnp.broadcast_to(tau_q, (LANES, qb)).T
        cnt_fast = n_above + jnp.broadcast_to(cnt_q, (LANES, qb)).T
    else:
        tau_fast = jnp.broadcast_to(tau_q, (qb, LANES))
        cnt_fast = n_above + jnp.broadcast_to(cnt_q, (qb, LANES))

    bad = jnp.logical_not(few) & (jnp.logical_not(bracketed) | (pop > 3.0) | (want < 1.0)
                                  | jnp.logical_not(tau_fast == tau_fast))
    tau, cnt_tau = lax.cond(jnp.max(jnp.where(bad, 1.0, 0.0)) > 0.0, key_bisection,
                            lambda: (tau_fast, cnt_fast))
    tau = jnp.where(few, -jnp.inf, tau)
    excess = jnp.where(few, 0.0, cnt_tau - topk)

    @pl.when(jnp.max(excess) > 0)
    def _resolve_surplus():
        idx_bits = (n_keys_pad - 1).bit_length()

        def kept(s, kidx, vstar, jstar):
            return (s >= tau) & ((s > vstar) | ((s == vstar) & (kidx < jstar)))

        def drop_step(state):
            exc, vstar, jstar = state

            def min_tile(t, vm):
                for c in range(chunks_per_tile):
                    ci = t * chunks_per_tile + c
                    s = sc_ref[ci]
                    vm = jnp.minimum(vm, jnp.where(kept(s, ci * LANES + lane, vstar, jstar), s, jnp.inf))
                return vm

            vm = lax.fori_loop(0, n_tiles, min_tile, jnp.full((qb, LANES), jnp.inf, F32))
            vmin = jnp.broadcast_to(jnp.min(vm, axis=1, keepdims=True), (qb, LANES))

            def tied(s, kidx):
                return kept(s, kidx, vstar, jstar) & (s == vmin)

            cmin = count_rows(tied)
            active = exc > 0
            drop_all = active & (exc >= cmin)
            partial = active & (exc < cmin)
            keep = cmin - exc

            def idx_bisect(b, j0):
                cand_j = j0 + lax.shift_left(jnp.int32(1), idx_bits - 1 - b)
                below = count_rows(lambda s, kidx: tied(s, kidx) & (kidx < cand_j))
                return jnp.where(below < keep, cand_j, j0)

            j0 = lax.fori_loop(0, idx_bits, idx_bisect, jnp.zeros((qb, LANES), I32))
            vstar = jnp.where(active, vmin, vstar)
            jstar = jnp.where(drop_all, 0, jnp.where(partial, j0 + 1, jstar))
            exc = jnp.where(drop_all, exc - cmin, jnp.where(partial, 0, exc))
            return exc, vstar, jstar

        _, vstar, jstar = lax.while_loop(
            lambda state: jnp.max(state[0]) > 0, drop_step,
            (excess, tau, jnp.full((qb, LANES), 2 ** 30, I32)))

        def rewrite_tile(t, carry):
            for c in range(chunks_per_tile):
                ci = t * chunks_per_tile + c
                s = sc_ref[ci]
                dropped = (s >= tau) & jnp.logical_not(kept(s, ci * LANES + lane, vstar, jstar))
                sc_ref[ci] = jnp.where(dropped, jnp.nan, s)
            return carry

        lax.fori_loop(0, n_tiles, rewrite_tile, 0)

    def masked_logits(t0, nt, kv):
        ks = pl.multiple_of(t0 * tk, tk)
        bias = jnp.concatenate(
            [jnp.where(sc_ref[t0 * chunks_per_tile + c] >= tau, 0.0, MASKED_LOGIT)
             for c in range(nt * chunks_per_tile)], axis=1)
        s = _dot_nt(qs_ref[kv], kb_ref[pl.ds(ks, nt * tk), kv * HEAD_DIM:(kv + 1) * HEAD_DIM])
        return (s.reshape(HEADS_PER_KV, qb, nt * tk) + bias[None]).reshape(HEADS_PER_KV * qb, nt * tk)

    def attend_tiles(t0, nt, carry):
        ks = pl.multiple_of(t0 * tk, tk)
        for kv in range(N_KV_HEADS):
            s = masked_logits(t0, nt, kv)
            m = m_ref[kv]
            l = l_ref[kv]
            ps = []
            for c in range(nt * chunks_per_tile):
                e = jnp.exp2(s[:, c * LANES:(c + 1) * LANES] - m)
                l = l + e
                ps.append(e.astype(BF16))
            l_ref[kv] = l
            p = jnp.concatenate(ps, axis=1)
            acc_ref[kv] += jnp.dot(p, vb_ref[pl.ds(ks, nt * tk), kv * HEAD_DIM:(kv + 1) * HEAD_DIM],
                                   preferred_element_type=F32)
        return carry

    def attend():
        l_ref[...] = jnp.zeros(l_ref.shape, F32)
        acc_ref[...] = jnp.zeros(acc_ref.shape, F32)
        over_tile_pairs(attend_tiles, 0)

    @pl.when(blk == 0)
    def _key_norms():
        for kv in range(N_KV_HEADS):
            def norm_tile(t, mx):
                ks = pl.multiple_of(t * tk, tk)
                k = kb_ref[pl.ds(ks, tk), kv * HEAD_DIM:(kv + 1) * HEAD_DIM].astype(F32)
                return jnp.maximum(mx, jnp.sum(k * k, axis=1, keepdims=True))

            mx = lax.fori_loop(0, n_keys_pad // tk, norm_tile, jnp.zeros((tk, 1), F32))
            kn_ref[kv] = jnp.broadcast_to(jnp.sqrt(jnp.max(mx, axis=0, keepdims=True)), kn_ref.shape[1:])

    for kv in range(N_KV_HEADS):
        qf = qs_ref[kv].astype(F32)
        qn = jnp.sqrt(jnp.sum(qf * qf, axis=1, keepdims=True))
        m_ref[kv] = jnp.broadcast_to(qn, m_ref.shape[1:]) * kn_ref[kv, 0:1, :]
    attend()

    l_min = jnp.min(jnp.sum(l_ref[...], axis=2))

    @pl.when(jnp.logical_not(l_min >= MIN_ROW_SUM))
    def _exact_shift():
        m_ref[...] = jnp.full(m_ref.shape, MASKED_LOGIT, F32)

        def max_tile(t, carry):
            for kv in range(N_KV_HEADS):
                s = masked_logits(t, 1, kv)
                mx = m_ref[kv]
                for c in range(chunks_per_tile):
                    mx = jnp.maximum(mx, s[:, c * LANES:(c + 1) * LANES])
                m_ref[kv] = mx
            return carry

        lax.fori_loop(0, n_tiles, max_tile, 0)
        for kv in range(N_KV_HEADS):
            m_ref[kv] = jnp.broadcast_to(jnp.max(m_ref[kv], axis=1, keepdims=True), m_ref.shape[1:])
        attend()

    for kv in range(N_KV_HEADS):
        o = acc_ref[kv] / jnp.sum(l_ref[kv], axis=1, keepdims=True)
        for hh in range(HEADS_PER_KV):
            c0 = (kv * HEADS_PER_KV + hh) * HEAD_DIM
            gate = ga_ref[:, c0:c0 + HEAD_DIM].astype(F32)
            o_ref[:, c0:c0 + HEAD_DIM] = (o[hh * qb:(hh + 1) * qb] * gate).astype(BF16)


def _dsa(q, iq, iw, ga, kb, vb, iklo, ikhi, *, qb, causal, n_keys):
    b, s, _ = q.shape
    lk = kb.shape[1]
    tk = KEY_TILE
    assert lk % tk == 0 and s % qb == 0
    topk = min(TOPK_MAX, n_keys // 4)
    kern = functools.partial(_dsa_kernel, qb=qb, tk=tk, topk=topk, causal=causal, n_keys=n_keys,
                             n_keys_pad=lk)

    def q_spec(width):
        return pl.BlockSpec((None, qb, width), lambda bi, i: (bi, i, 0))

    def kv_spec(width):
        return pl.BlockSpec((None, lk, width), lambda bi, i: (bi, 0, 0))

    return pl.pallas_call(
        kern,
        out_shape=jax.ShapeDtypeStruct((b, s, D_ATTN), BF16),
        grid=(b, s // qb),
        in_specs=[q_spec(D_ATTN), q_spec(N_IDX_HEADS * IDX_DIM), q_spec(N_IDX_HEADS), q_spec(D_ATTN),
                  kv_spec(KV_DIM), kv_spec(KV_DIM), kv_spec(LANES), kv_spec(LANES)],
        out_specs=q_spec(D_ATTN),
        scratch_shapes=[
            pltpu.VMEM((lk // LANES, qb, LANES), F32),
            pltpu.VMEM((N_IDX_HEADS, qb, LANES), F32),
            pltpu.VMEM((IDX_PAIRS * qb, LANES), BF16),
            pltpu.VMEM((N_KV_HEADS, HEADS_PER_KV * qb, HEAD_DIM), BF16),
            pltpu.VMEM((N_KV_HEADS, HEADS_PER_KV * qb, LANES), F32),
            pltpu.VMEM((N_KV_HEADS, HEADS_PER_KV * qb, LANES), F32),
            pltpu.VMEM((N_KV_HEADS, HEADS_PER_KV * qb, HEAD_DIM), F32),
            pltpu.VMEM((N_KV_HEADS, 8, LANES), F32),
        ],
        compiler_params=pltpu.CompilerParams(dimension_semantics=("arbitrary", "arbitrary"),
                                             vmem_limit_bytes=V7X_VMEM_LIMIT_BYTES),
        name="dsa",
    )(q, iq, iw, ga, kb, vb, iklo, ikhi)


def _ssm_prep_kernel(lre_c_ref, lim_c_ref, lre_r_ref, lim_r_ref, ldt_ref, btr_ref, bti_ref,
                     ctr_ref, cti_ref, kflat_ref, mtr_ref, mti_ref, emr_ref, emi_ref,
                     a64r_ref, a64i_ref, a16r_ref, a16i_ref):
    hp = lax.Precision.HIGHEST
    dt = jnp.exp(ldt_ref[...])
    n_lag = CHUNK
    width = n_lag * SSM_GROUP

    ldr_c = lre_c_ref[...] * dt
    ldi_c = lim_c_ref[...] * dt
    lag = lax.broadcasted_iota(I32, (1, n_lag), 1).astype(F32)
    mag = jnp.exp(ldr_c * lag)
    pr_lag = mag * jnp.cos(ldi_c * lag)
    pi_lag = mag * jnp.sin(ldi_c * lag)
    col = lax.broadcasted_iota(I32, (n_lag, width), 1)
    pick_lag = (col // SSM_GROUP == lax.broadcasted_iota(I32, (n_lag, width), 0)).astype(F32)
    col = lax.broadcasted_iota(I32, (SSM_GROUP, width), 1)
    pick_ch = (col % SSM_GROUP == lax.broadcasted_iota(I32, (SSM_GROUP, width), 0)).astype(F32)
    pr = jnp.dot(pr_lag, pick_lag, precision=hp, preferred_element_type=F32)
    pi = jnp.dot(pi_lag, pick_lag, precision=hp, preferred_element_type=F32)
    ctr = jnp.dot(ctr_ref[...], pick_ch, precision=hp, preferred_element_type=F32)
    cti = jnp.dot(cti_ref[...], pick_ch, precision=hp, preferred_element_type=F32)
    qr = pr * ctr - pi * cti
    qi = pr * cti + pi * ctr

    lre_r = lre_r_ref[...]
    lim_r = lim_r_ref[...]
    ldr_r = lre_r * dt
    ldi_r = lim_r * dt
    lbr = jnp.exp(ldr_r) * jnp.cos(ldi_r)
    lbi = jnp.exp(ldr_r) * jnp.sin(ldi_r)
    den = lre_r * lre_r + lim_r * lim_r
    nr = lbr - 1.0
    fr = (nr * lre_r + lbi * lim_r) / den
    fi = (lbi * lre_r - nr * lim_r) / den
    btr = btr_ref[...]
    bti = bti_ref[...]
    bbr = fr * btr - fi * bti
    bbi = fr * bti + fi * btr

    kflat_ref[...] = (jnp.dot(bbr, qr, precision=hp, preferred_element_type=F32)
                      - jnp.dot(bbi, qi, precision=hp, preferred_element_type=F32))

    lbr_c = jnp.exp(ldr_c) * jnp.cos(ldi_c)
    lbi_c = jnp.exp(ldr_c) * jnp.sin(ldi_c)
    emr_ref[...] = (lbr_c * qr - lbi_c * qi).astype(BF16)
    emi_ref[...] = (-(lbr_c * qi + lbi_c * qr)).astype(BF16)

    back = (n_lag - 1 - lax.broadcasted_iota(I32, (n_lag, 1), 0)).astype(F32)
    bmag = jnp.exp(ldr_r * back)
    bpr = bmag * jnp.cos(ldi_r * back)
    bpi = bmag * jnp.sin(ldi_r * back)
    for t in range(n_lag):
        wr = bpr[t:t + 1, :]
        wi = bpi[t:t + 1, :]
        rows = slice(t * SSM_GROUP, (t + 1) * SSM_GROUP)
        mtr_ref[rows, :] = (wr * bbr - wi * bbi).astype(BF16)
        mti_ref[rows, :] = (wr * bbi + wi * bbr).astype(BF16)

    for steps, ar_ref, ai_ref in ((float(CHUNK), a64r_ref, a64i_ref), (16.0, a16r_ref, a16i_ref)):
        amag = jnp.exp(ldr_r * steps)
        ar_ref[...] = amag * jnp.cos(ldi_r * steps)
        ai_ref[...] = amag * jnp.sin(ldi_r * steps)


def _ssm_prep(lambda_re, lambda_im, log_dt, b_re, b_im, c_re, c_im):
    g, p = lambda_re.shape
    width = CHUNK * SSM_GROUP
    lre_c = lambda_re.reshape(g, p, 1)
    lim_c = lambda_im.reshape(g, p, 1)
    lre_r = lambda_re.reshape(g, 1, p)
    lim_r = lambda_im.reshape(g, 1, p)
    ldt = log_dt.reshape(g, 1, 1)
    btr = jnp.swapaxes(b_re, 1, 2)
    bti = jnp.swapaxes(b_im, 1, 2)
    ctr = jnp.swapaxes(c_re, 1, 2)
    cti = jnp.swapaxes(c_im, 1, 2)

    def gspec(*shape):
        return pl.BlockSpec((None,) + shape, lambda i: (i,) + (0,) * len(shape))

    out_defs = [((SSM_GROUP, width), F32), ((width, p), BF16), ((width, p), BF16),
                ((p, width), BF16), ((p, width), BF16),
                ((1, p), F32), ((1, p), F32), ((1, p), F32), ((1, p), F32)]
    return pl.pallas_call(
        _ssm_prep_kernel,
        out_shape=[jax.ShapeDtypeStruct((g,) + s, dt) for s, dt in out_defs],
        grid=(g,),
        in_specs=[gspec(p, 1), gspec(p, 1), gspec(1, p), gspec(1, p), gspec(1, 1),
                  gspec(SSM_GROUP, p), gspec(SSM_GROUP, p), gspec(p, SSM_GROUP), gspec(p, SSM_GROUP)],
        out_specs=[gspec(*s) for s, _ in out_defs],
        compiler_params=pltpu.CompilerParams(dimension_semantics=("arbitrary",)),
        name="ssm_prep",
    )(lre_c, lim_c, lre_r, lim_r, ldt, btr, bti, ctr, cti)


def _ssm_group(x, kflat_ref, mtr_ref, mti_ref, emr_ref, emi_ref, ar_ref, ai_ref,
               h0r_ref, h0i_ref, dvec_ref, hr_ref, hi_ref,
               toep_ref, gr_ref, gi_ref, hpr_ref, hpi_ref, *, t_len, n_seq, n_chunks):
    width = t_len * SSM_GROUP
    per_vreg = LANES // SSM_GROUP
    kflat = kflat_ref[:, :width]
    kext = jnp.concatenate([kflat, jnp.zeros((SSM_GROUP, LANES), F32)], axis=1)
    shifted = [kext] + [pltpu.roll(kext, r * SSM_GROUP, axis=1) for r in range(1, per_vreg)]
    for t in range(t_len):
        q, r = divmod(t, per_vreg)
        blk = shifted[r][:, :width - q * LANES]
        if q:
            blk = jnp.concatenate([jnp.zeros((SSM_GROUP, q * LANES), F32), blk], axis=1)
        toep_ref[t * SSM_GROUP:(t + 1) * SSM_GROUP, :] = blk.astype(BF16)

    gr_ref[...] = jnp.dot(x, mtr_ref[...], preferred_element_type=F32)
    gi_ref[...] = jnp.dot(x, mti_ref[...], preferred_element_type=F32)

    ar = ar_ref[...]
    ai = ai_ref[...]

    def advance(h_r, h_i, a_r, a_i, g_r, g_i):
        return a_r * h_r - a_i * h_i + g_r, a_r * h_i + a_i * h_r + g_i

    block = _scan_block(n_chunks)
    if block == 1:
        def chunk_step(n, carry):
            h_r, h_i = carry
            rows = pl.ds(n * n_seq, n_seq)
            hpr_ref[rows, :] = h_r
            hpi_ref[rows, :] = h_i
            return advance(h_r, h_i, ar, ai, gr_ref[rows, :], gi_ref[rows, :])

        if n_chunks == 1:
            h_r, h_i = chunk_step(0, (h0r_ref[...], h0i_ref[...]))
        else:
            h_r, h_i = lax.fori_loop(0, n_chunks, chunk_step, (h0r_ref[...], h0i_ref[...]))
    else:
        n_blocks = n_chunks // block
        slab = n_blocks * n_seq
        zero = jnp.zeros((slab, SSM_STATE), F32)
        l_r, l_i = zero, zero
        local = []
        for j in range(block):
            local.append((l_r, l_i))
            rows = slice(j * slab, (j + 1) * slab)
            l_r, l_i = advance(l_r, l_i, ar, ai, gr_ref[rows, :], gi_ref[rows, :])
        powers = [(jnp.ones_like(ar), jnp.zeros_like(ar))]
        for j in range(block):
            p_r, p_i = powers[-1]
            powers.append((p_r * ar - p_i * ai, p_r * ai + p_i * ar))
        ab_r, ab_i = powers[block]
        h_r, h_i = h0r_ref[...], h0i_ref[...]
        entry = []
        for b in range(n_blocks):
            entry.append((h_r, h_i))
            rows = slice(b * n_seq, (b + 1) * n_seq)
            h_r, h_i = advance(h_r, h_i, ab_r, ab_i, l_r[rows], l_i[rows])
        e_r = jnp.concatenate([e[0] for e in entry], axis=0)
        e_i = jnp.concatenate([e[1] for e in entry], axis=0)
        for j in range(block):
            p_r, p_i = powers[j]
            rows = slice(j * slab, (j + 1) * slab)
            hpr_ref[rows, :] = local[j][0] + p_r * e_r - p_i * e_i
            hpi_ref[rows, :] = local[j][1] + p_r * e_i + p_i * e_r
    hr_ref[...] = h_r
    hi_ref[...] = h_i

    y = jnp.dot(x, toep_ref[...], preferred_element_type=F32)
    y = y + jnp.dot(hpr_ref[...].astype(BF16), emr_ref[:, :width], preferred_element_type=F32)
    y = y + jnp.dot(hpi_ref[...].astype(BF16), emi_ref[:, :width], preferred_element_type=F32)
    return y + x.astype(F32) * dvec_ref[...]


def _ssm_kernel(x_ref, kflat_ref, mtr_ref, mti_ref, emr_ref, emi_ref, ar_ref, ai_ref,
                h0r_ref, h0i_ref, dvec_ref, y_ref, hr_ref, hi_ref,
                toep_ref, gr_ref, gi_ref, hpr_ref, hpi_ref, **static):
    y = _ssm_group(x_ref[...], kflat_ref, mtr_ref, mti_ref, emr_ref, emi_ref, ar_ref, ai_ref,
                   h0r_ref, h0i_ref, dvec_ref, hr_ref, hi_ref,
                   toep_ref, gr_ref, gi_ref, hpr_ref, hpi_ref, **static)
    y_ref[...] = y.astype(BF16)


def _ssm(xg, prep, h0r, h0i, dvec, *, t_len, n_seq, n_chunks):
    kflat, mtr, mti, emr, emi, a64r, a64i, a16r, a16i = prep
    g, rows, width = xg.shape
    p = SSM_STATE
    full = CHUNK * SSM_GROUP
    if t_len == CHUNK:
        ar, ai = a64r, a64i
        mt_block = 0
    else:
        assert t_len == 16
        ar, ai = a16r, a16i
        mt_block = (full - width) // width
    kern = functools.partial(_ssm_kernel, t_len=t_len, n_seq=n_seq, n_chunks=n_chunks)

    def gspec(*shape):
        return pl.BlockSpec((None,) + shape, lambda i: (i,) + (0,) * len(shape))

    mt_spec = pl.BlockSpec((None, width, p), lambda i: (i, mt_block, 0))
    return pl.pallas_call(
        kern,
        out_shape=[jax.ShapeDtypeStruct((g, rows, width), BF16),
                   jax.ShapeDtypeStruct((g, n_seq, p), F32),
                   jax.ShapeDtypeStruct((g, n_seq, p), F32)],
        grid=(g,),
        in_specs=[gspec(rows, width), gspec(SSM_GROUP, full), mt_spec, mt_spec,
                  gspec(p, full), gspec(p, full), gspec(1, p), gspec(1, p),
                  gspec(n_seq, p), gspec(n_seq, p), gspec(1, width)],
        out_specs=[gspec(rows, width), gspec(n_seq, p), gspec(n_seq, p)],
        scratch_shapes=[pltpu.VMEM((width, width), BF16),
                        pltpu.VMEM((rows, p), F32), pltpu.VMEM((rows, p), F32),
                        pltpu.VMEM((rows, p), F32), pltpu.VMEM((rows, p), F32)],
        compiler_params=pltpu.CompilerParams(dimension_semantics=("arbitrary",)),
        name="ssm",
    )(xg, kflat, mtr, mti, emr, emi, ar, ai, h0r, h0i, dvec)


def _out_kernel(a_ref, ys_ref, gs_ref, x_ref, gate_ref, wglu_ref, bglu_ref, woa_ref, wos_ref,
                gpost_ref, o_ref):
    y = ys_ref[...].astype(F32)
    y = 0.5 * y * (1.0 + jnp.tanh(math.sqrt(2.0 / math.pi) * (y + 0.044715 * (y * y * y))))
    z = jnp.dot(y.astype(BF16), wglu_ref[...], preferred_element_type=F32) + bglu_ref[...]
    y = y * jax.nn.sigmoid(z) * gs_ref[...].astype(F32)
    out = (jnp.dot(a_ref[...], woa_ref[...], preferred_element_type=F32)
           + jnp.dot(y.astype(BF16), wos_ref[...], preferred_element_type=F32))
    ms = jnp.mean(out * out, axis=-1, keepdims=True)
    normed = out * lax.rsqrt(ms + RMS_EPS) * gpost_ref[...]
    o_ref[...] = x_ref[...] + gate_ref[...] * normed


def _out_proj(a, ys, gs, x2d, gate, w_glu, b_glu, w_out, g_post, *, tm, rows_per_mod):
    n, d = x2d.shape
    if rows_per_mod is None:
        mod_spec = pl.BlockSpec((tm, d), lambda i: (i, 0))
    else:
        tiles_per_mod = rows_per_mod // tm
        mod_spec = pl.BlockSpec((None, 1, d), lambda i: (i // tiles_per_mod, 0, 0))

    def row_spec(width):
        return pl.BlockSpec((tm, width), lambda i: (i, 0))

    wglu = w_glu.astype(BF16)
    woa = w_out[:D_ATTN].astype(BF16)
    wos = w_out[D_ATTN:].astype(BF16)
    return pl.pallas_call(
        _out_kernel,
        out_shape=jax.ShapeDtypeStruct((n, d), F32),
        grid=(n // tm,),
        in_specs=[row_spec(D_ATTN), row_spec(D_SSM), row_spec(D_SSM), row_spec(d), mod_spec,
                  _const_spec(wglu.shape), _const_spec((1, D_SSM)), _const_spec(woa.shape),
                  _const_spec(wos.shape), _const_spec((1, d))],
        out_specs=row_spec(d),
        compiler_params=pltpu.CompilerParams(dimension_semantics=("arbitrary",),
                                             vmem_limit_bytes=V7X_VMEM_LIMIT_BYTES),
        name="out_proj",
    )(a, ys, gs, x2d, gate, wglu, b_glu.reshape(1, D_SSM), woa, wos, g_post.reshape(1, d))


def _scan_block(n_chunks):
    return SCAN_BLOCK if n_chunks % SCAN_BLOCK == 0 else 1


def _to_groups(u2d, n_seq, n_chunks, t_len):
    j = _scan_block(n_chunks)
    u6 = u2d.reshape(n_seq, n_chunks // j, j, t_len, N_SSM_GROUPS, SSM_GROUP)
    return u6.transpose(4, 2, 1, 0, 3, 5).reshape(N_SSM_GROUPS, n_seq * n_chunks, t_len * SSM_GROUP)


def _from_groups(yg, n_seq, n_chunks, t_len):
    j = _scan_block(n_chunks)
    y6 = yg.reshape(N_SSM_GROUPS, j, n_chunks // j, n_seq, t_len, SSM_GROUP)
    return y6.transpose(3, 2, 1, 4, 0, 5).reshape(n_seq * n_chunks * t_len, D_SSM)


def _pad_rows(a, rows):
    return jnp.pad(a, ((0, 0), (0, rows - a.shape[1]), (0, 0)))


def _layer(x, mod, past, g_pre, g_post, weights, prep, d_skip, w_glu, b_glu, w_out):
    bsz, s, d = x.shape
    n = bsz * s
    x2d = x.reshape(n, d)
    shift, scale, gate = mod[:, :d], mod[:, d:2 * d], mod[:, 2 * d:]
    if past is None:
        tm = 256
        rows_per_mod = s
        mods = [m.reshape(bsz, 1, d) for m in (scale, shift, gate)]
    else:
        tm = n
        rows_per_mod = None
        mods = [jnp.repeat(m, s, axis=0) for m in (scale, shift, gate)]
    scale_m, shift_m, gate_m = mods

    (q, k32, kb, v32, vb, iq, ik32, iklo, ikhi, iw, ga, u, gs) = _proj(
        x2d, scale_m, shift_m, g_pre.reshape(1, d), weights, tm=tm, rows_per_mod=rows_per_mod)

    def seq(a):
        return a.reshape(bsz, s, a.shape[-1])

    if past is None:
        attn = _dsa(seq(q), seq(iq), seq(iw), seq(ga), seq(kb), seq(vb), seq(iklo), seq(ikhi),
                    qb=Q_BLOCK, causal=True, n_keys=s)
        t_len = CHUNK if s % CHUNK == 0 else s
        h0r = jnp.zeros((N_SSM_GROUPS, bsz, SSM_STATE), F32)
        h0i = h0r
    else:
        ck, cv, cik, sre, sim = past
        past_len = ck.shape[1]
        n_keys = past_len + s
        lk = -(-n_keys // KEY_TILE) * KEY_TILE
        k_all = jnp.concatenate([ck.reshape(bsz, past_len, KV_DIM).astype(BF16), seq(kb)], axis=1)
        v_all = jnp.concatenate([cv.reshape(bsz, past_len, KV_DIM).astype(BF16), seq(vb)], axis=1)
        cik_b = cik.astype(BF16)
        zeros = jnp.zeros_like(cik_b)
        iklo_all = jnp.concatenate([jnp.concatenate([cik_b, zeros], axis=-1), seq(iklo)], axis=1)
        ikhi_all = jnp.concatenate([jnp.concatenate([zeros, cik_b], axis=-1), seq(ikhi)], axis=1)
        attn = _dsa(seq(q), seq(iq), seq(iw), seq(ga), _pad_rows(k_all, lk), _pad_rows(v_all, lk),
                    _pad_rows(iklo_all, lk), _pad_rows(ikhi_all, lk),
                    qb=s, causal=False, n_keys=n_keys)
        t_len = CHUNK if s % CHUNK == 0 else s
        h0r = jnp.swapaxes(sre.astype(F32), 0, 1)
        h0i = jnp.swapaxes(sim.astype(F32), 0, 1)

    n_chunks = s // t_len
    dvec = jnp.tile(d_skip.astype(F32).reshape(N_SSM_GROUPS, 1, SSM_GROUP), (1, 1, t_len))
    rows = bsz * n_chunks
    rows_pad = -(-rows // 16) * 16
    xg = _to_groups(u, bsz, n_chunks, t_len)
    xg = _pad_rows(xg, rows_pad) if rows_pad != rows else xg
    if rows_pad != rows:
        assert n_chunks == 1
        h0r = _pad_rows(h0r, rows_pad)
        h0i = _pad_rows(h0i, rows_pad)
    yg, hr, hi = _ssm(xg, prep, h0r, h0i, dvec, t_len=t_len,
                      n_seq=rows_pad // n_chunks, n_chunks=n_chunks)
    ys = _from_groups(yg[:, :rows], bsz, n_chunks, t_len)
    hr = jnp.swapaxes(hr[:, :bsz], 0, 1)
    hi = jnp.swapaxes(hi[:, :bsz], 0, 1)

    y = _out_proj(attn.reshape(n, D_ATTN), ys, gs, x2d, gate_m, w_glu, b_glu, w_out, g_post,
                  tm=tm, rows_per_mod=rows_per_mod)
    k_out = k32.reshape(bsz, s, N_KV_HEADS, HEAD_DIM)
    v_out = v32.reshape(bsz, s, N_KV_HEADS, HEAD_DIM)
    ik_out = ik32.reshape(bsz, s, IDX_DIM)
    return y.reshape(bsz, s, d), (k_out, v_out, ik_out, hr, hi)


def kernel(x_prompt, x_sample, c_prompt, c_sample, cache_k, cache_v, cache_idx_k, state_ssm_re,
           state_ssm_im, w_ada, b_ada, g_pre, g_post, w_in, lambda_re, lambda_im, log_dt, b_re, b_im,
           c_re, c_im, d_skip, w_glu, b_glu, w_out):
    depth = w_ada.shape[0]
    bp = c_prompt.shape[0]
    bs = c_sample.shape[0]
    yp, ys = x_prompt, x_sample
    outs_p = [[] for _ in range(5)]
    outs_s = [[] for _ in range(5)]
    c_all = jnp.concatenate([c_prompt, c_sample], axis=0)
    c_rows = -(-c_all.shape[0] // 8) * 8
    c_all = jnp.pad(c_all, ((0, c_rows - c_all.shape[0]), (0, 0)))
    for l in range(depth):
        mod = _adaln(c_all, w_ada[l], b_ada[l].reshape(1, -1))
        weights = _split_w_in(w_in[l])
        prep = _ssm_prep(lambda_re[l], lambda_im[l], log_dt[l], b_re[l], b_im[l], c_re[l], c_im[l])
        common = (g_pre[l], g_post[l], weights, prep, d_skip[l], w_glu[l], b_glu[l], w_out[l])
        yp, new_p = _layer(yp, mod[:bp], None, *common)
        past = (cache_k[l], cache_v[l], cache_idx_k[l], state_ssm_re[l], state_ssm_im[l])
        ys, new_s = _layer(ys, mod[bp:bp + bs], past, *common)
        for acc, val in zip(outs_p, new_p):
            acc.append(val)
        for acc, val in zip(outs_s, new_s):
            acc.append(val)
    return (yp, ys) + tuple(jnp.stack(a) for a in outs_p) + tuple(jnp.stack(a) for a in outs_s)
```

```python
import functools
import math

import jax
import jax.numpy as jnp
from jax import lax
from jax.experimental import pallas as pl
from jax.experimental.pallas import tpu as pltpu

F32 = jnp.float32
BF16 = jnp.bfloat16
I32 = jnp.int32

CHUNK = 64
Q_BLOCK = 128
D_ATTN = 1024
N_HEADS = 8
N_KV_HEADS = 2
HEAD_DIM = 128
KV_DIM = N_KV_HEADS * HEAD_DIM
HEADS_PER_KV = N_HEADS // N_KV_HEADS
N_IDX_HEADS = 16
IDX_DIM = 64
IDX_PAIRS = N_IDX_HEADS // 2
TOPK_MAX = 256
D_SSM = 1024
SSM_GROUP = 16
N_SSM_GROUPS = D_SSM // SSM_GROUP
SSM_STATE = 64
RMS_EPS = 1e-6
IN_SIZES = (D_ATTN, KV_DIM, KV_DIM, N_IDX_HEADS * IDX_DIM, IDX_DIM, N_IDX_HEADS, D_ATTN, D_SSM, D_SSM)

LANES = 128
V7X_VMEM_LIMIT_BYTES = 56 * 1024 * 1024

INT_MIN = -(2 ** 31)
MASKED_LOGIT = -1e30
MIN_ROW_SUM = 2.0 ** -100
KEY_TILE = 512
SCORE_KEYS = 256
SCAN_BLOCK = 8
COARSE_STEPS = 9
FINE_STEPS = 12


def _const_spec(shape):
    nd = len(shape)
    return pl.BlockSpec(shape, lambda *_: (0,) * nd, pipeline_mode=pl.Buffered(1))


def _dot_nt(a, b):
    return lax.dot_general(a, b, (((1,), (1,)), ((), ())), preferred_element_type=F32)


def _adaln_kernel(c_ref, w_ref, b_ref, o_ref):
    o_ref[...] = jnp.dot(c_ref[...], w_ref[...], preferred_element_type=F32) + b_ref[...]


def _adaln(c, w_ada, b_ada):
    rows, d = c.shape
    n_out = w_ada.shape[1]
    tn = 512
    return pl.pallas_call(
        _adaln_kernel,
        out_shape=jax.ShapeDtypeStruct((rows, n_out), F32),
        grid=(n_out // tn,),
        in_specs=[pl.BlockSpec((rows, d), lambda j: (0, 0)),
                  pl.BlockSpec((d, tn), lambda j: (0, j)),
                  pl.BlockSpec((1, tn), lambda j: (0, j))],
        out_specs=pl.BlockSpec((rows, tn), lambda j: (0, j)),
        compiler_params=pltpu.CompilerParams(dimension_semantics=("arbitrary",)),
        name="adaln",
    )(c, w_ada, b_ada)


def _proj_kernel(x_ref, scale_ref, shift_ref, g_ref,
                 wq_ref, wk_ref, wv_ref, wiq_ref, wik_ref, wiw_ref, wza_ref, wu_ref, wzs_ref,
                 q_ref, k_ref, kb_ref, v_ref, vb_ref, iq_ref, ik_ref, iklo_ref, ikhi_ref,
                 iw_ref, ga_ref, u_ref, gs_ref):
    x = x_ref[...]
    ms = jnp.mean(x * x, axis=-1, keepdims=True)
    y = x * lax.rsqrt(ms + RMS_EPS) * g_ref[...]
    h = (y * (1.0 + scale_ref[...]) + shift_ref[...]).astype(BF16)

    def mm(w_ref):
        return jnp.dot(h, w_ref[...], preferred_element_type=F32)

    q_ref[...] = (mm(wq_ref) * (HEAD_DIM ** -0.5 * math.log2(math.e))).astype(BF16)
    rows = x.shape[0]
    zk = mm(wk_ref)
    zv = mm(wv_ref)
    for hd in range(N_KV_HEADS):
        k_ref[pl.ds(hd, rows, stride=N_KV_HEADS), :] = zk[:, hd * HEAD_DIM:(hd + 1) * HEAD_DIM]
        v_ref[pl.ds(hd, rows, stride=N_KV_HEADS), :] = zv[:, hd * HEAD_DIM:(hd + 1) * HEAD_DIM]
    kb_ref[...] = zk.astype(BF16)
    vb_ref[...] = zv.astype(BF16)
    iq_ref[...] = (mm(wiq_ref) * (IDX_DIM ** -0.5)).astype(BF16)
    zik = mm(wik_ref)
    ik_ref[...] = zik[:, :IDX_DIM]
    lane = lax.broadcasted_iota(I32, zik.shape, 1)
    iklo_ref[...] = jnp.where(lane < IDX_DIM, zik, 0.0).astype(BF16)
    ikhi_ref[...] = jnp.where(lane >= IDX_DIM, zik, 0.0).astype(BF16)
    iw_ref[...] = mm(wiw_ref)[:, :N_IDX_HEADS] * (N_IDX_HEADS ** -0.5)
    za = mm(wza_ref)
    ga_ref[...] = (za * jax.nn.sigmoid(za)).astype(BF16)
    u_ref[...] = mm(wu_ref).astype(BF16)
    zs = mm(wzs_ref)
    gs_ref[...] = (zs * jax.nn.sigmoid(zs)).astype(BF16)


def _split_w_in(w_in):
    offs = [0]
    for s in IN_SIZES:
        offs.append(offs[-1] + s)
    cols = [w_in[:, offs[i]:offs[i + 1]].astype(BF16) for i in range(len(IN_SIZES))]
    wq, wk, wv, wiq, wik, wiw, wza, wu, wzs = cols
    wik2 = jnp.concatenate([wik, wik], axis=1)
    wiw_p = jnp.pad(wiw, ((0, 0), (0, LANES - N_IDX_HEADS)))
    return (wq, wk, wv, wiq, wik2, wiw_p, wza, wu, wzs)


def _proj(x2d, scale, shift, g_pre, weights, *, tm, rows_per_mod):
    n, d = x2d.shape
    if rows_per_mod is None:
        mod_spec = pl.BlockSpec((tm, d), lambda i: (i, 0))
    else:
        tiles_per_mod = rows_per_mod // tm
        mod_spec = pl.BlockSpec((None, 1, d), lambda i: (i // tiles_per_mod, 0, 0))

    def row_spec(width):
        return pl.BlockSpec((tm, width), lambda i: (i, 0))

    out_defs = [
        (1, D_ATTN, BF16), (N_KV_HEADS, HEAD_DIM, F32), (1, KV_DIM, BF16), (N_KV_HEADS, HEAD_DIM, F32),
        (1, KV_DIM, BF16), (1, N_IDX_HEADS * IDX_DIM, BF16), (1, IDX_DIM, F32), (1, LANES, BF16),
        (1, LANES, BF16), (1, N_IDX_HEADS, F32), (1, D_ATTN, BF16), (1, D_SSM, BF16), (1, D_SSM, BF16)]
    return pl.pallas_call(
        _proj_kernel,
        out_shape=[jax.ShapeDtypeStruct((r * n, w), dt) for r, w, dt in out_defs],
        grid=(n // tm,),
        in_specs=[row_spec(d), mod_spec, mod_spec, _const_spec((1, d))]
                 + [_const_spec(w.shape) for w in weights],
        out_specs=[pl.BlockSpec((r * tm, w), lambda i: (i, 0)) for r, w, _ in out_defs],
        compiler_params=pltpu.CompilerParams(dimension_semantics=("arbitrary",),
                                             vmem_limit_bytes=V7X_VMEM_LIMIT_BYTES),
        name="in_proj",
    )(x2d, scale, shift, g_pre, *weights)


def _dsa_kernel(q_ref, iq_ref, iw_ref, ga_ref, kb_ref, vb_ref, iklo_ref, ikhi_ref, o_ref,
                sc_ref, wb_ref, iqs_ref, qs_ref, m_ref, l_ref, acc_ref, kn_ref,
                *, qb, tk, topk, causal, n_keys, n_keys_pad):
    blk = pl.program_id(1)
    chunks_per_tile = tk // LANES
    row = lax.broadcasted_iota(I32, (qb, LANES), 0)
    if causal:
        n_vis = (blk + 1) * qb
        limit = (((blk * qb + row) // CHUNK) + 1) * CHUNK
        n_tiles = (n_vis + tk - 1) // tk
    else:
        limit = jnp.full((qb, LANES), n_keys, I32)
        n_tiles = (n_keys + tk - 1) // tk

    for j in range(IDX_PAIRS):
        iqs_ref[j * qb:(j + 1) * qb, :] = iq_ref[:, j * LANES:(j + 1) * LANES]
    for kv in range(N_KV_HEADS):
        for hh in range(HEADS_PER_KV):
            c0 = (kv * HEADS_PER_KV + hh) * HEAD_DIM
            qs_ref[kv, hh * qb:(hh + 1) * qb, :] = q_ref[:, c0:c0 + HEAD_DIM]
    iw = iw_ref[...]
    for h in range(N_IDX_HEADS):
        wb_ref[h] = jnp.broadcast_to(iw[:, h:h + 1], (qb, LANES))

    lane = lax.broadcasted_iota(I32, (qb, LANES), 1)

    def score_keys(first_chunk, n_matmuls, carry):
        m1, m2 = carry
        iqs = iqs_ref[...]
        for part in range(n_matmuls):
            chunk0 = first_chunk + part * (SCORE_KEYS // LANES)
            k0 = pl.multiple_of(chunk0 * LANES, SCORE_KEYS)
            s_lo = _dot_nt(iqs, iklo_ref[pl.ds(k0, SCORE_KEYS), :])
            s_hi = _dot_nt(iqs, ikhi_ref[pl.ds(k0, SCORE_KEYS), :])
            for c in range(SCORE_KEYS // LANES):
                cs = slice(c * LANES, (c + 1) * LANES)
                acc = None
                for j in range(IDX_PAIRS):
                    rs = slice(j * qb, (j + 1) * qb)
                    term = (jnp.maximum(s_lo[rs, cs], 0.0) * wb_ref[2 * j]
                            + jnp.maximum(s_hi[rs, cs], 0.0) * wb_ref[2 * j + 1])
                    acc = term if acc is None else acc + term
                ci = chunk0 + c
                visible = (ci * LANES + lane) < limit
                sc_ref[ci] = jnp.where(visible, acc, jnp.nan)
                v = jnp.where(visible, acc, -jnp.inf)
                m1, v = jnp.maximum(m1, v), jnp.minimum(m1, v)
                m2 = jnp.maximum(m2, v)
        return m1, m2

    def over_tile_pairs(step, carry):
        carry = lax.fori_loop(0, n_tiles // 2, lambda t, c: step(2 * t, 2, c), carry)
        return lax.cond(n_tiles % 2 == 1, lambda c: step(n_tiles - 1, 1, c), lambda c: c, carry)

    ninf = jnp.full((qb, LANES), -jnp.inf, F32)
    m1, m2 = over_tile_pairs(
        lambda t0, nt, c: score_keys(t0 * chunks_per_tile, nt * tk // SCORE_KEYS, c), (ninf, ninf))

    def lane_fold(x, op):
        return jnp.broadcast_to(op(x, axis=1, keepdims=True), (qb, LANES))

    def count_rows(pred):
        def count_tile(t, cnt):
            for c in range(chunks_per_tile):
                ci = t * chunks_per_tile + c
                cnt = cnt + jnp.where(pred(sc_ref[ci], ci * LANES + lane), 1.0, 0.0)
            return cnt

        return lane_fold(lax.fori_loop(0, n_tiles, count_tile, jnp.zeros((qb, LANES), F32)), jnp.sum)

    def key_bisection():
        def ordered_to_f32(key):
            return lax.bitcast_convert_type(jnp.where(key < 0, key ^ jnp.int32(0x7FFFFFFF), key), F32)

        def bisect(i, carry):
            lo, cnt_lo = carry
            cand_key = lo + lax.shift_left(jnp.int32(1), 31 - i)
            cand = ordered_to_f32(cand_key)
            total = count_rows(lambda s, _: s >= cand)
            take = total >= topk
            return jnp.where(take, cand_key, lo), jnp.where(take, total, cnt_lo)

        lo, cnt_lo = lax.fori_loop(0, 32, bisect, (jnp.full((qb, LANES), INT_MIN, I32),
                                                   jnp.zeros((qb, LANES), F32)))
        return ordered_to_f32(lo), cnt_lo

    few = limit < topk

    bracketed = limit >= 2 * LANES
    top = m2 if topk > LANES else m1
    lo_f = jnp.where(bracketed, lane_fold(m2, jnp.min), 0.0)
    hi_f = jnp.where(bracketed, lane_fold(top, jnp.max), 0.0)
    hi_f = hi_f + (jnp.abs(hi_f) * 2.0 ** -20 + 1e-30)

    cnt_hi = lane_fold(jnp.where(m1 >= hi_f, 1.0, 0.0), jnp.sum)

    def halve(_, carry):
        lo_f, hi_f, cnt_hi = carry
        mid = lo_f + (hi_f - lo_f) * 0.5
        total = count_rows(lambda s, _: s >= mid)
        take = total >= topk
        return jnp.where(take, mid, lo_f), jnp.where(take, hi_f, mid), jnp.where(take, cnt_hi, total)

    lo_f, hi_f, n_above = lax.fori_loop(0, COARSE_STEPS, halve, (lo_f, hi_f, cnt_hi))

    def bucket_rows(rs):
        lo_r, hi_r = lo_f[rs], hi_f[rs]

        def bucket_tile(t, carry):
            b1, b2, b3, pop = carry
            for c in range(chunks_per_tile):
                s = sc_ref[t * chunks_per_tile + c, rs, :]
                inb = (s >= lo_r) & jnp.logical_not(s >= hi_r)
                v = jnp.where(inb, s, -jnp.inf)
                pop = pop + jnp.where(inb, 1.0, 0.0)
                b1, v = jnp.maximum(b1, v), jnp.minimum(b1, v)
                b2, v = jnp.maximum(b2, v), jnp.minimum(b2, v)
                b3 = jnp.maximum(b3, v)
            return b1, b2, b3, pop

        return lax.fori_loop(0, n_tiles, bucket_tile, (ninf[rs], ninf[rs], ninf[rs], jnp.zeros_like(lo_r)))

    halves = [bucket_rows(slice(h * qb // 2, (h + 1) * qb // 2)) for h in range(2)]
    b1, b2, b3, pop = (jnp.concatenate([half[k] for half in halves], axis=0) for k in range(4))
    want = topk - n_above

    transposed = qb == LANES
    if transposed:
        cands = jnp.concatenate([b1.T, b2.T, b3.T], axis=0)
        want_q = want.T[:1]
        axis = 0
    else:
        cands = jnp.concatenate([b1, b2, b3], axis=1)
        want_q = want[:, :1]
        axis = 1

    def walk(_, carry):
        prev, covered, tau_q, cnt_q = carry
        cur = jnp.max(jnp.where(cands < prev, cands, -jnp.inf), axis=axis, keepdims=True)
        covered_new = covered + jnp.sum(jnp.where(cands == cur, 1.0, 0.0), axis=axis, keepdims=True)
        found = (covered < want_q) & (covered_new >= want_q) & (cur > -jnp.inf)
        return cur, covered_new, jnp.where(found, cur, tau_q), jnp.where(found, covered_new, cnt_q)

    n_walk = jnp.minimum(jnp.max(jnp.where(few, 0.0, want)), float(FINE_STEPS)).astype(I32)
    nan_q = jnp.full(want_q.shape, jnp.nan, F32)
    _, _, tau_q, cnt_q = lax.fori_loop(
        0, n_walk, walk,
        (jnp.full(want_q.shape, jnp.inf, F32), jnp.zeros(want_q.shape, F32), nan_q, nan_q))
    if transposed:
        tau_fast = jnp.broadcast_to(tau_q, (LANES, qb)).T
        cnt_fast = n_above + jnp.broadcast_to(cnt_q, (LANES, qb)).T
    else:
        tau_fast = jnp.broadcast_to(tau_q, (qb, LANES))
        cnt_fast = n_above + jnp.broadcast_to(cnt_q, (qb, LANES))

    bad = jnp.logical_not(few) & (jnp.logical_not(bracketed) | (pop > 3.0) | (want < 1.0)
                                  | jnp.logical_not(tau_fast == tau_fast))
    tau, cnt_tau = lax.cond(jnp.max(jnp.where(bad, 1.0, 0.0)) > 0.0, key_bisection,
                            lambda: (tau_fast, cnt_fast))
    tau = jnp.where(few, -jnp.inf, tau)
    excess = jnp.where(few, 0.0, cnt_tau - topk)

    @pl.when(jnp.max(excess) > 0)
    def _resolve_surplus():
        idx_bits = (n_keys_pad - 1).bit_length()

        def kept(s, kidx, vstar, jstar):
            return (s >= tau) & ((s > vstar) | ((s == vstar) & (kidx < jstar)))

        def drop_step(state):
            exc, vstar, jstar = state

            def min_tile(t, vm):
                for c in range(chunks_per_tile):
                    ci = t * chunks_per_tile + c
                    s = sc_ref[ci]
                    vm = jnp.minimum(vm, jnp.where(kept(s, ci * LANES + lane, vstar, jstar), s, jnp.inf))
                return vm

            vm = lax.fori_loop(0, n_tiles, min_tile, jnp.full((qb, LANES), jnp.inf, F32))
            vmin = jnp.broadcast_to(jnp.min(vm, axis=1, keepdims=True), (qb, LANES))

            def tied(s, kidx):
                return kept(s, kidx, vstar, jstar) & (s == vmin)

            cmin = count_rows(tied)
            active = exc > 0
            drop_all = active & (exc >= cmin)
            partial = active & (exc < cmin)
            keep = cmin - exc

            def idx_bisect(b, j0):
                cand_j = j0 + lax.shift_left(jnp.int32(1), idx_bits - 1 - b)
                below = count_rows(lambda s, kidx: tied(s, kidx) & (kidx < cand_j))
                return jnp.where(below < keep, cand_j, j0)

            j0 = lax.fori_loop(0, idx_bits, idx_bisect, jnp.zeros((qb, LANES), I32))
            vstar = jnp.where(active, vmin, vstar)
            jstar = jnp.where(drop_all, 0, jnp.where(partial, j0 + 1, jstar))
            exc = jnp.where(drop_all, exc - cmin, jnp.where(partial, 0, exc))
            return exc, vstar, jstar

        _, vstar, jstar = lax.while_loop(
            lambda state: jnp.max(state[0]) > 0, drop_step,
            (excess, tau, jnp.full((qb, LANES), 2 ** 30, I32)))

        def rewrite_tile(t, carry):
            for c in range(chunks_per_tile):
                ci = t * chunks_per_tile + c
                s = sc_ref[ci]
                dropped = (s >= tau) & jnp.logical_not(kept(s, ci * LANES + lane, vstar, jstar))
                sc_ref[ci] = jnp.where(dropped, jnp.nan, s)
            return carry

        lax.fori_loop(0, n_tiles, rewrite_tile, 0)

    def masked_logits(t0, nt, kv):
        ks = pl.multiple_of(t0 * tk, tk)
        bias = jnp.concatenate(
            [jnp.where(sc_ref[t0 * chunks_per_tile + c] >= tau, 0.0, MASKED_LOGIT)
             for c in range(nt * chunks_per_tile)], axis=1)
        s = _dot_nt(qs_ref[kv], kb_ref[pl.ds(ks, nt * tk), kv * HEAD_DIM:(kv + 1) * HEAD_DIM])
        return (s.reshape(HEADS_PER_KV, qb, nt * tk) + bias[None]).reshape(HEADS_PER_KV * qb, nt * tk)

    def attend_tiles(t0, nt, carry):
        ks = pl.multiple_of(t0 * tk, tk)
        for kv in range(N_KV_HEADS):
            s = masked_logits(t0, nt, kv)
            m = m_ref[kv]
            l = l_ref[kv]
            ps = []
            for c in range(nt * chunks_per_tile):
                e = jnp.exp2(s[:, c * LANES:(c + 1) * LANES] - m)
                l = l + e
                ps.append(e.astype(BF16))
            l_ref[kv] = l
            p = jnp.concatenate(ps, axis=1)
            acc_ref[kv] += jnp.dot(p, vb_ref[pl.ds(ks, nt * tk), kv * HEAD_DIM:(kv + 1) * HEAD_DIM],
                                   preferred_element_type=F32)
        return carry

    def attend():
        l_ref[...] = jnp.zeros(l_ref.shape, F32)
        acc_ref[...] = jnp.zeros(acc_ref.shape, F32)
        over_tile_pairs(attend_tiles, 0)

    @pl.when(blk == 0)
    def _key_norms():
        for kv in range(N_KV_HEADS):
            def norm_tile(t, mx):
                ks = pl.multiple_of(t * tk, tk)
                k = kb_ref[pl.ds(ks, tk), kv * HEAD_DIM:(kv + 1) * HEAD_DIM].astype(F32)
                return jnp.maximum(mx, jnp.sum(k * k, axis=1, keepdims=True))

            mx = lax.fori_loop(0, n_keys_pad // tk, norm_tile, jnp.zeros((tk, 1), F32))
            kn_ref[kv] = jnp.broadcast_to(jnp.sqrt(jnp.max(mx, axis=0, keepdims=True)), kn_ref.shape[1:])

    for kv in range(N_KV_HEADS):
        qf = qs_ref[kv].astype(F32)
        qn = jnp.sqrt(jnp.sum(qf * qf, axis=1, keepdims=True))
        m_ref[kv] = jnp.broadcast_to(qn, m_ref.shape[1:]) * kn_ref[kv, 0:1, :]
    attend()

    l_min = jnp.min(jnp.sum(l_ref[...], axis=2))

    @pl.when(jnp.logical_not(l_min >= MIN_ROW_SUM))
    def _exact_shift():
        m_ref[...] = jnp.full(m_ref.shape, MASKED_LOGIT, F32)

        def max_tile(t, carry):
            for kv in range(N_KV_HEADS):
                s = masked_logits(t, 1, kv)
                mx = m_ref[kv]
                for c in range(chunks_per_tile):
                    mx = jnp.maximum(mx, s[:, c * LANES:(c + 1) * LANES])
                m_ref[kv] = mx
            return carry

        lax.fori_loop(0, n_tiles, max_tile, 0)
        for kv in range(N_KV_HEADS):
            m_ref[kv] = jnp.broadcast_to(jnp.max(m_ref[kv], axis=1, keepdims=True), m_ref.shape[1:])
        attend()

    for kv in range(N_KV_HEADS):
        o = acc_ref[kv] / jnp.sum(l_ref[kv], axis=1, keepdims=True)
        for hh in range(HEADS_PER_KV):
            c0 = (kv * HEADS_PER_KV + hh) * HEAD_DIM
            gate = ga_ref[:, c0:c0 + HEAD_DIM].astype(F32)
            o_ref[:, c0:c0 + HEAD_DIM] = (o[hh * qb:(hh + 1) * qb] * gate).astype(BF16)


def _dsa(q, iq, iw, ga, kb, vb, iklo, ikhi, *, qb, causal, n_keys):
    b, s, _ = q.shape
    lk = kb.shape[1]
    tk = KEY_TILE
    assert lk % tk == 0 and s % qb == 0
    topk = min(TOPK_MAX, n_keys // 4)
    kern = functools.partial(_dsa_kernel, qb=qb, tk=tk, topk=topk, causal=causal, n_keys=n_keys,
                             n_keys_pad=lk)

    def q_spec(width):
        return pl.BlockSpec((None, qb, width), lambda bi, i: (bi, i, 0))

    def kv_spec(width):
        return pl.BlockSpec((None, lk, width), lambda bi, i: (bi, 0, 0))

    return pl.pallas_call(
        kern,
        out_shape=jax.ShapeDtypeStruct((b, s, D_ATTN), BF16),
        grid=(b, s // qb),
        in_specs=[q_spec(D_ATTN), q_spec(N_IDX_HEADS * IDX_DIM), q_spec(N_IDX_HEADS), q_spec(D_ATTN),
                  kv_spec(KV_DIM), kv_spec(KV_DIM), kv_spec(LANES), kv_spec(LANES)],
        out_specs=q_spec(D_ATTN),
        scratch_shapes=[
            pltpu.VMEM((lk // LANES, qb, LANES), F32),
            pltpu.VMEM((N_IDX_HEADS, qb, LANES), F32),
            pltpu.VMEM((IDX_PAIRS * qb, LANES), BF16),
            pltpu.VMEM((N_KV_HEADS, HEADS_PER_KV * qb, HEAD_DIM), BF16),
            pltpu.VMEM((N_KV_HEADS, HEADS_PER_KV * qb, LANES), F32),
            pltpu.VMEM((N_KV_HEADS, HEADS_PER_KV * qb, LANES), F32),
            pltpu.VMEM((N_KV_HEADS, HEADS_PER_KV * qb, HEAD_DIM), F32),
            pltpu.VMEM((N_KV_HEADS, 8, LANES), F32),
        ],
        compiler_params=pltpu.CompilerParams(dimension_semantics=("arbitrary", "arbitrary"),
                                             vmem_limit_bytes=V7X_VMEM_LIMIT_BYTES),
        name="dsa",
    )(q, iq, iw, ga, kb, vb, iklo, ikhi)


def _ssm_prep_kernel(lre_c_ref, lim_c_ref, lre_r_ref, lim_r_ref, ldt_ref, btr_ref, bti_ref,
                     ctr_ref, cti_ref, kflat_ref, mt_ref, em_ref,
                     a64r_ref, a64i_ref, a16r_ref, a16i_ref):
    hp = lax.Precision.HIGHEST
    dt = jnp.exp(ldt_ref[...])
    n_lag = CHUNK
    width = n_lag * SSM_GROUP

    ldr_c = lre_c_ref[...] * dt
    ldi_c = lim_c_ref[...] * dt
    lag = lax.broadcasted_iota(I32, (1, n_lag), 1).astype(F32)
    mag = jnp.exp(ldr_c * lag)
    pr_lag = mag * jnp.cos(ldi_c * lag)
    pi_lag = mag * jnp.sin(ldi_c * lag)
    col = lax.broadcasted_iota(I32, (n_lag, width), 1)
    pick_lag = (col // SSM_GROUP == lax.broadcasted_iota(I32, (n_lag, width), 0)).astype(F32)
    col = lax.broadcasted_iota(I32, (SSM_GROUP, width), 1)
    pick_ch = (col % SSM_GROUP == lax.broadcasted_iota(I32, (SSM_GROUP, width), 0)).astype(F32)
    pr = jnp.dot(pr_lag, pick_lag, precision=hp, preferred_element_type=F32)
    pi = jnp.dot(pi_lag, pick_lag, precision=hp, preferred_element_type=F32)
    ctr = jnp.dot(ctr_ref[...], pick_ch, precision=hp, preferred_element_type=F32)
    cti = jnp.dot(cti_ref[...], pick_ch, precision=hp, preferred_element_type=F32)
    qr = pr * ctr - pi * cti
    qi = pr * cti + pi * ctr

    lre_r = lre_r_ref[...]
    lim_r = lim_r_ref[...]
    ldr_r = lre_r * dt
    ldi_r = lim_r * dt
    lbr = jnp.exp(ldr_r) * jnp.cos(ldi_r)
    lbi = jnp.exp(ldr_r) * jnp.sin(ldi_r)
    den = lre_r * lre_r + lim_r * lim_r
    nr = lbr - 1.0
    fr = (nr * lre_r + lbi * lim_r) / den
    fi = (lbi * lre_r - nr * lim_r) / den
    btr = btr_ref[...]
    bti = bti_ref[...]
    bbr = fr * btr - fi * bti
    bbi = fr * bti + fi * btr

    kflat_ref[...] = (jnp.dot(bbr, qr, precision=hp, preferred_element_type=F32)
                      - jnp.dot(bbi, qi, precision=hp, preferred_element_type=F32))

    lbr_c = jnp.exp(ldr_c) * jnp.cos(ldi_c)
    lbi_c = jnp.exp(ldr_c) * jnp.sin(ldi_c)
    n_state = lbr_c.shape[0]
    em_ref[:n_state, :] = (lbr_c * qr - lbi_c * qi).astype(BF16)
    em_ref[n_state:, :] = (-(lbr_c * qi + lbi_c * qr)).astype(BF16)

    back = (n_lag - 1 - lax.broadcasted_iota(I32, (n_lag, 1), 0)).astype(F32)
    bmag = jnp.exp(ldr_r * back)
    bpr = bmag * jnp.cos(ldi_r * back)
    bpi = bmag * jnp.sin(ldi_r * back)
    for t in range(n_lag):
        wr = bpr[t:t + 1, :]
        wi = bpi[t:t + 1, :]
        rows = slice(t * SSM_GROUP, (t + 1) * SSM_GROUP)
        mt_ref[rows, :] = jnp.concatenate([wr * bbr - wi * bbi, wr * bbi + wi * bbr], axis=1).astype(BF16)

    for steps, ar_ref, ai_ref in ((float(CHUNK), a64r_ref, a64i_ref), (16.0, a16r_ref, a16i_ref)):
        amag = jnp.exp(ldr_r * steps)
        ar_ref[...] = amag * jnp.cos(ldi_r * steps)
        ai_ref[...] = amag * jnp.sin(ldi_r * steps)


def _ssm_prep(lambda_re, lambda_im, log_dt, b_re, b_im, c_re, c_im):
    g, p = lambda_re.shape
    width = CHUNK * SSM_GROUP
    lre_c = lambda_re.reshape(g, p, 1)
    lim_c = lambda_im.reshape(g, p, 1)
    lre_r = lambda_re.reshape(g, 1, p)
    lim_r = lambda_im.reshape(g, 1, p)
    ldt = log_dt.reshape(g, 1, 1)
    btr = jnp.swapaxes(b_re, 1, 2)
    bti = jnp.swapaxes(b_im, 1, 2)
    ctr = jnp.swapaxes(c_re, 1, 2)
    cti = jnp.swapaxes(c_im, 1, 2)

    def gspec(*shape):
        return pl.BlockSpec((None,) + shape, lambda i: (i,) + (0,) * len(shape))

    out_defs = [((SSM_GROUP, width), F32), ((width, 2 * p), BF16), ((2 * p, width), BF16),
                ((1, p), F32), ((1, p), F32), ((1, p), F32), ((1, p), F32)]
    return pl.pallas_call(
        _ssm_prep_kernel,
        out_shape=[jax.ShapeDtypeStruct((g,) + s, dt) for s, dt in out_defs],
        grid=(g,),
        in_specs=[gspec(p, 1), gspec(p, 1), gspec(1, p), gspec(1, p), gspec(1, 1),
                  gspec(SSM_GROUP, p), gspec(SSM_GROUP, p), gspec(p, SSM_GROUP), gspec(p, SSM_GROUP)],
        out_specs=[gspec(*s) for s, _ in out_defs],
        compiler_params=pltpu.CompilerParams(dimension_semantics=("arbitrary",)),
        name="ssm_prep",
    )(lre_c, lim_c, lre_r, lim_r, ldt, btr, bti, ctr, cti)


def _ssm_group(x, kflat_ref, mt_ref, em_ref, ar_ref, ai_ref,
               h0r_ref, h0i_ref, dvec_ref, hr_ref, hi_ref,
               toep_ref, gr_ref, gi_ref, hpr_ref, hpi_ref, *, t_len, n_seq, n_chunks):
    width = t_len * SSM_GROUP
    per_vreg = LANES // SSM_GROUP
    kflat = kflat_ref[:, :width]
    kext = jnp.concatenate([kflat, jnp.zeros((SSM_GROUP, LANES), F32)], axis=1)
    shifted = [kext] + [pltpu.roll(kext, r * SSM_GROUP, axis=1) for r in range(1, per_vreg)]
    for t in range(t_len):
        q, r = divmod(t, per_vreg)
        blk = shifted[r][:, :width - q * LANES]
        if q:
            blk = jnp.concatenate([jnp.zeros((SSM_GROUP, q * LANES), F32), blk], axis=1)
        toep_ref[t * SSM_GROUP:(t + 1) * SSM_GROUP, :] = blk.astype(BF16)

    g = jnp.dot(x, mt_ref[...], preferred_element_type=F32)
    gr_ref[...] = g[:, :SSM_STATE]
    gi_ref[...] = g[:, SSM_STATE:]

    ar = ar_ref[...]
    ai = ai_ref[...]

    def advance(h_r, h_i, a_r, a_i, g_r, g_i):
        return a_r * h_r - a_i * h_i + g_r, a_r * h_i + a_i * h_r + g_i

    block = _scan_block(n_chunks)
    if block == 1:
        def chunk_step(n, carry):
            h_r, h_i = carry
            rows = pl.ds(n * n_seq, n_seq)
            hpr_ref[rows, :] = h_r
            hpi_ref[rows, :] = h_i
            return advance(h_r, h_i, ar, ai, gr_ref[rows, :], gi_ref[rows, :])

        if n_chunks == 1:
            h_r, h_i = chunk_step(0, (h0r_ref[...], h0i_ref[...]))
        else:
            h_r, h_i = lax.fori_loop(0, n_chunks, chunk_step, (h0r_ref[...], h0i_ref[...]))
    else:
        n_blocks = n_chunks // block
        slab = n_blocks * n_seq
        zero = jnp.zeros((slab, SSM_STATE), F32)
        l_r, l_i = zero, zero
        local = []
        for j in range(block):
            local.append((l_r, l_i))
            rows = slice(j * slab, (j + 1) * slab)
            l_r, l_i = advance(l_r, l_i, ar, ai, gr_ref[rows, :], gi_ref[rows, :])
        powers = [(jnp.ones_like(ar), jnp.zeros_like(ar))]
        for j in range(block):
            p_r, p_i = powers[-1]
            powers.append((p_r * ar - p_i * ai, p_r * ai + p_i * ar))
        ab_r, ab_i = powers[block]
        h_r, h_i = h0r_ref[...], h0i_ref[...]
        entry = []
        for b in range(n_blocks):
            entry.append((h_r, h_i))
            rows = slice(b * n_seq, (b + 1) * n_seq)
            h_r, h_i = advance(h_r, h_i, ab_r, ab_i, l_r[rows], l_i[rows])
        e_r = jnp.concatenate([e[0] for e in entry], axis=0)
        e_i = jnp.concatenate([e[1] for e in entry], axis=0)
        for j in range(block):
            p_r, p_i = powers[j]
            rows = slice(j * slab, (j + 1) * slab)
            hpr_ref[rows, :] = local[j][0] + p_r * e_r - p_i * e_i
            hpi_ref[rows, :] = local[j][1] + p_r * e_i + p_i * e_r
    hr_ref[...] = h_r
    hi_ref[...] = h_i

    y = jnp.dot(x, toep_ref[...], preferred_element_type=F32)
    hp = jnp.concatenate([hpr_ref[...], hpi_ref[...]], axis=1).astype(BF16)
    y = y + jnp.dot(hp, em_ref[:, :width], preferred_element_type=F32)
    return y + x.astype(F32) * dvec_ref[...]


def _ssm_kernel(x_ref, kflat_ref, mt_ref, em_ref, ar_ref, ai_ref,
                h0r_ref, h0i_ref, dvec_ref, y_ref, hr_ref, hi_ref,
                toep_ref, gr_ref, gi_ref, hpr_ref, hpi_ref, **static):
    y = _ssm_group(x_ref[...], kflat_ref, mt_ref, em_ref, ar_ref, ai_ref,
                   h0r_ref, h0i_ref, dvec_ref, hr_ref, hi_ref,
                   toep_ref, gr_ref, gi_ref, hpr_ref, hpi_ref, **static)
    y_ref[...] = y.astype(BF16)


def _ssm(xg, prep, h0r, h0i, dvec, *, t_len, n_seq, n_chunks):
    kflat, mt, em, a64r, a64i, a16r, a16i = prep
    g, rows, width = xg.shape
    p = SSM_STATE
    full = CHUNK * SSM_GROUP
    if t_len == CHUNK:
        ar, ai = a64r, a64i
        mt_block = 0
    else:
        assert t_len == 16
        ar, ai = a16r, a16i
        mt_block = (full - width) // width
    kern = functools.partial(_ssm_kernel, t_len=t_len, n_seq=n_seq, n_chunks=n_chunks)

    def gspec(*shape):
        return pl.BlockSpec((None,) + shape, lambda i: (i,) + (0,) * len(shape))

    mt_spec = pl.BlockSpec((None, width, 2 * p), lambda i: (i, mt_block, 0))
    return pl.pallas_call(
        kern,
        out_shape=[jax.ShapeDtypeStruct((g, rows, width), BF16),
                   jax.ShapeDtypeStruct((g, n_seq, p), F32),
                   jax.ShapeDtypeStruct((g, n_seq, p), F32)],
        grid=(g,),
        in_specs=[gspec(rows, width), gspec(SSM_GROUP, full), mt_spec,
                  gspec(2 * p, full), gspec(1, p), gspec(1, p),
                  gspec(n_seq, p), gspec(n_seq, p), gspec(1, width)],
        out_specs=[gspec(rows, width), gspec(n_seq, p), gspec(n_seq, p)],
        scratch_shapes=[pltpu.VMEM((width, width), BF16),
                        pltpu.VMEM((rows, p), F32), pltpu.VMEM((rows, p), F32),
                        pltpu.VMEM((rows, p), F32), pltpu.VMEM((rows, p), F32)],
        compiler_params=pltpu.CompilerParams(dimension_semantics=("arbitrary",)),
        name="ssm",
    )(xg, kflat, mt, em, ar, ai, h0r, h0i, dvec)


def _out_kernel(a_ref, ys_ref, gs_ref, x_ref, gate_ref, wglu_ref, bglu_ref, woa_ref, wos_ref,
                gpost_ref, o_ref):
    y = ys_ref[...].astype(F32)
    y = 0.5 * y * (1.0 + jnp.tanh(math.sqrt(2.0 / math.pi) * (y + 0.044715 * (y * y * y))))
    z = jnp.dot(y.astype(BF16), wglu_ref[...], preferred_element_type=F32) + bglu_ref[...]
    y = y * jax.nn.sigmoid(z) * gs_ref[...].astype(F32)
    out = (jnp.dot(a_ref[...], woa_ref[...], preferred_element_type=F32)
           + jnp.dot(y.astype(BF16), wos_ref[...], preferred_element_type=F32))
    ms = jnp.mean(out * out, axis=-1, keepdims=True)
    normed = out * lax.rsqrt(ms + RMS_EPS) * gpost_ref[...]
    o_ref[...] = x_ref[...] + gate_ref[...] * normed


def _out_proj(a, ys, gs, x2d, gate, w_glu, b_glu, w_out, g_post, *, tm, rows_per_mod):
    n, d = x2d.shape
    if rows_per_mod is None:
        mod_spec = pl.BlockSpec((tm, d), lambda i: (i, 0))
    else:
        tiles_per_mod = rows_per_mod // tm
        mod_spec = pl.BlockSpec((None, 1, d), lambda i: (i // tiles_per_mod, 0, 0))

    def row_spec(width):
        return pl.BlockSpec((tm, width), lambda i: (i, 0))

    wglu = w_glu.astype(BF16)
    woa = w_out[:D_ATTN].astype(BF16)
    wos = w_out[D_ATTN:].astype(BF16)
    return pl.pallas_call(
        _out_kernel,
        out_shape=jax.ShapeDtypeStruct((n, d), F32),
        grid=(n // tm,),
        in_specs=[row_spec(D_ATTN), row_spec(D_SSM), row_spec(D_SSM), row_spec(d), mod_spec,
                  _const_spec(wglu.shape), _const_spec((1, D_SSM)), _const_spec(woa.shape),
                  _const_spec(wos.shape), _const_spec((1, d))],
        out_specs=row_spec(d),
        compiler_params=pltpu.CompilerParams(dimension_semantics=("arbitrary",),
                                             vmem_limit_bytes=V7X_VMEM_LIMIT_BYTES),
        name="out_proj",
    )(a, ys, gs, x2d, gate, wglu, b_glu.reshape(1, D_SSM), woa, wos, g_post.reshape(1, d))


def _scan_block(n_chunks):
    return SCAN_BLOCK if n_chunks % SCAN_BLOCK == 0 else 1


def _to_groups(u2d, n_seq, n_chunks, t_len):
    j = _scan_block(n_chunks)
    u6 = u2d.reshape(n_seq, n_chunks // j, j, t_len, N_SSM_GROUPS, SSM_GROUP)
    return u6.transpose(4, 2, 1, 0, 3, 5).reshape(N_SSM_GROUPS, n_seq * n_chunks, t_len * SSM_GROUP)


def _from_groups(yg, n_seq, n_chunks, t_len):
    j = _scan_block(n_chunks)
    y6 = yg.reshape(N_SSM_GROUPS, j, n_chunks // j, n_seq, t_len, SSM_GROUP)
    return y6.transpose(3, 2, 1, 4, 0, 5).reshape(n_seq * n_chunks * t_len, D_SSM)


def _pad_rows(a, rows):
    return jnp.pad(a, ((0, 0), (0, rows - a.shape[1]), (0, 0)))


def _layer(x, mod, past, g_pre, g_post, weights, prep, d_skip, w_glu, b_glu, w_out):
    bsz, s, d = x.shape
    n = bsz * s
    x2d = x.reshape(n, d)
    shift, scale, gate = mod[:, :d], mod[:, d:2 * d], mod[:, 2 * d:]
    if past is None:
        tm = 256
        rows_per_mod = s
        mods = [m.reshape(bsz, 1, d) for m in (scale, shift, gate)]
    else:
        tm = n
        rows_per_mod = None
        mods = [jnp.repeat(m, s, axis=0) for m in (scale, shift, gate)]
    scale_m, shift_m, gate_m = mods

    (q, k32, kb, v32, vb, iq, ik32, iklo, ikhi, iw, ga, u, gs) = _proj(
        x2d, scale_m, shift_m, g_pre.reshape(1, d), weights, tm=tm, rows_per_mod=rows_per_mod)

    def seq(a):
        return a.reshape(bsz, s, a.shape[-1])

    if past is None:
        attn = _dsa(seq(q), seq(iq), seq(iw), seq(ga), seq(kb), seq(vb), seq(iklo), seq(ikhi),
                    qb=Q_BLOCK, causal=True, n_keys=s)
        t_len = CHUNK if s % CHUNK == 0 else s
        h0r = jnp.zeros((N_SSM_GROUPS, bsz, SSM_STATE), F32)
        h0i = h0r
    else:
        ck, cv, cik, sre, sim = past
        past_len = ck.shape[1]
        n_keys = past_len + s
        lk = -(-n_keys // KEY_TILE) * KEY_TILE
        k_all = jnp.concatenate([ck.reshape(bsz, past_len, KV_DIM).astype(BF16), seq(kb)], axis=1)
        v_all = jnp.concatenate([cv.reshape(bsz, past_len, KV_DIM).astype(BF16), seq(vb)], axis=1)
        cik_b = cik.astype(BF16)
        zeros = jnp.zeros_like(cik_b)
        iklo_all = jnp.concatenate([jnp.concatenate([cik_b, zeros], axis=-1), seq(iklo)], axis=1)
        ikhi_all = jnp.concatenate([jnp.concatenate([zeros, cik_b], axis=-1), seq(ikhi)], axis=1)
        attn = _dsa(seq(q), seq(iq), seq(iw), seq(ga), _pad_rows(k_all, lk), _pad_rows(v_all, lk),
                    _pad_rows(iklo_all, lk), _pad_rows(ikhi_all, lk),
                    qb=s, causal=False, n_keys=n_keys)
        t_len = CHUNK if s % CHUNK == 0 else s
        h0r = jnp.swapaxes(sre.astype(F32), 0, 1)
        h0i = jnp.swapaxes(sim.astype(F32), 0, 1)

    n_chunks = s // t_len
    dvec = jnp.tile(d_skip.astype(F32).reshape(N_SSM_GROUPS, 1, SSM_GROUP), (1, 1, t_len))
    rows = bsz * n_chunks
    rows_pad = -(-rows // 16) * 16
    xg = _to_groups(u, bsz, n_chunks, t_len)
    xg = _pad_rows(xg, rows_pad) if rows_pad != rows else xg
    if rows_pad != rows:
        assert n_chunks == 1
        h0r = _pad_rows(h0r, rows_pad)
        h0i = _pad_rows(h0i, rows_pad)
    yg, hr, hi = _ssm(xg, prep, h0r, h0i, dvec, t_len=t_len,
                      n_seq=rows_pad // n_chunks, n_chunks=n_chunks)
    ys = _from_groups(yg[:, :rows], bsz, n_chunks, t_len)
    hr = jnp.swapaxes(hr[:, :bsz], 0, 1)
    hi = jnp.swapaxes(hi[:, :bsz], 0, 1)

    y = _out_proj(attn.reshape(n, D_ATTN), ys, gs, x2d, gate_m, w_glu, b_glu, w_out, g_post,
                  tm=tm, rows_per_mod=rows_per_mod)
    k_out = k32.reshape(bsz, s, N_KV_HEADS, HEAD_DIM)
    v_out = v32.reshape(bsz, s, N_KV_HEADS, HEAD_DIM)
    ik_out = ik32.reshape(bsz, s, IDX_DIM)
    return y.reshape(bsz, s, d), (k_out, v_out, ik_out, hr, hi)


def kernel(x_prompt, x_sample, c_prompt, c_sample, cache_k, cache_v, cache_idx_k, state_ssm_re,
           state_ssm_im, w_ada, b_ada, g_pre, g_post, w_in, lambda_re, lambda_im, log_dt, b_re, b_im,
           c_re, c_im, d_skip, w_glu, b_glu, w_out):
    depth = w_ada.shape[0]
    bp = c_prompt.shape[0]
    bs = c_sample.shape[0]
    yp, ys = x_prompt, x_sample
    outs_p = [[] for _ in range(5)]
    outs_s = [[] for _ in range(5)]
    c_all = jnp.concatenate([c_prompt, c_sample], axis=0)
    c_rows = -(-c_all.shape[0] // 8) * 8
    c_all = jnp.pad(c_all, ((0, c_rows - c_all.shape[0]), (0, 0)))
    for l in range(depth):
        mod = _adaln(c_all, w_ada[l], b_ada[l].reshape(1, -1))
        weights = _split_w_in(w_in[l])
        prep = _ssm_prep(lambda_re[l], lambda_im[l], log_dt[l], b_re[l], b_im[l], c_re[l], c_im[l])
        common = (g_pre[l], g_post[l], weights, prep, d_skip[l], w_glu[l], b_glu[l], w_out[l])
        yp, new_p = _layer(yp, mod[:bp], None, *common)
        past = (cache_k[l], cache_v[l], cache_idx_k[l], state_ssm_re[l], state_ssm_im[l])
        ys, new_s = _layer(ys, mod[bp:bp + bs], past, *common)
        for acc, val in zip(outs_p, new_p):
            acc.append(val)
        for acc, val in zip(outs_s, new_s):
            acc.append(val)
    return (yp, ys) + tuple(jnp.stack(a) for a in outs_p) + tuple(jnp.stack(a) for a in outs_s)
```

```python
import functools
import math

import jax
import jax.numpy as jnp
from jax import lax
from jax.experimental import pallas as pl
from jax.experimental.pallas import tpu as pltpu

F32 = jnp.float32
BF16 = jnp.bfloat16
I32 = jnp.int32

CHUNK = 64
Q_BLOCK = 128
D_ATTN = 1024
N_HEADS = 8
N_KV_HEADS = 2
HEAD_DIM = 128
KV_DIM = N_KV_HEADS * HEAD_DIM
HEADS_PER_KV = N_HEADS // N_KV_HEADS
N_IDX_HEADS = 16
IDX_DIM = 64
IDX_PAIRS = N_IDX_HEADS // 2
TOPK_MAX = 256
D_SSM = 1024
SSM_GROUP = 16
N_SSM_GROUPS = D_SSM // SSM_GROUP
SSM_STATE = 64
RMS_EPS = 1e-6
IN_SIZES = (D_ATTN, KV_DIM, KV_DIM, N_IDX_HEADS * IDX_DIM, IDX_DIM, N_IDX_HEADS, D_ATTN, D_SSM, D_SSM)

LANES = 128
V7X_VMEM_LIMIT_BYTES = 56 * 1024 * 1024

INT_MIN = -(2 ** 31)
MASKED_LOGIT = -1e30
MIN_ROW_SUM = 2.0 ** -100
KEY_TILE = 512
SCORE_KEYS = 256
SCAN_BLOCK = 8
SSM_GROUPS_PER_STEP = 8
COARSE_STEPS = 9
FINE_STEPS = 24


def _const_spec(shape):
    nd = len(shape)
    return pl.BlockSpec(shape, lambda *_: (0,) * nd, pipeline_mode=pl.Buffered(1))


def _dot_nt(a, b):
    return lax.dot_general(a, b, (((1,), (1,)), ((), ())), preferred_element_type=F32)


def _adaln_kernel(c_ref, w_ref, b_ref, o_ref):
    o_ref[...] = jnp.dot(c_ref[...], w_ref[...], preferred_element_type=F32) + b_ref[...]


def _adaln(c, w_ada, b_ada):
    rows, d = c.shape
    n_out = w_ada.shape[1]
    tn = 512
    return pl.pallas_call(
        _adaln_kernel,
        out_shape=jax.ShapeDtypeStruct((rows, n_out), F32),
        grid=(n_out // tn,),
        in_specs=[pl.BlockSpec((rows, d), lambda j: (0, 0)),
                  pl.BlockSpec((d, tn), lambda j: (0, j)),
                  pl.BlockSpec((1, tn), lambda j: (0, j))],
        out_specs=pl.BlockSpec((rows, tn), lambda j: (0, j)),
        compiler_params=pltpu.CompilerParams(dimension_semantics=("arbitrary",)),
        name="adaln",
    )(c, w_ada, b_ada)


def _proj_kernel(x_ref, scale_ref, shift_ref, g_ref,
                 wq_ref, wk_ref, wv_ref, wiq_ref, wik_ref, wiw_ref, wza_ref, wu_ref, wzs_ref,
                 q_ref, k_ref, kb_ref, v_ref, vb_ref, iq_ref, ik_ref, iklo_ref, ikhi_ref,
                 iw_ref, ga_ref, u_ref, gs_ref):
    x = x_ref[...]
    ms = jnp.mean(x * x, axis=-1, keepdims=True)
    y = x * lax.rsqrt(ms + RMS_EPS) * g_ref[...]
    h = (y * (1.0 + scale_ref[...]) + shift_ref[...]).astype(BF16)

    def mm(w_ref):
        return jnp.dot(h, w_ref[...], preferred_element_type=F32)

    q_ref[...] = (mm(wq_ref) * (HEAD_DIM ** -0.5 * math.log2(math.e))).astype(BF16)
    rows = x.shape[0]
    zk = mm(wk_ref)
    zv = mm(wv_ref)
    for hd in range(N_KV_HEADS):
        k_ref[pl.ds(hd, rows, stride=N_KV_HEADS), :] = zk[:, hd * HEAD_DIM:(hd + 1) * HEAD_DIM]
        v_ref[pl.ds(hd, rows, stride=N_KV_HEADS), :] = zv[:, hd * HEAD_DIM:(hd + 1) * HEAD_DIM]
    kb_ref[...] = zk.astype(BF16)
    vb_ref[...] = zv.astype(BF16)
    iq_ref[...] = (mm(wiq_ref) * (IDX_DIM ** -0.5)).astype(BF16)
    zik = mm(wik_ref)
    ik_ref[...] = zik[:, :IDX_DIM]
    lane = lax.broadcasted_iota(I32, zik.shape, 1)
    iklo_ref[...] = jnp.where(lane < IDX_DIM, zik, 0.0).astype(BF16)
    ikhi_ref[...] = jnp.where(lane >= IDX_DIM, zik, 0.0).astype(BF16)
    iw_ref[...] = mm(wiw_ref)[:, :N_IDX_HEADS] * (N_IDX_HEADS ** -0.5)
    za = mm(wza_ref)
    ga_ref[...] = (za * jax.nn.sigmoid(za)).astype(BF16)
    u_ref[...] = mm(wu_ref).astype(BF16)
    zs = mm(wzs_ref)
    gs_ref[...] = (zs * jax.nn.sigmoid(zs)).astype(BF16)


def _split_w_in(w_in):
    offs = [0]
    for s in IN_SIZES:
        offs.append(offs[-1] + s)
    cols = [w_in[:, offs[i]:offs[i + 1]].astype(BF16) for i in range(len(IN_SIZES))]
    wq, wk, wv, wiq, wik, wiw, wza, wu, wzs = cols
    wik2 = jnp.concatenate([wik, wik], axis=1)
    wiw_p = jnp.pad(wiw, ((0, 0), (0, LANES - N_IDX_HEADS)))
    return (wq, wk, wv, wiq, wik2, wiw_p, wza, wu, wzs)


def _proj(x2d, scale, shift, g_pre, weights, *, tm, rows_per_mod):
    n, d = x2d.shape
    if rows_per_mod is None:
        mod_spec = pl.BlockSpec((tm, d), lambda i: (i, 0))
    else:
        tiles_per_mod = rows_per_mod // tm
        mod_spec = pl.BlockSpec((None, 1, d), lambda i: (i // tiles_per_mod, 0, 0))

    def row_spec(width):
        return pl.BlockSpec((tm, width), lambda i: (i, 0))

    out_defs = [
        (1, D_ATTN, BF16), (N_KV_HEADS, HEAD_DIM, F32), (1, KV_DIM, BF16), (N_KV_HEADS, HEAD_DIM, F32),
        (1, KV_DIM, BF16), (1, N_IDX_HEADS * IDX_DIM, BF16), (1, IDX_DIM, F32), (1, LANES, BF16),
        (1, LANES, BF16), (1, N_IDX_HEADS, F32), (1, D_ATTN, BF16), (1, D_SSM, BF16), (1, D_SSM, BF16)]
    return pl.pallas_call(
        _proj_kernel,
        out_shape=[jax.ShapeDtypeStruct((r * n, w), dt) for r, w, dt in out_defs],
        grid=(n // tm,),
        in_specs=[row_spec(d), mod_spec, mod_spec, _const_spec((1, d))]
                 + [_const_spec(w.shape) for w in weights],
        out_specs=[pl.BlockSpec((r * tm, w), lambda i: (i, 0)) for r, w, _ in out_defs],
        compiler_params=pltpu.CompilerParams(dimension_semantics=("arbitrary",),
                                             vmem_limit_bytes=V7X_VMEM_LIMIT_BYTES),
        name="in_proj",
    )(x2d, scale, shift, g_pre, *weights)


def _dsa_kernel(q_ref, iq_ref, iw_ref, ga_ref, kb_ref, vb_ref, iklo_ref, ikhi_ref, o_ref,
                sc_ref, wb_ref, iqs_ref, qs_ref, m_ref, l_ref, acc_ref, kn_ref,
                *, qb, tk, topk, causal, n_keys, n_keys_pad):
    blk = pl.program_id(1)
    chunks_per_tile = tk // LANES
    row = lax.broadcasted_iota(I32, (qb, LANES), 0)
    if causal:
        n_vis = (blk + 1) * qb
        limit = (((blk * qb + row) // CHUNK) + 1) * CHUNK
        n_tiles = (n_vis + tk - 1) // tk
    else:
        limit = jnp.full((qb, LANES), n_keys, I32)
        n_tiles = (n_keys + tk - 1) // tk

    for j in range(IDX_PAIRS):
        iqs_ref[j * qb:(j + 1) * qb, :] = iq_ref[:, j * LANES:(j + 1) * LANES]
    for kv in range(N_KV_HEADS):
        for hh in range(HEADS_PER_KV):
            c0 = (kv * HEADS_PER_KV + hh) * HEAD_DIM
            qs_ref[kv, hh * qb:(hh + 1) * qb, :] = q_ref[:, c0:c0 + HEAD_DIM]
    iw = iw_ref[...]
    for h in range(N_IDX_HEADS):
        wb_ref[h] = jnp.broadcast_to(iw[:, h:h + 1], (qb, LANES))

    lane = lax.broadcasted_iota(I32, (qb, LANES), 1)

    def score_keys(first_chunk, n_matmuls, carry):
        m1, m2 = carry
        iqs = iqs_ref[...]
        for part in range(n_matmuls):
            chunk0 = first_chunk + part * (SCORE_KEYS // LANES)
            k0 = pl.multiple_of(chunk0 * LANES, SCORE_KEYS)
            s_lo = _dot_nt(iqs, iklo_ref[pl.ds(k0, SCORE_KEYS), :])
            s_hi = _dot_nt(iqs, ikhi_ref[pl.ds(k0, SCORE_KEYS), :])
            for c in range(SCORE_KEYS // LANES):
                cs = slice(c * LANES, (c + 1) * LANES)
                acc = None
                for j in range(IDX_PAIRS):
                    rs = slice(j * qb, (j + 1) * qb)
                    term = (jnp.maximum(s_lo[rs, cs], 0.0) * wb_ref[2 * j]
                            + jnp.maximum(s_hi[rs, cs], 0.0) * wb_ref[2 * j + 1])
                    acc = term if acc is None else acc + term
                ci = chunk0 + c
                visible = (ci * LANES + lane) < limit
                sc_ref[ci] = jnp.where(visible, acc, jnp.nan)
                v = jnp.where(visible, acc, -jnp.inf)
                m1, v = jnp.maximum(m1, v), jnp.minimum(m1, v)
                m2 = jnp.maximum(m2, v)
        return m1, m2

    def over_tile_pairs(step, carry):
        carry = lax.fori_loop(0, n_tiles // 2, lambda t, c: step(2 * t, 2, c), carry)
        return lax.cond(n_tiles % 2 == 1, lambda c: step(n_tiles - 1, 1, c), lambda c: c, carry)

    ninf = jnp.full((qb, LANES), -jnp.inf, F32)
    m1, m2 = over_tile_pairs(
        lambda t0, nt, c: score_keys(t0 * chunks_per_tile, nt * tk // SCORE_KEYS, c), (ninf, ninf))

    def lane_fold(x, op):
        return jnp.broadcast_to(op(x, axis=1, keepdims=True), (qb, LANES))

    def count_rows(pred):
        def count_tile(t, cnt):
            for c in range(chunks_per_tile):
                ci = t * chunks_per_tile + c
                cnt = cnt + jnp.where(pred(sc_ref[ci], ci * LANES + lane), 1.0, 0.0)
            return cnt

        return lane_fold(lax.fori_loop(0, n_tiles, count_tile, jnp.zeros((qb, LANES), F32)), jnp.sum)

    def key_bisection():
        def ordered_to_f32(key):
            return lax.bitcast_convert_type(jnp.where(key < 0, key ^ jnp.int32(0x7FFFFFFF), key), F32)

        def bisect(i, carry):
            lo, cnt_lo = carry
            cand_key = lo + lax.shift_left(jnp.int32(1), 31 - i)
            cand = ordered_to_f32(cand_key)
            total = count_rows(lambda s, _: s >= cand)
            take = total >= topk
            return jnp.where(take, cand_key, lo), jnp.where(take, total, cnt_lo)

        lo, cnt_lo = lax.fori_loop(0, 32, bisect, (jnp.full((qb, LANES), INT_MIN, I32),
                                                   jnp.zeros((qb, LANES), F32)))
        return ordered_to_f32(lo), cnt_lo

    few = limit < topk

    bracketed = limit >= 2 * LANES
    top = m2 if topk > LANES else m1
    lo_f = jnp.where(bracketed, lane_fold(m2, jnp.min), 0.0)
    hi_f = jnp.where(bracketed, lane_fold(top, jnp.max), 0.0)
    hi_f = hi_f + (jnp.abs(hi_f) * 2.0 ** -20 + 1e-30)

    cnt_hi = lane_fold(jnp.where(m1 >= hi_f, 1.0, 0.0), jnp.sum)

    def halve(_, carry):
        lo_f, hi_f, cnt_hi = carry
        mid = lo_f + (hi_f - lo_f) * 0.5
        total = count_rows(lambda s, _: s >= mid)
        take = total >= topk
        return jnp.where(take, mid, lo_f), jnp.where(take, hi_f, mid), jnp.where(take, cnt_hi, total)

    lo_f, hi_f, n_above = lax.fori_loop(0, COARSE_STEPS, halve, (lo_f, hi_f, cnt_hi))

    def bucket_rows(rs):
        lo_r, hi_r = lo_f[rs], hi_f[rs]

        def bucket_tile(t, carry):
            b1, b2, b3, pop = carry
            for c in range(chunks_per_tile):
                s = sc_ref[t * chunks_per_tile + c, rs, :]
                inb = (s >= lo_r) & jnp.logical_not(s >= hi_r)
                v = jnp.where(inb, s, -jnp.inf)
                pop = pop + jnp.where(inb, 1.0, 0.0)
                b1, v = jnp.maximum(b1, v), jnp.minimum(b1, v)
                b2, v = jnp.maximum(b2, v), jnp.minimum(b2, v)
                b3 = jnp.maximum(b3, v)
            return b1, b2, b3, pop

        return lax.fori_loop(0, n_tiles, bucket_tile, (ninf[rs], ninf[rs], ninf[rs], jnp.zeros_like(lo_r)))

    halves = [bucket_rows(slice(h * qb // 2, (h + 1) * qb // 2)) for h in range(2)]
    b1, b2, b3, pop = (jnp.concatenate([half[k] for half in halves], axis=0) for k in range(4))
    want = topk - n_above

    transposed = qb == LANES
    if transposed:
        cands = jnp.concatenate([b1.T, b2.T, b3.T], axis=0)
        want_q = want.T[:1]
        axis = 0
    else:
        cands = jnp.concatenate([b1, b2, b3], axis=1)
        want_q = want[:, :1]
        axis = 1

    def walk(_, carry):
        prev, covered, tau_q, cnt_q = carry
        cur = jnp.max(jnp.where(cands < prev, cands, -jnp.inf), axis=axis, keepdims=True)
        covered_new = covered + jnp.sum(jnp.where(cands == cur, 1.0, 0.0), axis=axis, keepdims=True)
        found = (covered < want_q) & (covered_new >= want_q) & (cur > -jnp.inf)
        return cur, covered_new, jnp.where(found, cur, tau_q), jnp.where(found, covered_new, cnt_q)

    n_walk = jnp.minimum(jnp.max(jnp.where(few, 0.0, want)), float(FINE_STEPS)).astype(I32)
    nan_q = jnp.full(want_q.shape, jnp.nan, F32)
    _, _, tau_q, cnt_q = lax.fori_loop(
        0, n_walk, walk,
        (jnp.full(want_q.shape, jnp.inf, F32), jnp.zeros(want_q.shape, F32), nan_q, nan_q))
    if transposed:
        tau_fast = jnp.broadcast_to(tau_q, (LANES, qb)).T
        cnt_fast = n_above + jnp.broadcast_to(cnt_q, (LANES, qb)).T
    else:
        tau_fast = jnp.broadcast_to(tau_q, (qb, LANES))
        cnt_fast = n_above + jnp.broadcast_to(cnt_q, (qb, LANES))

    bad = jnp.logical_not(few) & (jnp.logical_not(bracketed) | (pop > 3.0) | (want < 1.0)
                                  | jnp.logical_not(tau_fast == tau_fast))
    tau, cnt_tau = lax.cond(jnp.max(jnp.where(bad, 1.0, 0.0)) > 0.0, key_bisection,
                            lambda: (tau_fast, cnt_fast))
    tau = jnp.where(few, -jnp.inf, tau)
    excess = jnp.where(few, 0.0, cnt_tau - topk)

    @pl.when(jnp.max(excess) > 0)
    def _resolve_surplus():
        idx_bits = (n_keys_pad - 1).bit_length()

        def kept(s, kidx, vstar, jstar):
            return (s >= tau) & ((s > vstar) | ((s == vstar) & (kidx < jstar)))

        def drop_step(state):
            exc, vstar, jstar = state

            def min_tile(t, vm):
                for c in range(chunks_per_tile):
                    ci = t * chunks_per_tile + c
                    s = sc_ref[ci]
                    vm = jnp.minimum(vm, jnp.where(kept(s, ci * LANES + lane, vstar, jstar), s, jnp.inf))
                return vm

            vm = lax.fori_loop(0, n_tiles, min_tile, jnp.full((qb, LANES), jnp.inf, F32))
            vmin = jnp.broadcast_to(jnp.min(vm, axis=1, keepdims=True), (qb, LANES))

            def tied(s, kidx):
                return kept(s, kidx, vstar, jstar) & (s == vmin)

            cmin = count_rows(tied)
            active = exc > 0
            drop_all = active & (exc >= cmin)
            partial = active & (exc < cmin)
            keep = cmin - exc

            def idx_bisect(b, j0):
                cand_j = j0 + lax.shift_left(jnp.int32(1), idx_bits - 1 - b)
                below = count_rows(lambda s, kidx: tied(s, kidx) & (kidx < cand_j))
                return jnp.where(below < keep, cand_j, j0)

            j0 = lax.fori_loop(0, idx_bits, idx_bisect, jnp.zeros((qb, LANES), I32))
            vstar = jnp.where(active, vmin, vstar)
            jstar = jnp.where(drop_all, 0, jnp.where(partial, j0 + 1, jstar))
            exc = jnp.where(drop_all, exc - cmin, jnp.where(partial, 0, exc))
            return exc, vstar, jstar

        _, vstar, jstar = lax.while_loop(
            lambda state: jnp.max(state[0]) > 0, drop_step,
            (excess, tau, jnp.full((qb, LANES), 2 ** 30, I32)))

        def rewrite_tile(t, carry):
            for c in range(chunks_per_tile):
                ci = t * chunks_per_tile + c
                s = sc_ref[ci]
                dropped = (s >= tau) & jnp.logical_not(kept(s, ci * LANES + lane, vstar, jstar))
                sc_ref[ci] = jnp.where(dropped, jnp.nan, s)
            return carry

        lax.fori_loop(0, n_tiles, rewrite_tile, 0)

    def masked_logits(t0, nt, kv):
        ks = pl.multiple_of(t0 * tk, tk)
        bias = jnp.concatenate(
            [jnp.where(sc_ref[t0 * chunks_per_tile + c] >= tau, 0.0, MASKED_LOGIT)
             for c in range(nt * chunks_per_tile)], axis=1)
        s = _dot_nt(qs_ref[kv], kb_ref[pl.ds(ks, nt * tk), kv * HEAD_DIM:(kv + 1) * HEAD_DIM])
        return (s.reshape(HEADS_PER_KV, qb, nt * tk) + bias[None]).reshape(HEADS_PER_KV * qb, nt * tk)

    def attend_tiles(t0, nt, carry):
        ks = pl.multiple_of(t0 * tk, tk)
        for kv in range(N_KV_HEADS):
            s = masked_logits(t0, nt, kv)
            m = m_ref[kv]
            l = l_ref[kv]
            ps = []
            for c in range(nt * chunks_per_tile):
                e = jnp.exp2(s[:, c * LANES:(c + 1) * LANES] - m)
                l = l + e
                ps.append(e.astype(BF16))
            l_ref[kv] = l
            p = jnp.concatenate(ps, axis=1)
            acc_ref[kv] += jnp.dot(p, vb_ref[pl.ds(ks, nt * tk), kv * HEAD_DIM:(kv + 1) * HEAD_DIM],
                                   preferred_element_type=F32)
        return carry

    def attend():
        l_ref[...] = jnp.zeros(l_ref.shape, F32)
        acc_ref[...] = jnp.zeros(acc_ref.shape, F32)
        over_tile_pairs(attend_tiles, 0)
        for kv in range(N_KV_HEADS):
            l_ref[kv] = jnp.broadcast_to(jnp.sum(l_ref[kv], axis=1, keepdims=True), l_ref.shape[1:])

    @pl.when(blk == 0)
    def _key_norms():
        for kv in range(N_KV_HEADS):
            def norm_tile(t, mx):
                ks = pl.multiple_of(t * tk, tk)
                k = kb_ref[pl.ds(ks, tk), kv * HEAD_DIM:(kv + 1) * HEAD_DIM].astype(F32)
                return jnp.maximum(mx, jnp.sum(k * k, axis=1, keepdims=True))

            mx = lax.fori_loop(0, n_keys_pad // tk, norm_tile, jnp.zeros((tk, 1), F32))
            kn_ref[kv] = jnp.broadcast_to(jnp.sqrt(jnp.max(mx, axis=0, keepdims=True)), kn_ref.shape[1:])

    for kv in range(N_KV_HEADS):
        qf = qs_ref[kv].astype(F32)
        qn = jnp.sqrt(jnp.sum(qf * qf, axis=1, keepdims=True))
        m_ref[kv] = jnp.broadcast_to(qn, m_ref.shape[1:]) * kn_ref[kv, 0:1, :]
    attend()

    l_min = jnp.min(l_ref[...])

    @pl.when(jnp.logical_not(l_min >= MIN_ROW_SUM))
    def _exact_shift():
        m_ref[...] = jnp.full(m_ref.shape, MASKED_LOGIT, F32)

        def max_tile(t, carry):
            for kv in range(N_KV_HEADS):
                s = masked_logits(t, 1, kv)
                mx = m_ref[kv]
                for c in range(chunks_per_tile):
                    mx = jnp.maximum(mx, s[:, c * LANES:(c + 1) * LANES])
                m_ref[kv] = mx
            return carry

        lax.fori_loop(0, n_tiles, max_tile, 0)
        for kv in range(N_KV_HEADS):
            m_ref[kv] = jnp.broadcast_to(jnp.max(m_ref[kv], axis=1, keepdims=True), m_ref.shape[1:])
        attend()

    for kv in range(N_KV_HEADS):
        o = acc_ref[kv] / l_ref[kv]
        for hh in range(HEADS_PER_KV):
            c0 = (kv * HEADS_PER_KV + hh) * HEAD_DIM
            gate = ga_ref[:, c0:c0 + HEAD_DIM].astype(F32)
            o_ref[:, c0:c0 + HEAD_DIM] = (o[hh * qb:(hh + 1) * qb] * gate).astype(BF16)


def _dsa(q, iq, iw, ga, kb, vb, iklo, ikhi, *, qb, causal, n_keys):
    b, s, _ = q.shape
    lk = kb.shape[1]
    tk = KEY_TILE
    assert lk % tk == 0 and s % qb == 0
    topk = min(TOPK_MAX, n_keys // 4)
    kern = functools.partial(_dsa_kernel, qb=qb, tk=tk, topk=topk, causal=causal, n_keys=n_keys,
                             n_keys_pad=lk)

    def q_spec(width):
        return pl.BlockSpec((None, qb, width), lambda bi, i: (bi, i, 0))

    def kv_spec(width):
        return pl.BlockSpec((None, lk, width), lambda bi, i: (bi, 0, 0))

    return pl.pallas_call(
        kern,
        out_shape=jax.ShapeDtypeStruct((b, s, D_ATTN), BF16),
        grid=(b, s // qb),
        in_specs=[q_spec(D_ATTN), q_spec(N_IDX_HEADS * IDX_DIM), q_spec(N_IDX_HEADS), q_spec(D_ATTN),
                  kv_spec(KV_DIM), kv_spec(KV_DIM), kv_spec(LANES), kv_spec(LANES)],
        out_specs=q_spec(D_ATTN),
        scratch_shapes=[
            pltpu.VMEM((lk // LANES, qb, LANES), F32),
            pltpu.VMEM((N_IDX_HEADS, qb, LANES), F32),
            pltpu.VMEM((IDX_PAIRS * qb, LANES), BF16),
            pltpu.VMEM((N_KV_HEADS, HEADS_PER_KV * qb, HEAD_DIM), BF16),
            pltpu.VMEM((N_KV_HEADS, HEADS_PER_KV * qb, LANES), F32),
            pltpu.VMEM((N_KV_HEADS, HEADS_PER_KV * qb, LANES), F32),
            pltpu.VMEM((N_KV_HEADS, HEADS_PER_KV * qb, HEAD_DIM), F32),
            pltpu.VMEM((N_KV_HEADS, 8, LANES), F32),
        ],
        compiler_params=pltpu.CompilerParams(dimension_semantics=("arbitrary", "arbitrary"),
                                             vmem_limit_bytes=V7X_VMEM_LIMIT_BYTES),
        name="dsa",
    )(q, iq, iw, ga, kb, vb, iklo, ikhi)


def _ssm_prep_kernel(lre_c_ref, lim_c_ref, lre_r_ref, lim_r_ref, ldt_ref, btr_ref, bti_ref,
                     ctr_ref, cti_ref, kflat_ref, mt_ref, em_ref,
                     a64r_ref, a64i_ref, a16r_ref, a16i_ref):
    hp = lax.Precision.HIGHEST
    dt = jnp.exp(ldt_ref[...])
    n_lag = CHUNK
    width = n_lag * SSM_GROUP

    ldr_c = lre_c_ref[...] * dt
    ldi_c = lim_c_ref[...] * dt
    lag = lax.broadcasted_iota(I32, (1, n_lag), 1).astype(F32)
    mag = jnp.exp(ldr_c * lag)
    pr_lag = mag * jnp.cos(ldi_c * lag)
    pi_lag = mag * jnp.sin(ldi_c * lag)
    col = lax.broadcasted_iota(I32, (n_lag, width), 1)
    pick_lag = (col // SSM_GROUP == lax.broadcasted_iota(I32, (n_lag, width), 0)).astype(F32)
    col = lax.broadcasted_iota(I32, (SSM_GROUP, width), 1)
    pick_ch = (col % SSM_GROUP == lax.broadcasted_iota(I32, (SSM_GROUP, width), 0)).astype(F32)
    pr = jnp.dot(pr_lag, pick_lag, precision=hp, preferred_element_type=F32)
    pi = jnp.dot(pi_lag, pick_lag, precision=hp, preferred_element_type=F32)
    ctr = jnp.dot(ctr_ref[...], pick_ch, precision=hp, preferred_element_type=F32)
    cti = jnp.dot(cti_ref[...], pick_ch, precision=hp, preferred_element_type=F32)
    qr = pr * ctr - pi * cti
    qi = pr * cti + pi * ctr

    lre_r = lre_r_ref[...]
    lim_r = lim_r_ref[...]
    ldr_r = lre_r * dt
    ldi_r = lim_r * dt
    lbr = jnp.exp(ldr_r) * jnp.cos(ldi_r)
    lbi = jnp.exp(ldr_r) * jnp.sin(ldi_r)
    den = lre_r * lre_r + lim_r * lim_r
    nr = lbr - 1.0
    fr = (nr * lre_r + lbi * lim_r) / den
    fi = (lbi * lre_r - nr * lim_r) / den
    btr = btr_ref[...]
    bti = bti_ref[...]
    bbr = fr * btr - fi * bti
    bbi = fr * bti + fi * btr

    kflat_ref[...] = (jnp.dot(bbr, qr, precision=hp, preferred_element_type=F32)
                      - jnp.dot(bbi, qi, precision=hp, preferred_element_type=F32))

    lbr_c = jnp.exp(ldr_c) * jnp.cos(ldi_c)
    lbi_c = jnp.exp(ldr_c) * jnp.sin(ldi_c)
    n_state = lbr_c.shape[0]
    em_ref[:n_state, :] = (lbr_c * qr - lbi_c * qi).astype(BF16)
    em_ref[n_state:, :] = (-(lbr_c * qi + lbi_c * qr)).astype(BF16)

    back = (n_lag - 1 - lax.broadcasted_iota(I32, (n_lag, 1), 0)).astype(F32)
    bmag = jnp.exp(ldr_r * back)
    bpr = bmag * jnp.cos(ldi_r * back)
    bpi = bmag * jnp.sin(ldi_r * back)
    for t in range(n_lag):
        wr = bpr[t:t + 1, :]
        wi = bpi[t:t + 1, :]
        rows = slice(t * SSM_GROUP, (t + 1) * SSM_GROUP)
        mt_ref[rows, :] = jnp.concatenate([wr * bbr - wi * bbi, wr * bbi + wi * bbr], axis=1).astype(BF16)

    for steps, ar_ref, ai_ref in ((float(CHUNK), a64r_ref, a64i_ref), (16.0, a16r_ref, a16i_ref)):
        amag = jnp.exp(ldr_r * steps)
        ar_ref[...] = amag * jnp.cos(ldi_r * steps)
        ai_ref[...] = amag * jnp.sin(ldi_r * steps)


def _ssm_prep(lambda_re, lambda_im, log_dt, b_re, b_im, c_re, c_im):
    g, p = lambda_re.shape
    width = CHUNK * SSM_GROUP
    lre_c = lambda_re.reshape(g, p, 1)
    lim_c = lambda_im.reshape(g, p, 1)
    lre_r = lambda_re.reshape(g, 1, p)
    lim_r = lambda_im.reshape(g, 1, p)
    ldt = log_dt.reshape(g, 1, 1)
    btr = jnp.swapaxes(b_re, 1, 2)
    bti = jnp.swapaxes(b_im, 1, 2)
    ctr = jnp.swapaxes(c_re, 1, 2)
    cti = jnp.swapaxes(c_im, 1, 2)

    def gspec(*shape):
        return pl.BlockSpec((None,) + shape, lambda i: (i,) + (0,) * len(shape))

    out_defs = [((SSM_GROUP, width), F32), ((width, 2 * p), BF16), ((2 * p, width), BF16),
                ((1, p), F32), ((1, p), F32), ((1, p), F32), ((1, p), F32)]
    return pl.pallas_call(
        _ssm_prep_kernel,
        out_shape=[jax.ShapeDtypeStruct((g,) + s, dt) for s, dt in out_defs],
        grid=(g,),
        in_specs=[gspec(p, 1), gspec(p, 1), gspec(1, p), gspec(1, p), gspec(1, 1),
                  gspec(SSM_GROUP, p), gspec(SSM_GROUP, p), gspec(p, SSM_GROUP), gspec(p, SSM_GROUP)],
        out_specs=[gspec(*s) for s, _ in out_defs],
        compiler_params=pltpu.CompilerParams(dimension_semantics=("arbitrary",)),
        name="ssm_prep",
    )(lre_c, lim_c, lre_r, lim_r, ldt, btr, bti, ctr, cti)


def _ssm_group(x, kflat_ref, mt_ref, em_ref, ar_ref, ai_ref,
               h0r_ref, h0i_ref, dvec_ref, hr_ref, hi_ref,
               toep_ref, gr_ref, gi_ref, hpr_ref, hpi_ref, *, t_len, n_seq, n_chunks):
    width = t_len * SSM_GROUP
    per_vreg = LANES // SSM_GROUP
    kflat = kflat_ref[:, :width]
    kext = jnp.concatenate([kflat, jnp.zeros((SSM_GROUP, LANES), F32)], axis=1)
    shifted = [kext] + [pltpu.roll(kext, r * SSM_GROUP, axis=1) for r in range(1, per_vreg)]
    for t in range(t_len):
        q, r = divmod(t, per_vreg)
        blk = shifted[r][:, :width - q * LANES]
        if q:
            blk = jnp.concatenate([jnp.zeros((SSM_GROUP, q * LANES), F32), blk], axis=1)
        toep_ref[t * SSM_GROUP:(t + 1) * SSM_GROUP, :] = blk.astype(BF16)

    g = jnp.dot(x, mt_ref[...], preferred_element_type=F32)
    gr_ref[...] = g[:, :SSM_STATE]
    gi_ref[...] = g[:, SSM_STATE:]

    ar = ar_ref[...]
    ai = ai_ref[...]

    def advance(h_r, h_i, a_r, a_i, g_r, g_i):
        return a_r * h_r - a_i * h_i + g_r, a_r * h_i + a_i * h_r + g_i

    block = _scan_block(n_chunks)
    if block == 1:
        def chunk_step(n, carry):
            h_r, h_i = carry
            rows = pl.ds(n * n_seq, n_seq)
            hpr_ref[rows, :] = h_r
            hpi_ref[rows, :] = h_i
            return advance(h_r, h_i, ar, ai, gr_ref[rows, :], gi_ref[rows, :])

        if n_chunks == 1:
            h_r, h_i = chunk_step(0, (h0r_ref[...], h0i_ref[...]))
        else:
            h_r, h_i = lax.fori_loop(0, n_chunks, chunk_step, (h0r_ref[...], h0i_ref[...]))
    else:
        n_blocks = n_chunks // block
        slab = n_blocks * n_seq
        zero = jnp.zeros((slab, SSM_STATE), F32)
        l_r, l_i = zero, zero
        local = []
        for j in range(block):
            local.append((l_r, l_i))
            rows = slice(j * slab, (j + 1) * slab)
            l_r, l_i = advance(l_r, l_i, ar, ai, gr_ref[rows, :], gi_ref[rows, :])
        powers = [(jnp.ones_like(ar), jnp.zeros_like(ar))]
        for j in range(block):
            p_r, p_i = powers[-1]
            powers.append((p_r * ar - p_i * ai, p_r * ai + p_i * ar))
        ab_r, ab_i = powers[block]
        h_r, h_i = h0r_ref[...], h0i_ref[...]
        entry = []
        for b in range(n_blocks):
            entry.append((h_r, h_i))
            rows = slice(b * n_seq, (b + 1) * n_seq)
            h_r, h_i = advance(h_r, h_i, ab_r, ab_i, l_r[rows], l_i[rows])
        e_r = jnp.concatenate([e[0] for e in entry], axis=0)
        e_i = jnp.concatenate([e[1] for e in entry], axis=0)
        for j in range(block):
            p_r, p_i = powers[j]
            rows = slice(j * slab, (j + 1) * slab)
            hpr_ref[rows, :] = local[j][0] + p_r * e_r - p_i * e_i
            hpi_ref[rows, :] = local[j][1] + p_r * e_i + p_i * e_r
    hr_ref[...] = h_r
    hi_ref[...] = h_i

    y = jnp.dot(x, toep_ref[...], preferred_element_type=F32)
    hp = jnp.concatenate([hpr_ref[...], hpi_ref[...]], axis=1).astype(BF16)
    y = y + jnp.dot(hp, em_ref[:, :width], preferred_element_type=F32)
    return y + x.astype(F32) * dvec_ref[...]


def _ssm_kernel(x_ref, kflat_ref, mt_ref, em_ref, ar_ref, ai_ref,
                h0r_ref, h0i_ref, dvec_ref, y_ref, hr_ref, hi_ref,
                toep_ref, gr_ref, gi_ref, hpr_ref, hpi_ref, **static):
    for gi in range(x_ref.shape[0]):
        per_group = [r.at[gi] for r in (kflat_ref, mt_ref, em_ref, ar_ref, ai_ref, h0r_ref, h0i_ref,
                                        dvec_ref, hr_ref, hi_ref)]
        y = _ssm_group(x_ref[gi], *per_group, toep_ref, gr_ref, gi_ref, hpr_ref, hpi_ref, **static)
        y_ref[gi] = y.astype(BF16)


def _ssm(xg, prep, h0r, h0i, dvec, *, t_len, n_seq, n_chunks):
    kflat, mt, em, a64r, a64i, a16r, a16i = prep
    g, rows, width = xg.shape
    p = SSM_STATE
    full = CHUNK * SSM_GROUP
    if t_len == CHUNK:
        ar, ai = a64r, a64i
        mt_block = 0
    else:
        assert t_len == 16
        ar, ai = a16r, a16i
        mt_block = (full - width) // width
    kern = functools.partial(_ssm_kernel, t_len=t_len, n_seq=n_seq, n_chunks=n_chunks)
    per_step = SSM_GROUPS_PER_STEP if n_chunks == 1 else 1

    def gspec(*shape):
        return pl.BlockSpec((per_step,) + shape, lambda i: (i,) + (0,) * len(shape))

    mt_spec = pl.BlockSpec((per_step, width, 2 * p), lambda i: (i, mt_block, 0))
    return pl.pallas_call(
        kern,
        out_shape=[jax.ShapeDtypeStruct((g, rows, width), BF16),
                   jax.ShapeDtypeStruct((g, n_seq, p), F32),
                   jax.ShapeDtypeStruct((g, n_seq, p), F32)],
        grid=(g // per_step,),
        in_specs=[gspec(rows, width), gspec(SSM_GROUP, full), mt_spec,
                  gspec(2 * p, full), gspec(1, p), gspec(1, p),
                  gspec(n_seq, p), gspec(n_seq, p), gspec(1, width)],
        out_specs=[gspec(rows, width), gspec(n_seq, p), gspec(n_seq, p)],
        scratch_shapes=[pltpu.VMEM((width, width), BF16),
                        pltpu.VMEM((rows, p), F32), pltpu.VMEM((rows, p), F32),
                        pltpu.VMEM((rows, p), F32), pltpu.VMEM((rows, p), F32)],
        compiler_params=pltpu.CompilerParams(dimension_semantics=("arbitrary",)),
        name="ssm",
    )(xg, kflat, mt, em, ar, ai, h0r, h0i, dvec)


def _out_kernel(a_ref, ys_ref, gs_ref, x_ref, gate_ref, wglu_ref, bglu_ref, woa_ref, wos_ref,
                gpost_ref, o_ref):
    y = ys_ref[...].astype(F32)
    y = 0.5 * y * (1.0 + jnp.tanh(math.sqrt(2.0 / math.pi) * (y + 0.044715 * (y * y * y))))
    z = jnp.dot(y.astype(BF16), wglu_ref[...], preferred_element_type=F32) + bglu_ref[...]
    y = y * jax.nn.sigmoid(z) * gs_ref[...].astype(F32)
    out = (jnp.dot(a_ref[...], woa_ref[...], preferred_element_type=F32)
           + jnp.dot(y.astype(BF16), wos_ref[...], preferred_element_type=F32))
    ms = jnp.mean(out * out, axis=-1, keepdims=True)
    normed = out * lax.rsqrt(ms + RMS_EPS) * gpost_ref[...]
    o_ref[...] = x_ref[...] + gate_ref[...] * normed


def _out_proj(a, ys, gs, x2d, gate, w_glu, b_glu, w_out, g_post, *, tm, rows_per_mod):
    n, d = x2d.shape
    if rows_per_mod is None:
        mod_spec = pl.BlockSpec((tm, d), lambda i: (i, 0))
    else:
        tiles_per_mod = rows_per_mod // tm
        mod_spec = pl.BlockSpec((None, 1, d), lambda i: (i // tiles_per_mod, 0, 0))

    def row_spec(width):
        return pl.BlockSpec((tm, width), lambda i: (i, 0))

    wglu = w_glu.astype(BF16)
    woa = w_out[:D_ATTN].astype(BF16)
    wos = w_out[D_ATTN:].astype(BF16)
    return pl.pallas_call(
        _out_kernel,
        out_shape=jax.ShapeDtypeStruct((n, d), F32),
        grid=(n // tm,),
        in_specs=[row_spec(D_ATTN), row_spec(D_SSM), row_spec(D_SSM), row_spec(d), mod_spec,
                  _const_spec(wglu.shape), _const_spec((1, D_SSM)), _const_spec(woa.shape),
                  _const_spec(wos.shape), _const_spec((1, d))],
        out_specs=row_spec(d),
        compiler_params=pltpu.CompilerParams(dimension_semantics=("arbitrary",),
                                             vmem_limit_bytes=V7X_VMEM_LIMIT_BYTES),
        name="out_proj",
    )(a, ys, gs, x2d, gate, wglu, b_glu.reshape(1, D_SSM), woa, wos, g_post.reshape(1, d))


def _scan_block(n_chunks):
    return SCAN_BLOCK if n_chunks % SCAN_BLOCK == 0 else 1


def _to_groups(u2d, n_seq, n_chunks, t_len):
    j = _scan_block(n_chunks)
    u6 = u2d.reshape(n_seq, n_chunks // j, j, t_len, N_SSM_GROUPS, SSM_GROUP)
    return u6.transpose(4, 2, 1, 0, 3, 5).reshape(N_SSM_GROUPS, n_seq * n_chunks, t_len * SSM_GROUP)


def _from_groups(yg, n_seq, n_chunks, t_len):
    j = _scan_block(n_chunks)
    y6 = yg.reshape(N_SSM_GROUPS, j, n_chunks // j, n_seq, t_len, SSM_GROUP)
    return y6.transpose(3, 2, 1, 4, 0, 5).reshape(n_seq * n_chunks * t_len, D_SSM)


def _pad_rows(a, rows):
    return jnp.pad(a, ((0, 0), (0, rows - a.shape[1]), (0, 0)))


def _layer(x, mod, past, g_pre, g_post, weights, prep, d_skip, w_glu, b_glu, w_out):
    bsz, s, d = x.shape
    n = bsz * s
    x2d = x.reshape(n, d)
    shift, scale, gate = mod[:, :d], mod[:, d:2 * d], mod[:, 2 * d:]
    if past is None:
        tm = 256
        rows_per_mod = s
        mods = [m.reshape(bsz, 1, d) for m in (scale, shift, gate)]
    else:
        tm = n
        rows_per_mod = None
        mods = [jnp.repeat(m, s, axis=0) for m in (scale, shift, gate)]
    scale_m, shift_m, gate_m = mods

    (q, k32, kb, v32, vb, iq, ik32, iklo, ikhi, iw, ga, u, gs) = _proj(
        x2d, scale_m, shift_m, g_pre.reshape(1, d), weights, tm=tm, rows_per_mod=rows_per_mod)

    def seq(a):
        return a.reshape(bsz, s, a.shape[-1])

    if past is None:
        attn = _dsa(seq(q), seq(iq), seq(iw), seq(ga), seq(kb), seq(vb), seq(iklo), seq(ikhi),
                    qb=Q_BLOCK, causal=True, n_keys=s)
        t_len = CHUNK if s % CHUNK == 0 else s
        h0r = jnp.zeros((N_SSM_GROUPS, bsz, SSM_STATE), F32)
        h0i = h0r
    else:
        ck, cv, cik, sre, sim = past
        past_len = ck.shape[1]
        n_keys = past_len + s
        lk = -(-n_keys // KEY_TILE) * KEY_TILE
        k_all = jnp.concatenate([ck.reshape(bsz, past_len, KV_DIM).astype(BF16), seq(kb)], axis=1)
        v_all = jnp.concatenate([cv.reshape(bsz, past_len, KV_DIM).astype(BF16), seq(vb)], axis=1)
        cik_b = cik.astype(BF16)
        zeros = jnp.zeros_like(cik_b)
        iklo_all = jnp.concatenate([jnp.concatenate([cik_b, zeros], axis=-1), seq(iklo)], axis=1)
        ikhi_all = jnp.concatenate([jnp.concatenate([zeros, cik_b], axis=-1), seq(ikhi)], axis=1)
        attn = _dsa(seq(q), seq(iq), seq(iw), seq(ga), _pad_rows(k_all, lk), _pad_rows(v_all, lk),
                    _pad_rows(iklo_all, lk), _pad_rows(ikhi_all, lk),
                    qb=s, causal=False, n_keys=n_keys)
        t_len = CHUNK if s % CHUNK == 0 else s
        h0r = jnp.swapaxes(sre.astype(F32), 0, 1)
        h0i = jnp.swapaxes(sim.astype(F32), 0, 1)

    n_chunks = s // t_len
    dvec = jnp.tile(d_skip.astype(F32).reshape(N_SSM_GROUPS, 1, SSM_GROUP), (1, 1, t_len))
    rows = bsz * n_chunks
    rows_pad = -(-rows // 16) * 16
    xg = _to_groups(u, bsz, n_chunks, t_len)
    xg = _pad_rows(xg, rows_pad) if rows_pad != rows else xg
    if rows_pad != rows:
        assert n_chunks == 1
        h0r = _pad_rows(h0r, rows_pad)
        h0i = _pad_rows(h0i, rows_pad)
    yg, hr, hi = _ssm(xg, prep, h0r, h0i, dvec, t_len=t_len,
                      n_seq=rows_pad // n_chunks, n_chunks=n_chunks)
    ys = _from_groups(yg[:, :rows], bsz, n_chunks, t_len)
    hr = jnp.swapaxes(hr[:, :bsz], 0, 1)
    hi = jnp.swapaxes(hi[:, :bsz], 0, 1)

    y = _out_proj(attn.reshape(n, D_ATTN), ys, gs, x2d, gate_m, w_glu, b_glu, w_out, g_post,
                  tm=tm, rows_per_mod=rows_per_mod)
    k_out = k32.reshape(bsz, s, N_KV_HEADS, HEAD_DIM)
    v_out = v32.reshape(bsz, s, N_KV_HEADS, HEAD_DIM)
    ik_out = ik32.reshape(bsz, s, IDX_DIM)
    return y.reshape(bsz, s, d), (k_out, v_out, ik_out, hr, hi)


def kernel(x_prompt, x_sample, c_prompt, c_sample, cache_k, cache_v, cache_idx_k, state_ssm_re,
           state_ssm_im, w_ada, b_ada, g_pre, g_post, w_in, lambda_re, lambda_im, log_dt, b_re, b_im,
           c_re, c_im, d_skip, w_glu, b_glu, w_out):
    depth = w_ada.shape[0]
    bp = c_prompt.shape[0]
    bs = c_sample.shape[0]
    yp, ys = x_prompt, x_sample
    outs_p = [[] for _ in range(5)]
    outs_s = [[] for _ in range(5)]
    c_all = jnp.concatenate([c_prompt, c_sample], axis=0)
    c_rows = -(-c_all.shape[0] // 8) * 8
    c_all = jnp.pad(c_all, ((0, c_rows - c_all.shape[0]), (0, 0)))
    for l in range(depth):
        mod = _adaln(c_all, w_ada[l], b_ada[l].reshape(1, -1))
        weights = _split_w_in(w_in[l])
        prep = _ssm_prep(lambda_re[l], lambda_im[l], log_dt[l], b_re[l], b_im[l], c_re[l], c_im[l])
        common = (g_pre[l], g_post[l], weights, prep, d_skip[l], w_glu[l], b_glu[l], w_out[l])
        yp, new_p = _layer(yp, mod[:bp], None, *common)
        past = (cache_k[l], cache_v[l], cache_idx_k[l], state_ssm_re[l], state_ssm_im[l])
        ys, new_s = _layer(ys, mod[bp:bp + bs], past, *common)
        for acc, val in zip(outs_p, new_p):
            acc.append(val)
        for acc, val in zip(outs_s, new_s):
            acc.append(val)
    return (yp, ys) + tuple(jnp.stack(a) for a in outs_p) + tuple(jnp.stack(a) for a in outs_s)
```

```python
import functools
import math

import jax
import jax.numpy as jnp
from jax import lax
from jax.experimental import pallas as pl
from jax.experimental.pallas import tpu as pltpu

F32 = jnp.float32
BF16 = jnp.bfloat16
I32 = jnp.int32

CHUNK = 64
Q_BLOCK = 128
D_ATTN = 1024
N_HEADS = 8
N_KV_HEADS = 2
HEAD_DIM = 128
KV_DIM = N_KV_HEADS * HEAD_DIM
HEADS_PER_KV = N_HEADS // N_KV_HEADS
N_IDX_HEADS = 16
IDX_DIM = 64
IDX_PAIRS = N_IDX_HEADS // 2
TOPK_MAX = 256
D_SSM = 1024
SSM_GROUP = 16
N_SSM_GROUPS = D_SSM // SSM_GROUP
SSM_STATE = 64
RMS_EPS = 1e-6
IN_SIZES = (D_ATTN, KV_DIM, KV_DIM, N_IDX_HEADS * IDX_DIM, IDX_DIM, N_IDX_HEADS, D_ATTN, D_SSM, D_SSM)

LANES = 128
V7X_VMEM_LIMIT_BYTES = 56 * 1024 * 1024

INT_MIN = -(2 ** 31)
MASKED_LOGIT = -1e30
MIN_ROW_SUM = 2.0 ** -100
KEY_TILE = 512
SCORE_KEYS = 256
SCAN_BLOCK = 8
SSM_GROUPS_PER_STEP = 8
COARSE_STEPS = 9
FINE_STEPS = 24


def _const_spec(shape):
    nd = len(shape)
    return pl.BlockSpec(shape, lambda *_: (0,) * nd, pipeline_mode=pl.Buffered(1))


def _dot_nt(a, b):
    return lax.dot_general(a, b, (((1,), (1,)), ((), ())), preferred_element_type=F32)


def _adaln_kernel(c_ref, w_ref, b_ref, o_ref):
    o_ref[...] = jnp.dot(c_ref[...], w_ref[...], preferred_element_type=F32) + b_ref[...]


def _adaln(c, w_ada, b_ada):
    rows, d = c.shape
    n_out = w_ada.shape[1]
    tn = 512
    return pl.pallas_call(
        _adaln_kernel,
        out_shape=jax.ShapeDtypeStruct((rows, n_out), F32),
        grid=(n_out // tn,),
        in_specs=[pl.BlockSpec((rows, d), lambda j: (0, 0)),
                  pl.BlockSpec((d, tn), lambda j: (0, j)),
                  pl.BlockSpec((1, tn), lambda j: (0, j))],
        out_specs=pl.BlockSpec((rows, tn), lambda j: (0, j)),
        compiler_params=pltpu.CompilerParams(dimension_semantics=("arbitrary",)),
        name="adaln",
    )(c, w_ada, b_ada)


def _proj_kernel(x_ref, scale_ref, shift_ref, g_ref,
                 wq_ref, wk_ref, wv_ref, wiq_ref, wnar_ref, wza_ref, wu_ref, wzs_ref,
                 q_ref, k_ref, kb_ref, v_ref, vb_ref, iq_ref, ik_ref, iklo_ref, ikhi_ref,
                 iw_ref, ga_ref, u_ref, gs_ref):
    x = x_ref[...]
    ms = jnp.mean(x * x, axis=-1, keepdims=True)
    y = x * lax.rsqrt(ms + RMS_EPS) * g_ref[...]
    h = (y * (1.0 + scale_ref[...]) + shift_ref[...]).astype(BF16)

    def mm(w_ref):
        return jnp.dot(h, w_ref[...], preferred_element_type=F32)

    q_ref[...] = (mm(wq_ref) * (HEAD_DIM ** -0.5 * math.log2(math.e))).astype(BF16)
    rows = x.shape[0]
    zk = mm(wk_ref)
    zv = mm(wv_ref)
    for hd in range(N_KV_HEADS):
        k_ref[pl.ds(hd, rows, stride=N_KV_HEADS), :] = zk[:, hd * HEAD_DIM:(hd + 1) * HEAD_DIM]
        v_ref[pl.ds(hd, rows, stride=N_KV_HEADS), :] = zv[:, hd * HEAD_DIM:(hd + 1) * HEAD_DIM]
    kb_ref[...] = zk.astype(BF16)
    vb_ref[...] = zv.astype(BF16)
    iq_ref[...] = (mm(wiq_ref) * (IDX_DIM ** -0.5)).astype(BF16)
    znar = mm(wnar_ref)
    zik = znar[:, :LANES]
    ik_ref[...] = zik[:, :IDX_DIM]
    lane = lax.broadcasted_iota(I32, zik.shape, 1)
    iklo_ref[...] = jnp.where(lane < IDX_DIM, zik, 0.0).astype(BF16)
    ikhi_ref[...] = jnp.where(lane >= IDX_DIM, zik, 0.0).astype(BF16)
    iw_ref[...] = znar[:, LANES:LANES + N_IDX_HEADS] * (N_IDX_HEADS ** -0.5)
    za = mm(wza_ref)
    ga_ref[...] = (za * jax.nn.sigmoid(za)).astype(BF16)
    u_ref[...] = mm(wu_ref).astype(BF16)
    zs = mm(wzs_ref)
    gs_ref[...] = (zs * jax.nn.sigmoid(zs)).astype(BF16)


def _split_w_in(w_in):
    offs = [0]
    for s in IN_SIZES:
        offs.append(offs[-1] + s)
    cols = [w_in[:, offs[i]:offs[i + 1]].astype(BF16) for i in range(len(IN_SIZES))]
    wq, wk, wv, wiq, wik, wiw, wza, wu, wzs = cols
    wnar = jnp.concatenate([wik, wik, jnp.pad(wiw, ((0, 0), (0, LANES - N_IDX_HEADS)))], axis=1)
    return (wq, wk, wv, wiq, wnar, wza, wu, wzs)


def _proj(x2d, scale, shift, g_pre, weights, *, tm, rows_per_mod):
    n, d = x2d.shape
    if rows_per_mod is None:
        mod_spec = pl.BlockSpec((tm, d), lambda i: (i, 0))
    else:
        tiles_per_mod = rows_per_mod // tm
        mod_spec = pl.BlockSpec((None, 1, d), lambda i: (i // tiles_per_mod, 0, 0))

    def row_spec(width):
        return pl.BlockSpec((tm, width), lambda i: (i, 0))

    out_defs = [
        (1, D_ATTN, BF16), (N_KV_HEADS, HEAD_DIM, F32), (1, KV_DIM, BF16), (N_KV_HEADS, HEAD_DIM, F32),
        (1, KV_DIM, BF16), (1, N_IDX_HEADS * IDX_DIM, BF16), (1, IDX_DIM, F32), (1, LANES, BF16),
        (1, LANES, BF16), (1, N_IDX_HEADS, F32), (1, D_ATTN, BF16), (1, D_SSM, BF16), (1, D_SSM, BF16)]
    return pl.pallas_call(
        _proj_kernel,
        out_shape=[jax.ShapeDtypeStruct((r * n, w), dt) for r, w, dt in out_defs],
        grid=(n // tm,),
        in_specs=[row_spec(d), mod_spec, mod_spec, _const_spec((1, d))]
                 + [_const_spec(w.shape) for w in weights],
        out_specs=[pl.BlockSpec((r * tm, w), lambda i: (i, 0)) for r, w, _ in out_defs],
        compiler_params=pltpu.CompilerParams(dimension_semantics=("arbitrary",),
                                             vmem_limit_bytes=V7X_VMEM_LIMIT_BYTES),
        name="in_proj",
    )(x2d, scale, shift, g_pre, *weights)


def _dsa_kernel(q_ref, iq_ref, iw_ref, ga_ref, kb_ref, vb_ref, iklo_ref, ikhi_ref, o_ref,
                sc_ref, wb_ref, iqs_ref, qs_ref, m_ref, l_ref, acc_ref, kn_ref,
                *, qb, tk, topk, causal, n_keys, n_keys_pad):
    blk = pl.program_id(1)
    chunks_per_tile = tk // LANES
    row = lax.broadcasted_iota(I32, (qb, LANES), 0)
    if causal:
        n_vis = (blk + 1) * qb
        limit = (((blk * qb + row) // CHUNK) + 1) * CHUNK
        n_tiles = (n_vis + tk - 1) // tk
    else:
        limit = jnp.full((qb, LANES), n_keys, I32)
        n_tiles = (n_keys + tk - 1) // tk

    for j in range(IDX_PAIRS):
        iqs_ref[j * qb:(j + 1) * qb, :] = iq_ref[:, j * LANES:(j + 1) * LANES]
    for kv in range(N_KV_HEADS):
        for hh in range(HEADS_PER_KV):
            c0 = (kv * HEADS_PER_KV + hh) * HEAD_DIM
            qs_ref[kv, hh * qb:(hh + 1) * qb, :] = q_ref[:, c0:c0 + HEAD_DIM]
    iw = iw_ref[...]
    for h in range(N_IDX_HEADS):
        wb_ref[h] = jnp.broadcast_to(iw[:, h:h + 1], (qb, LANES))

    lane = lax.broadcasted_iota(I32, (qb, LANES), 1)

    def score_keys(first_chunk, n_matmuls, carry):
        m1, m2 = carry
        iqs = iqs_ref[...]
        for part in range(n_matmuls):
            chunk0 = first_chunk + part * (SCORE_KEYS // LANES)
            k0 = pl.multiple_of(chunk0 * LANES, SCORE_KEYS)
            s_lo = _dot_nt(iqs, iklo_ref[pl.ds(k0, SCORE_KEYS), :])
            s_hi = _dot_nt(iqs, ikhi_ref[pl.ds(k0, SCORE_KEYS), :])
            for c in range(SCORE_KEYS // LANES):
                cs = slice(c * LANES, (c + 1) * LANES)
                acc = None
                for j in range(IDX_PAIRS):
                    rs = slice(j * qb, (j + 1) * qb)
                    term = (jnp.maximum(s_lo[rs, cs], 0.0) * wb_ref[2 * j]
                            + jnp.maximum(s_hi[rs, cs], 0.0) * wb_ref[2 * j + 1])
                    acc = term if acc is None else acc + term
                ci = chunk0 + c
                visible = (ci * LANES + lane) < limit
                sc_ref[ci] = jnp.where(visible, acc, jnp.nan)
                v = jnp.where(visible, acc, -jnp.inf)
                m1, v = jnp.maximum(m1, v), jnp.minimum(m1, v)
                m2 = jnp.maximum(m2, v)
        return m1, m2

    def over_tile_pairs(step, carry):
        carry = lax.fori_loop(0, n_tiles // 2, lambda t, c: step(2 * t, 2, c), carry)
        return lax.cond(n_tiles % 2 == 1, lambda c: step(n_tiles - 1, 1, c), lambda c: c, carry)

    ninf = jnp.full((qb, LANES), -jnp.inf, F32)
    m1, m2 = over_tile_pairs(
        lambda t0, nt, c: score_keys(t0 * chunks_per_tile, nt * tk // SCORE_KEYS, c), (ninf, ninf))

    def lane_fold(x, op):
        return jnp.broadcast_to(op(x, axis=1, keepdims=True), (qb, LANES))

    def count_rows(pred):
        def count_tile(t, cnt):
            for c in range(chunks_per_tile):
                ci = t * chunks_per_tile + c
                cnt = cnt + jnp.where(pred(sc_ref[ci], ci * LANES + lane), 1.0, 0.0)
            return cnt

        return lane_fold(lax.fori_loop(0, n_tiles, count_tile, jnp.zeros((qb, LANES), F32)), jnp.sum)

    def key_bisection():
        def ordered_to_f32(key):
            return lax.bitcast_convert_type(jnp.where(key < 0, key ^ jnp.int32(0x7FFFFFFF), key), F32)

        def bisect(i, carry):
            lo, cnt_lo = carry
            cand_key = lo + lax.shift_left(jnp.int32(1), 31 - i)
            cand = ordered_to_f32(cand_key)
            total = count_rows(lambda s, _: s >= cand)
            take = total >= topk
            return jnp.where(take, cand_key, lo), jnp.where(take, total, cnt_lo)

        lo, cnt_lo = lax.fori_loop(0, 32, bisect, (jnp.full((qb, LANES), INT_MIN, I32),
                                                   jnp.zeros((qb, LANES), F32)))
        return ordered_to_f32(lo), cnt_lo

    few = limit < topk

    bracketed = limit >= 2 * LANES
    top = m2 if topk > LANES else m1
    lo_f = jnp.where(bracketed, lane_fold(m2, jnp.min), 0.0)
    hi_f = jnp.where(bracketed, lane_fold(top, jnp.max), 0.0)
    hi_f = hi_f + (jnp.abs(hi_f) * 2.0 ** -20 + 1e-30)

    cnt_hi = lane_fold(jnp.where(m1 >= hi_f, 1.0, 0.0), jnp.sum)

    def halve(_, carry):
        lo_f, hi_f, cnt_hi = carry
        mid = lo_f + (hi_f - lo_f) * 0.5
        total = count_rows(lambda s, _: s >= mid)
        take = total >= topk
        return jnp.where(take, mid, lo_f), jnp.where(take, hi_f, mid), jnp.where(take, cnt_hi, total)

    lo_f, hi_f, n_above = lax.fori_loop(0, COARSE_STEPS, halve, (lo_f, hi_f, cnt_hi))

    def bucket_rows(rs):
        lo_r, hi_r = lo_f[rs], hi_f[rs]

        def bucket_tile(t, carry):
            b1, b2, b3, pop = carry
            for c in range(chunks_per_tile):
                s = sc_ref[t * chunks_per_tile + c, rs, :]
                inb = (s >= lo_r) & jnp.logical_not(s >= hi_r)
                v = jnp.where(inb, s, -jnp.inf)
                pop = pop + jnp.where(inb, 1.0, 0.0)
                b1, v = jnp.maximum(b1, v), jnp.minimum(b1, v)
                b2, v = jnp.maximum(b2, v), jnp.minimum(b2, v)
                b3 = jnp.maximum(b3, v)
            return b1, b2, b3, pop

        return lax.fori_loop(0, n_tiles, bucket_tile, (ninf[rs], ninf[rs], ninf[rs], jnp.zeros_like(lo_r)))

    halves = [bucket_rows(slice(h * qb // 2, (h + 1) * qb // 2)) for h in range(2)]
    b1, b2, b3, pop = (jnp.concatenate([half[k] for half in halves], axis=0) for k in range(4))
    want = topk - n_above

    transposed = qb == LANES
    if transposed:
        cands = jnp.concatenate([b1.T, b2.T, b3.T], axis=0)
        want_q = want.T[:1]
        axis = 0
    else:
        cands = jnp.concatenate([b1, b2, b3], axis=1)
        want_q = want[:, :1]
        axis = 1

    def walk(_, carry):
        prev, covered, tau_q, cnt_q = carry
        cur = jnp.max(jnp.where(cands < prev, cands, -jnp.inf), axis=axis, keepdims=True)
        covered_new = covered + jnp.sum(jnp.where(cands == cur, 1.0, 0.0), axis=axis, keepdims=True)
        found = (covered < want_q) & (covered_new >= want_q) & (cur > -jnp.inf)
        return cur, covered_new, jnp.where(found, cur, tau_q), jnp.where(found, covered_new, cnt_q)

    n_walk = jnp.minimum(jnp.max(jnp.where(few, 0.0, want)), float(FINE_STEPS)).astype(I32)
    nan_q = jnp.full(want_q.shape, jnp.nan, F32)
    _, _, tau_q, cnt_q = lax.fori_loop(
        0, n_walk, walk,
        (jnp.full(want_q.shape, jnp.inf, F32), jnp.zeros(want_q.shape, F32), nan_q, nan_q))
    if transposed:
        tau_fast = jnp.broadcast_to(tau_q, (LANES, qb)).T
        cnt_fast = n_above + jnp.broadcast_to(cnt_q, (LANES, qb)).T
    else:
        tau_fast = jnp.broadcast_to(tau_q, (qb, LANES))
        cnt_fast = n_above + jnp.broadcast_to(cnt_q, (qb, LANES))

    bad = jnp.logical_not(few) & (jnp.logical_not(bracketed) | (pop > 3.0) | (want < 1.0)
                                  | jnp.logical_not(tau_fast == tau_fast))
    tau, cnt_tau = lax.cond(jnp.max(jnp.where(bad, 1.0, 0.0)) > 0.0, key_bisection,
                            lambda: (tau_fast, cnt_fast))
    tau = jnp.where(few, -jnp.inf, tau)
    excess = jnp.where(few, 0.0, cnt_tau - topk)

    @pl.when(jnp.max(excess) > 0)
    def _resolve_surplus():
        idx_bits = (n_keys_pad - 1).bit_length()

        def kept(s, kidx, vstar, jstar):
            return (s >= tau) & ((s > vstar) | ((s == vstar) & (kidx < jstar)))

        def drop_step(state):
            exc, vstar, jstar = state

            def min_tile(t, vm):
                for c in range(chunks_per_tile):
                    ci = t * chunks_per_tile + c
                    s = sc_ref[ci]
                    vm = jnp.minimum(vm, jnp.where(kept(s, ci * LANES + lane, vstar, jstar), s, jnp.inf))
                return vm

            vm = lax.fori_loop(0, n_tiles, min_tile, jnp.full((qb, LANES), jnp.inf, F32))
            vmin = jnp.broadcast_to(jnp.min(vm, axis=1, keepdims=True), (qb, LANES))

            def tied(s, kidx):
                return kept(s, kidx, vstar, jstar) & (s == vmin)

            cmin = count_rows(tied)
            active = exc > 0
            drop_all = active & (exc >= cmin)
            partial = active & (exc < cmin)
            keep = cmin - exc

            def idx_bisect(b, j0):
                cand_j = j0 + lax.shift_left(jnp.int32(1), idx_bits - 1 - b)
                below = count_rows(lambda s, kidx: tied(s, kidx) & (kidx < cand_j))
                return jnp.where(below < keep, cand_j, j0)

            j0 = lax.fori_loop(0, idx_bits, idx_bisect, jnp.zeros((qb, LANES), I32))
            vstar = jnp.where(active, vmin, vstar)
            jstar = jnp.where(drop_all, 0, jnp.where(partial, j0 + 1, jstar))
            exc = jnp.where(drop_all, exc - cmin, jnp.where(partial, 0, exc))
            return exc, vstar, jstar

        _, vstar, jstar = lax.while_loop(
            lambda state: jnp.max(state[0]) > 0, drop_step,
            (excess, tau, jnp.full((qb, LANES), 2 ** 30, I32)))

        def rewrite_tile(t, carry):
            for c in range(chunks_per_tile):
                ci = t * chunks_per_tile + c
                s = sc_ref[ci]
                dropped = (s >= tau) & jnp.logical_not(kept(s, ci * LANES + lane, vstar, jstar))
                sc_ref[ci] = jnp.where(dropped, jnp.nan, s)
            return carry

        lax.fori_loop(0, n_tiles, rewrite_tile, 0)

    def masked_logits(t0, nt, kv):
        ks = pl.multiple_of(t0 * tk, tk)
        bias = jnp.concatenate(
            [jnp.where(sc_ref[t0 * chunks_per_tile + c] >= tau, 0.0, MASKED_LOGIT)
             for c in range(nt * chunks_per_tile)], axis=1)
        s = _dot_nt(qs_ref[kv], kb_ref[pl.ds(ks, nt * tk), kv * HEAD_DIM:(kv + 1) * HEAD_DIM])
        return (s.reshape(HEADS_PER_KV, qb, nt * tk) + bias[None]).reshape(HEADS_PER_KV * qb, nt * tk)

    def attend_tiles(t0, nt, carry):
        ks = pl.multiple_of(t0 * tk, tk)
        for kv in range(N_KV_HEADS):
            s = masked_logits(t0, nt, kv)
            m = m_ref[kv]
            l = l_ref[kv]
            ps = []
            for c in range(nt * chunks_per_tile):
                e = jnp.exp2(s[:, c * LANES:(c + 1) * LANES] - m)
                l = l + e
                ps.append(e.astype(BF16))
            l_ref[kv] = l
            p = jnp.concatenate(ps, axis=1)
            acc_ref[kv] += jnp.dot(p, vb_ref[pl.ds(ks, nt * tk), kv * HEAD_DIM:(kv + 1) * HEAD_DIM],
                                   preferred_element_type=F32)
        return carry

    def attend():
        l_ref[...] = jnp.zeros(l_ref.shape, F32)
        acc_ref[...] = jnp.zeros(acc_ref.shape, F32)
        over_tile_pairs(attend_tiles, 0)
        for kv in range(N_KV_HEADS):
            l_ref[kv] = jnp.broadcast_to(jnp.sum(l_ref[kv], axis=1, keepdims=True), l_ref.shape[1:])

    @pl.when(blk == 0)
    def _key_norms():
        for kv in range(N_KV_HEADS):
            def norm_tile(t, mx):
                ks = pl.multiple_of(t * tk, tk)
                k = kb_ref[pl.ds(ks, tk), kv * HEAD_DIM:(kv + 1) * HEAD_DIM].astype(F32)
                return jnp.maximum(mx, jnp.sum(k * k, axis=1, keepdims=True))

            mx = lax.fori_loop(0, n_keys_pad // tk, norm_tile, jnp.zeros((tk, 1), F32))
            kn_ref[kv] = jnp.broadcast_to(jnp.sqrt(jnp.max(mx, axis=0, keepdims=True)), kn_ref.shape[1:])

    for kv in range(N_KV_HEADS):
        qf = qs_ref[kv].astype(F32)
        qn = jnp.sqrt(jnp.sum(qf * qf, axis=1, keepdims=True))
        m_ref[kv] = jnp.broadcast_to(qn, m_ref.shape[1:]) * kn_ref[kv, 0:1, :]
    attend()

    l_min = jnp.min(l_ref[...])

    @pl.when(jnp.logical_not(l_min >= MIN_ROW_SUM))
    def _exact_shift():
        m_ref[...] = jnp.full(m_ref.shape, MASKED_LOGIT, F32)

        def max_tile(t, carry):
            for kv in range(N_KV_HEADS):
                s = masked_logits(t, 1, kv)
                mx = m_ref[kv]
                for c in range(chunks_per_tile):
                    mx = jnp.maximum(mx, s[:, c * LANES:(c + 1) * LANES])
                m_ref[kv] = mx
            return carry

        lax.fori_loop(0, n_tiles, max_tile, 0)
        for kv in range(N_KV_HEADS):
            m_ref[kv] = jnp.broadcast_to(jnp.max(m_ref[kv], axis=1, keepdims=True), m_ref.shape[1:])
        attend()

    for kv in range(N_KV_HEADS):
        o = acc_ref[kv] / l_ref[kv]
        for hh in range(HEADS_PER_KV):
            c0 = (kv * HEADS_PER_KV + hh) * HEAD_DIM
            gate = ga_ref[:, c0:c0 + HEAD_DIM].astype(F32)
            o_ref[:, c0:c0 + HEAD_DIM] = (o[hh * qb:(hh + 1) * qb] * gate).astype(BF16)


def _dsa(q, iq, iw, ga, kb, vb, iklo, ikhi, *, qb, causal, n_keys):
    b, s, _ = q.shape
    lk = kb.shape[1]
    tk = KEY_TILE
    assert lk % tk == 0 and s % qb == 0
    topk = min(TOPK_MAX, n_keys // 4)
    kern = functools.partial(_dsa_kernel, qb=qb, tk=tk, topk=topk, causal=causal, n_keys=n_keys,
                             n_keys_pad=lk)

    def q_spec(width):
        return pl.BlockSpec((None, qb, width), lambda bi, i: (bi, i, 0))

    def kv_spec(width):
        return pl.BlockSpec((None, lk, width), lambda bi, i: (bi, 0, 0))

    return pl.pallas_call(
        kern,
        out_shape=jax.ShapeDtypeStruct((b, s, D_ATTN), BF16),
        grid=(b, s // qb),
        in_specs=[q_spec(D_ATTN), q_spec(N_IDX_HEADS * IDX_DIM), q_spec(N_IDX_HEADS), q_spec(D_ATTN),
                  kv_spec(KV_DIM), kv_spec(KV_DIM), kv_spec(LANES), kv_spec(LANES)],
        out_specs=q_spec(D_ATTN),
        scratch_shapes=[
            pltpu.VMEM((lk // LANES, qb, LANES), F32),
            pltpu.VMEM((N_IDX_HEADS, qb, LANES), F32),
            pltpu.VMEM((IDX_PAIRS * qb, LANES), BF16),
            pltpu.VMEM((N_KV_HEADS, HEADS_PER_KV * qb, HEAD_DIM), BF16),
            pltpu.VMEM((N_KV_HEADS, HEADS_PER_KV * qb, LANES), F32),
            pltpu.VMEM((N_KV_HEADS, HEADS_PER_KV * qb, LANES), F32),
            pltpu.VMEM((N_KV_HEADS, HEADS_PER_KV * qb, HEAD_DIM), F32),
            pltpu.VMEM((N_KV_HEADS, 8, LANES), F32),
        ],
        compiler_params=pltpu.CompilerParams(dimension_semantics=("arbitrary", "arbitrary"),
                                             vmem_limit_bytes=V7X_VMEM_LIMIT_BYTES),
        name="dsa",
    )(q, iq, iw, ga, kb, vb, iklo, ikhi)


def _ssm_prep_kernel(lre_c_ref, lim_c_ref, lre_r_ref, lim_r_ref, ldt_ref, btr_ref, bti_ref,
                     ctr_ref, cti_ref, kflat_ref, mt_ref, em_ref,
                     a64r_ref, a64i_ref, a16r_ref, a16i_ref):
    hp = lax.Precision.HIGHEST
    dt = jnp.exp(ldt_ref[...])
    n_lag = CHUNK
    width = n_lag * SSM_GROUP

    ldr_c = lre_c_ref[...] * dt
    ldi_c = lim_c_ref[...] * dt
    lag = lax.broadcasted_iota(I32, (1, n_lag), 1).astype(F32)
    mag = jnp.exp(ldr_c * lag)
    pr_lag = mag * jnp.cos(ldi_c * lag)
    pi_lag = mag * jnp.sin(ldi_c * lag)
    col = lax.broadcasted_iota(I32, (n_lag, width), 1)
    pick_lag = (col // SSM_GROUP == lax.broadcasted_iota(I32, (n_lag, width), 0)).astype(F32)
    col = lax.broadcasted_iota(I32, (SSM_GROUP, width), 1)
    pick_ch = (col % SSM_GROUP == lax.broadcasted_iota(I32, (SSM_GROUP, width), 0)).astype(F32)
    pr = jnp.dot(pr_lag, pick_lag, precision=hp, preferred_element_type=F32)
    pi = jnp.dot(pi_lag, pick_lag, precision=hp, preferred_element_type=F32)
    ctr = jnp.dot(ctr_ref[...], pick_ch, precision=hp, preferred_element_type=F32)
    cti = jnp.dot(cti_ref[...], pick_ch, precision=hp, preferred_element_type=F32)
    qr = pr * ctr - pi * cti
    qi = pr * cti + pi * ctr

    lre_r = lre_r_ref[...]
    lim_r = lim_r_ref[...]
    ldr_r = lre_r * dt
    ldi_r = lim_r * dt
    lbr = jnp.exp(ldr_r) * jnp.cos(ldi_r)
    lbi = jnp.exp(ldr_r) * jnp.sin(ldi_r)
    den = lre_r * lre_r + lim_r * lim_r
    nr = lbr - 1.0
    fr = (nr * lre_r + lbi * lim_r) / den
    fi = (lbi * lre_r - nr * lim_r) / den
    btr = btr_ref[...]
    bti = bti_ref[...]
    bbr = fr * btr - fi * bti
    bbi = fr * bti + fi * btr

    kflat_ref[...] = (jnp.dot(bbr, qr, precision=hp, preferred_element_type=F32)
                      - jnp.dot(bbi, qi, precision=hp, preferred_element_type=F32))

    lbr_c = jnp.exp(ldr_c) * jnp.cos(ldi_c)
    lbi_c = jnp.exp(ldr_c) * jnp.sin(ldi_c)
    n_state = lbr_c.shape[0]
    em_ref[:n_state, :] = (lbr_c * qr - lbi_c * qi).astype(BF16)
    em_ref[n_state:, :] = (-(lbr_c * qi + lbi_c * qr)).astype(BF16)

    back = (n_lag - 1 - lax.broadcasted_iota(I32, (n_lag, 1), 0)).astype(F32)
    bmag = jnp.exp(ldr_r * back)
    bpr = bmag * jnp.cos(ldi_r * back)
    bpi = bmag * jnp.sin(ldi_r * back)
    for t in range(n_lag):
        wr = bpr[t:t + 1, :]
        wi = bpi[t:t + 1, :]
        rows = slice(t * SSM_GROUP, (t + 1) * SSM_GROUP)
        mt_ref[rows, :] = jnp.concatenate([wr * bbr - wi * bbi, wr * bbi + wi * bbr], axis=1).astype(BF16)

    for steps, ar_ref, ai_ref in ((float(CHUNK), a64r_ref, a64i_ref), (16.0, a16r_ref, a16i_ref)):
        amag = jnp.exp(ldr_r * steps)
        ar_ref[...] = amag * jnp.cos(ldi_r * steps)
        ai_ref[...] = amag * jnp.sin(ldi_r * steps)


def _ssm_prep(lambda_re, lambda_im, log_dt, b_re, b_im, c_re, c_im):
    g, p = lambda_re.shape
    width = CHUNK * SSM_GROUP
    lre_c = lambda_re.reshape(g, p, 1)
    lim_c = lambda_im.reshape(g, p, 1)
    lre_r = lambda_re.reshape(g, 1, p)
    lim_r = lambda_im.reshape(g, 1, p)
    ldt = log_dt.reshape(g, 1, 1)
    btr = jnp.swapaxes(b_re, 1, 2)
    bti = jnp.swapaxes(b_im, 1, 2)
    ctr = jnp.swapaxes(c_re, 1, 2)
    cti = jnp.swapaxes(c_im, 1, 2)

    def gspec(*shape):
        return pl.BlockSpec((None,) + shape, lambda i: (i,) + (0,) * len(shape))

    out_defs = [((SSM_GROUP, width), F32), ((width, 2 * p), BF16), ((2 * p, width), BF16),
                ((1, p), F32), ((1, p), F32), ((1, p), F32), ((1, p), F32)]
    return pl.pallas_call(
        _ssm_prep_kernel,
        out_shape=[jax.ShapeDtypeStruct((g,) + s, dt) for s, dt in out_defs],
        grid=(g,),
        in_specs=[gspec(p, 1), gspec(p, 1), gspec(1, p), gspec(1, p), gspec(1, 1),
                  gspec(SSM_GROUP, p), gspec(SSM_GROUP, p), gspec(p, SSM_GROUP), gspec(p, SSM_GROUP)],
        out_specs=[gspec(*s) for s, _ in out_defs],
        compiler_params=pltpu.CompilerParams(dimension_semantics=("arbitrary",)),
        name="ssm_prep",
    )(lre_c, lim_c, lre_r, lim_r, ldt, btr, bti, ctr, cti)


def _ssm_group(x, kflat_ref, mt_ref, em_ref, ar_ref, ai_ref,
               h0r_ref, h0i_ref, dvec_ref, hr_ref, hi_ref,
               toep_ref, gr_ref, gi_ref, hpr_ref, hpi_ref, *, t_len, n_seq, n_chunks):
    width = t_len * SSM_GROUP
    per_vreg = LANES // SSM_GROUP
    kflat = kflat_ref[:, :width]
    kext = jnp.concatenate([kflat, jnp.zeros((SSM_GROUP, LANES), F32)], axis=1)
    shifted = [kext] + [pltpu.roll(kext, r * SSM_GROUP, axis=1) for r in range(1, per_vreg)]
    for t in range(t_len):
        q, r = divmod(t, per_vreg)
        blk = shifted[r][:, :width - q * LANES]
        if q:
            blk = jnp.concatenate([jnp.zeros((SSM_GROUP, q * LANES), F32), blk], axis=1)
        toep_ref[t * SSM_GROUP:(t + 1) * SSM_GROUP, :] = blk.astype(BF16)

    g = jnp.dot(x, mt_ref[...], preferred_element_type=F32)
    gr_ref[...] = g[:, :SSM_STATE]
    gi_ref[...] = g[:, SSM_STATE:]

    ar = ar_ref[...]
    ai = ai_ref[...]

    def advance(h_r, h_i, a_r, a_i, g_r, g_i):
        return a_r * h_r - a_i * h_i + g_r, a_r * h_i + a_i * h_r + g_i

    block = _scan_block(n_chunks)
    if block == 1:
        def chunk_step(n, carry):
            h_r, h_i = carry
            rows = pl.ds(n * n_seq, n_seq)
            hpr_ref[rows, :] = h_r
            hpi_ref[rows, :] = h_i
            return advance(h_r, h_i, ar, ai, gr_ref[rows, :], gi_ref[rows, :])

        if n_chunks == 1:
            h_r, h_i = chunk_step(0, (h0r_ref[...], h0i_ref[...]))
        else:
            h_r, h_i = lax.fori_loop(0, n_chunks, chunk_step, (h0r_ref[...], h0i_ref[...]))
    else:
        n_blocks = n_chunks // block
        slab = n_blocks * n_seq
        zero = jnp.zeros((slab, SSM_STATE), F32)
        l_r, l_i = zero, zero
        local = []
        for j in range(block):
            local.append((l_r, l_i))
            rows = slice(j * slab, (j + 1) * slab)
            l_r, l_i = advance(l_r, l_i, ar, ai, gr_ref[rows, :], gi_ref[rows, :])
        powers = [(jnp.ones_like(ar), jnp.zeros_like(ar))]
        for j in range(block):
            p_r, p_i = powers[-1]
            powers.append((p_r * ar - p_i * ai, p_r * ai + p_i * ar))
        ab_r, ab_i = powers[block]
        h_r, h_i = h0r_ref[...], h0i_ref[...]
        entry = []
        for b in range(n_blocks):
            entry.append((h_r, h_i))
            rows = slice(b * n_seq, (b + 1) * n_seq)
            h_r, h_i = advance(h_r, h_i, ab_r, ab_i, l_r[rows], l_i[rows])
        e_r = jnp.concatenate([e[0] for e in entry], axis=0)
        e_i = jnp.concatenate([e[1] for e in entry], axis=0)
        for j in range(block):
            p_r, p_i = powers[j]
            rows = slice(j * slab, (j + 1) * slab)
            hpr_ref[rows, :] = local[j][0] + p_r * e_r - p_i * e_i
            hpi_ref[rows, :] = local[j][1] + p_r * e_i + p_i * e_r
    hr_ref[...] = h_r
    hi_ref[...] = h_i

    y = jnp.dot(x, toep_ref[...], preferred_element_type=F32)
    hp = jnp.concatenate([hpr_ref[...], hpi_ref[...]], axis=1).astype(BF16)
    y = y + jnp.dot(hp, em_ref[:, :width], preferred_element_type=F32)
    return y + x.astype(F32) * dvec_ref[...]


def _ssm_kernel(x_ref, kflat_ref, mt_ref, em_ref, ar_ref, ai_ref,
                h0r_ref, h0i_ref, dvec_ref, y_ref, hr_ref, hi_ref,
                toep_ref, gr_ref, gi_ref, hpr_ref, hpi_ref, **static):
    for gi in range(x_ref.shape[0]):
        per_group = [r.at[gi] for r in (kflat_ref, mt_ref, em_ref, ar_ref, ai_ref, h0r_ref, h0i_ref,
                                        dvec_ref, hr_ref, hi_ref)]
        y = _ssm_group(x_ref[gi], *per_group, toep_ref, gr_ref, gi_ref, hpr_ref, hpi_ref, **static)
        y_ref[gi] = y.astype(BF16)


def _ssm(xg, prep, h0r, h0i, dvec, *, t_len, n_seq, n_chunks):
    kflat, mt, em, a64r, a64i, a16r, a16i = prep
    g, rows, width = xg.shape
    p = SSM_STATE
    full = CHUNK * SSM_GROUP
    if t_len == CHUNK:
        ar, ai = a64r, a64i
        mt_block = 0
    else:
        assert t_len == 16
        ar, ai = a16r, a16i
        mt_block = (full - width) // width
    kern = functools.partial(_ssm_kernel, t_len=t_len, n_seq=n_seq, n_chunks=n_chunks)
    per_step = SSM_GROUPS_PER_STEP if n_chunks == 1 else 1

    def gspec(*shape):
        return pl.BlockSpec((per_step,) + shape, lambda i: (i,) + (0,) * len(shape))

    mt_spec = pl.BlockSpec((per_step, width, 2 * p), lambda i: (i, mt_block, 0))
    return pl.pallas_call(
        kern,
        out_shape=[jax.ShapeDtypeStruct((g, rows, width), BF16),
                   jax.ShapeDtypeStruct((g, n_seq, p), F32),
                   jax.ShapeDtypeStruct((g, n_seq, p), F32)],
        grid=(g // per_step,),
        in_specs=[gspec(rows, width), gspec(SSM_GROUP, full), mt_spec,
                  gspec(2 * p, full), gspec(1, p), gspec(1, p),
                  gspec(n_seq, p), gspec(n_seq, p), gspec(1, width)],
        out_specs=[gspec(rows, width), gspec(n_seq, p), gspec(n_seq, p)],
        scratch_shapes=[pltpu.VMEM((width, width), BF16),
                        pltpu.VMEM((rows, p), F32), pltpu.VMEM((rows, p), F32),
                        pltpu.VMEM((rows, p), F32), pltpu.VMEM((rows, p), F32)],
        compiler_params=pltpu.CompilerParams(dimension_semantics=("arbitrary",)),
        name="ssm",
    )(xg, kflat, mt, em, ar, ai, h0r, h0i, dvec)


def _out_kernel(a_ref, ys_ref, gs_ref, x_ref, gate_ref, wglu_ref, bglu_ref, woa_ref, wos_ref,
                gpost_ref, o_ref):
    y = ys_ref[...].astype(F32)
    y = 0.5 * y * (1.0 + jnp.tanh(math.sqrt(2.0 / math.pi) * (y + 0.044715 * (y * y * y))))
    z = jnp.dot(y.astype(BF16), wglu_ref[...], preferred_element_type=F32) + bglu_ref[...]
    y = y * jax.nn.sigmoid(z) * gs_ref[...].astype(F32)
    out = (jnp.dot(a_ref[...], woa_ref[...], preferred_element_type=F32)
           + jnp.dot(y.astype(BF16), wos_ref[...], preferred_element_type=F32))
    ms = jnp.mean(out * out, axis=-1, keepdims=True)
    normed = out * lax.rsqrt(ms + RMS_EPS) * gpost_ref[...]
    o_ref[...] = x_ref[...] + gate_ref[...] * normed


def _out_proj(a, ys, gs, x2d, gate, w_glu, b_glu, w_out, g_post, *, tm, rows_per_mod):
    n, d = x2d.shape
    if rows_per_mod is None:
        mod_spec = pl.BlockSpec((tm, d), lambda i: (i, 0))
    else:
        tiles_per_mod = rows_per_mod // tm
        mod_spec = pl.BlockSpec((None, 1, d), lambda i: (i // tiles_per_mod, 0, 0))

    def row_spec(width):
        return pl.BlockSpec((tm, width), lambda i: (i, 0))

    wglu = w_glu.astype(BF16)
    woa = w_out[:D_ATTN].astype(BF16)
    wos = w_out[D_ATTN:].astype(BF16)
    return pl.pallas_call(
        _out_kernel,
        out_shape=jax.ShapeDtypeStruct((n, d), F32),
        grid=(n // tm,),
        in_specs=[row_spec(D_ATTN), row_spec(D_SSM), row_spec(D_SSM), row_spec(d), mod_spec,
                  _const_spec(wglu.shape), _const_spec((1, D_SSM)), _const_spec(woa.shape),
                  _const_spec(wos.shape), _const_spec((1, d))],
        out_specs=row_spec(d),
        compiler_params=pltpu.CompilerParams(dimension_semantics=("arbitrary",),
                                             vmem_limit_bytes=V7X_VMEM_LIMIT_BYTES),
        name="out_proj",
    )(a, ys, gs, x2d, gate, wglu, b_glu.reshape(1, D_SSM), woa, wos, g_post.reshape(1, d))


def _scan_block(n_chunks):
    return SCAN_BLOCK if n_chunks % SCAN_BLOCK == 0 else 1


def _to_groups(u2d, n_seq, n_chunks, t_len):
    j = _scan_block(n_chunks)
    u6 = u2d.reshape(n_seq, n_chunks // j, j, t_len, N_SSM_GROUPS, SSM_GROUP)
    return u6.transpose(4, 2, 1, 0, 3, 5).reshape(N_SSM_GROUPS, n_seq * n_chunks, t_len * SSM_GROUP)


def _from_groups(yg, n_seq, n_chunks, t_len):
    j = _scan_block(n_chunks)
    y6 = yg.reshape(N_SSM_GROUPS, j, n_chunks // j, n_seq, t_len, SSM_GROUP)
    return y6.transpose(3, 2, 1, 4, 0, 5).reshape(n_seq * n_chunks * t_len, D_SSM)


def _pad_rows(a, rows):
    return jnp.pad(a, ((0, 0), (0, rows - a.shape[1]), (0, 0)))


def _layer(x, mod, past, g_pre, g_post, weights, prep, d_skip, w_glu, b_glu, w_out):
    bsz, s, d = x.shape
    n = bsz * s
    x2d = x.reshape(n, d)
    shift, scale, gate = mod[:, :d], mod[:, d:2 * d], mod[:, 2 * d:]
    if past is None:
        tm = 256
        rows_per_mod = s
        mods = [m.reshape(bsz, 1, d) for m in (scale, shift, gate)]
    else:
        tm = n
        rows_per_mod = None
        mods = [jnp.repeat(m, s, axis=0) for m in (scale, shift, gate)]
    scale_m, shift_m, gate_m = mods

    (q, k32, kb, v32, vb, iq, ik32, iklo, ikhi, iw, ga, u, gs) = _proj(
        x2d, scale_m, shift_m, g_pre.reshape(1, d), weights, tm=tm, rows_per_mod=rows_per_mod)

    def seq(a):
        return a.reshape(bsz, s, a.shape[-1])

    if past is None:
        attn = _dsa(seq(q), seq(iq), seq(iw), seq(ga), seq(kb), seq(vb), seq(iklo), seq(ikhi),
                    qb=Q_BLOCK, causal=True, n_keys=s)
        t_len = CHUNK if s % CHUNK == 0 else s
        h0r = jnp.zeros((N_SSM_GROUPS, bsz, SSM_STATE), F32)
        h0i = h0r
    else:
        ck, cv, cik, sre, sim = past
        past_len = ck.shape[1]
        n_keys = past_len + s
        lk = -(-n_keys // KEY_TILE) * KEY_TILE
        k_all = jnp.concatenate([ck.reshape(bsz, past_len, KV_DIM).astype(BF16), seq(kb)], axis=1)
        v_all = jnp.concatenate([cv.reshape(bsz, past_len, KV_DIM).astype(BF16), seq(vb)], axis=1)
        cik_b = cik.astype(BF16)
        zeros = jnp.zeros_like(cik_b)
        iklo_all = jnp.concatenate([jnp.concatenate([cik_b, zeros], axis=-1), seq(iklo)], axis=1)
        ikhi_all = jnp.concatenate([jnp.concatenate([zeros, cik_b], axis=-1), seq(ikhi)], axis=1)
        attn = _dsa(seq(q), seq(iq), seq(iw), seq(ga), _pad_rows(k_all, lk), _pad_rows(v_all, lk),
                    _pad_rows(iklo_all, lk), _pad_rows(ikhi_all, lk),
                    qb=s, causal=False, n_keys=n_keys)
        t_len = CHUNK if s % CHUNK == 0 else s
        h0r = jnp.swapaxes(sre.astype(F32), 0, 1)
        h0i = jnp.swapaxes(sim.astype(F32), 0, 1)

    n_chunks = s // t_len
    dvec = jnp.tile(d_skip.astype(F32).reshape(N_SSM_GROUPS, 1, SSM_GROUP), (1, 1, t_len))
    rows = bsz * n_chunks
    rows_pad = -(-rows // 16) * 16
    xg = _to_groups(u, bsz, n_chunks, t_len)
    xg = _pad_rows(xg, rows_pad) if rows_pad != rows else xg
    if rows_pad != rows:
        assert n_chunks == 1
        h0r = _pad_rows(h0r, rows_pad)
        h0i = _pad_rows(h0i, rows_pad)
    yg, hr, hi = _ssm(xg, prep, h0r, h0i, dvec, t_len=t_len,
                      n_seq=rows_pad // n_chunks, n_chunks=n_chunks)
    ys = _from_groups(yg[:, :rows], bsz, n_chunks, t_len)
    hr = jnp.swapaxes(hr[:, :bsz], 0, 1)
    hi = jnp.swapaxes(hi[:, :bsz], 0, 1)

    y = _out_proj(attn.reshape(n, D_ATTN), ys, gs, x2d, gate_m, w_glu, b_glu, w_out, g_post,
                  tm=tm, rows_per_mod=rows_per_mod)
    k_out = k32.reshape(bsz, s, N_KV_HEADS, HEAD_DIM)
    v_out = v32.reshape(bsz, s, N_KV_HEADS, HEAD_DIM)
    ik_out = ik32.reshape(bsz, s, IDX_DIM)
    return y.reshape(bsz, s, d), (k_out, v_out, ik_out, hr, hi)


def kernel(x_prompt, x_sample, c_prompt, c_sample, cache_k, cache_v, cache_idx_k, state_ssm_re,
           state_ssm_im, w_ada, b_ada, g_pre, g_post, w_in, lambda_re, lambda_im, log_dt, b_re, b_im,
           c_re, c_im, d_skip, w_glu, b_glu, w_out):
    depth = w_ada.shape[0]
    bp = c_prompt.shape[0]
    bs = c_sample.shape[0]
    yp, ys = x_prompt, x_sample
    outs_p = [[] for _ in range(5)]
    outs_s = [[] for _ in range(5)]
    c_all = jnp.concatenate([c_prompt, c_sample], axis=0)
    c_rows = -(-c_all.shape[0] // 8) * 8
    c_all = jnp.pad(c_all, ((0, c_rows - c_all.shape[0]), (0, 0)))
    for l in range(depth):
        mod = _adaln(c_all, w_ada[l], b_ada[l].reshape(1, -1))
        weights = _split_w_in(w_in[l])
        prep = _ssm_prep(lambda_re[l], lambda_im[l], log_dt[l], b_re[l], b_im[l], c_re[l], c_im[l])
        common = (g_pre[l], g_post[l], weights, prep, d_skip[l], w_glu[l], b_glu[l], w_out[l])
        yp, new_p = _layer(yp, mod[:bp], None, *common)
        past = (cache_k[l], cache_v[l], cache_idx_k[l], state_ssm_re[l], state_ssm_im[l])
        ys, new_s = _layer(ys, mod[bp:bp + bs], past, *common)
        for acc, val in zip(outs_p, new_p):
            acc.append(val)
        for acc, val in zip(outs_s, new_s):
            acc.append(val)
    return (yp, ys) + tuple(jnp.stack(a) for a in outs_p) + tuple(jnp.stack(a) for a in outs_s)
```

```python
import functools
import math

import jax
import jax.numpy as jnp
from jax import lax
from jax.experimental import pallas as pl
from jax.experimental.pallas import tpu as pltpu

F32 = jnp.float32
BF16 = jnp.bfloat16
I32 = jnp.int32

CHUNK = 64
Q_BLOCK = 128
D_ATTN = 1024
N_HEADS = 8
N_KV_HEADS = 2
HEAD_DIM = 128
KV_DIM = N_KV_HEADS * HEAD_DIM
HEADS_PER_KV = N_HEADS // N_KV_HEADS
N_IDX_HEADS = 16
IDX_DIM = 64
IDX_PAIRS = N_IDX_HEADS // 2
TOPK_MAX = 256
D_SSM = 1024
SSM_GROUP = 16
N_SSM_GROUPS = D_SSM // SSM_GROUP
SSM_STATE = 64
RMS_EPS = 1e-6
IN_SIZES = (D_ATTN, KV_DIM, KV_DIM, N_IDX_HEADS * IDX_DIM, IDX_DIM, N_IDX_HEADS, D_ATTN, D_SSM, D_SSM)

LANES = 128
V7X_VMEM_LIMIT_BYTES = 56 * 1024 * 1024

INT_MIN = -(2 ** 31)
MASKED_LOGIT = -1e30
MIN_ROW_SUM = 2.0 ** -100
KEY_TILE = 512
SCORE_KEYS = 256
SCAN_BLOCK = 8
SSM_GROUPS_PER_STEP = 8
COARSE_STEPS = 9
FINE_STEPS = 24


def _const_spec(shape):
    nd = len(shape)
    return pl.BlockSpec(shape, lambda *_: (0,) * nd, pipeline_mode=pl.Buffered(1))


def _dot_nt(a, b):
    return lax.dot_general(a, b, (((1,), (1,)), ((), ())), preferred_element_type=F32)


def _adaln_kernel(c_ref, w_ref, b_ref, o_ref):
    o_ref[...] = jnp.dot(c_ref[...], w_ref[...], preferred_element_type=F32) + b_ref[...]


def _adaln(c, w_ada, b_ada):
    rows, d = c.shape
    n_out = w_ada.shape[1]
    tn = 512
    return pl.pallas_call(
        _adaln_kernel,
        out_shape=jax.ShapeDtypeStruct((rows, n_out), F32),
        grid=(n_out // tn,),
        in_specs=[pl.BlockSpec((rows, d), lambda j: (0, 0)),
                  pl.BlockSpec((d, tn), lambda j: (0, j)),
                  pl.BlockSpec((1, tn), lambda j: (0, j))],
        out_specs=pl.BlockSpec((rows, tn), lambda j: (0, j)),
        compiler_params=pltpu.CompilerParams(dimension_semantics=("arbitrary",)),
        name="adaln",
    )(c, w_ada, b_ada)


def _proj_kernel(x_ref, scale_ref, shift_ref, g_ref,
                 wq_ref, wk_ref, wv_ref, wiq_ref, wnar_ref, wza_ref, wu_ref, wzs_ref,
                 q_ref, k_ref, kb_ref, v_ref, vb_ref, iq_ref, ik_ref, iklo_ref, ikhi_ref,
                 iw_ref, ga_ref, u_ref, gs_ref):
    x = x_ref[...]
    ms = jnp.mean(x * x, axis=-1, keepdims=True)
    y = x * lax.rsqrt(ms + RMS_EPS) * g_ref[...]
    h = (y * (1.0 + scale_ref[...]) + shift_ref[...]).astype(BF16)

    def mm(w_ref):
        return jnp.dot(h, w_ref[...], preferred_element_type=F32)

    q_ref[...] = (mm(wq_ref) * (HEAD_DIM ** -0.5 * math.log2(math.e))).astype(BF16)
    rows = x.shape[0]
    zk = mm(wk_ref)
    zv = mm(wv_ref)
    for hd in range(N_KV_HEADS):
        k_ref[pl.ds(hd, rows, stride=N_KV_HEADS), :] = zk[:, hd * HEAD_DIM:(hd + 1) * HEAD_DIM]
        v_ref[pl.ds(hd, rows, stride=N_KV_HEADS), :] = zv[:, hd * HEAD_DIM:(hd + 1) * HEAD_DIM]
    kb_ref[...] = zk.astype(BF16)
    vb_ref[...] = zv.astype(BF16)
    iq_ref[...] = (mm(wiq_ref) * (IDX_DIM ** -0.5)).astype(BF16)
    znar = mm(wnar_ref)
    zik = znar[:, :LANES]
    ik_ref[...] = zik[:, :IDX_DIM]
    lane = lax.broadcasted_iota(I32, zik.shape, 1)
    iklo_ref[...] = jnp.where(lane < IDX_DIM, zik, 0.0).astype(BF16)
    ikhi_ref[...] = jnp.where(lane >= IDX_DIM, zik, 0.0).astype(BF16)
    iw_ref[...] = znar[:, LANES:LANES + N_IDX_HEADS] * (N_IDX_HEADS ** -0.5)
    za = mm(wza_ref)
    ga_ref[...] = (za * jax.nn.sigmoid(za)).astype(BF16)
    u_ref[...] = mm(wu_ref).astype(BF16)
    zs = mm(wzs_ref)
    gs_ref[...] = (zs * jax.nn.sigmoid(zs)).astype(BF16)


def _split_w_in(w_in):
    offs = [0]
    for s in IN_SIZES:
        offs.append(offs[-1] + s)
    cols = [w_in[:, offs[i]:offs[i + 1]].astype(BF16) for i in range(len(IN_SIZES))]
    wq, wk, wv, wiq, wik, wiw, wza, wu, wzs = cols
    wnar = jnp.concatenate([wik, wik, jnp.pad(wiw, ((0, 0), (0, LANES - N_IDX_HEADS)))], axis=1)
    return (wq, wk, wv, wiq, wnar, wza, wu, wzs)


def _proj(x2d, scale, shift, g_pre, weights, *, tm, rows_per_mod):
    n, d = x2d.shape
    if rows_per_mod is None:
        mod_spec = pl.BlockSpec((tm, d), lambda i: (i, 0))
    else:
        tiles_per_mod = rows_per_mod // tm
        mod_spec = pl.BlockSpec((None, 1, d), lambda i: (i // tiles_per_mod, 0, 0))

    def row_spec(width):
        return pl.BlockSpec((tm, width), lambda i: (i, 0))

    out_defs = [
        (1, D_ATTN, BF16), (N_KV_HEADS, HEAD_DIM, F32), (1, KV_DIM, BF16), (N_KV_HEADS, HEAD_DIM, F32),
        (1, KV_DIM, BF16), (1, N_IDX_HEADS * IDX_DIM, BF16), (1, IDX_DIM, F32), (1, LANES, BF16),
        (1, LANES, BF16), (1, N_IDX_HEADS, F32), (1, D_ATTN, BF16), (1, D_SSM, BF16), (1, D_SSM, BF16)]
    return pl.pallas_call(
        _proj_kernel,
        out_shape=[jax.ShapeDtypeStruct((r * n, w), dt) for r, w, dt in out_defs],
        grid=(n // tm,),
        in_specs=[row_spec(d), mod_spec, mod_spec, _const_spec((1, d))]
                 + [_const_spec(w.shape) for w in weights],
        out_specs=[pl.BlockSpec((r * tm, w), lambda i: (i, 0)) for r, w, _ in out_defs],
        compiler_params=pltpu.CompilerParams(dimension_semantics=("arbitrary",),
                                             vmem_limit_bytes=V7X_VMEM_LIMIT_BYTES),
        name="in_proj",
    )(x2d, scale, shift, g_pre, *weights)


def _dsa_kernel(q_ref, iq_ref, iw_ref, ga_ref, kb_ref, vb_ref, iklo_ref, ikhi_ref, o_ref,
                sc_ref, wb_ref, iqs_ref, qs_ref, m_ref, l_ref, acc_ref, kn_ref,
                *, qb, tk, topk, causal, n_keys, n_keys_pad):
    blk = pl.program_id(1)
    chunks_per_tile = tk // LANES
    row = lax.broadcasted_iota(I32, (qb, LANES), 0)
    if causal:
        n_vis = (blk + 1) * qb
        limit = (((blk * qb + row) // CHUNK) + 1) * CHUNK
        n_tiles = (n_vis + tk - 1) // tk
    else:
        limit = jnp.full((qb, LANES), n_keys, I32)
        n_tiles = (n_keys + tk - 1) // tk

    for j in range(IDX_PAIRS):
        iqs_ref[j * qb:(j + 1) * qb, :] = iq_ref[:, j * LANES:(j + 1) * LANES]
    for kv in range(N_KV_HEADS):
        for hh in range(HEADS_PER_KV):
            c0 = (kv * HEADS_PER_KV + hh) * HEAD_DIM
            qs_ref[kv, hh * qb:(hh + 1) * qb, :] = q_ref[:, c0:c0 + HEAD_DIM]
    iw = iw_ref[...]
    for h in range(N_IDX_HEADS):
        wb_ref[h] = jnp.broadcast_to(iw[:, h:h + 1], (qb, LANES))

    lane = lax.broadcasted_iota(I32, (qb, LANES), 1)

    def score_keys(first_chunk, n_matmuls, carry):
        m1, m2 = carry
        iqs = iqs_ref[...]
        for part in range(n_matmuls):
            chunk0 = first_chunk + part * (SCORE_KEYS // LANES)
            k0 = pl.multiple_of(chunk0 * LANES, SCORE_KEYS)
            s_lo = _dot_nt(iqs, iklo_ref[pl.ds(k0, SCORE_KEYS), :])
            s_hi = _dot_nt(iqs, ikhi_ref[pl.ds(k0, SCORE_KEYS), :])
            for c in range(SCORE_KEYS // LANES):
                cs = slice(c * LANES, (c + 1) * LANES)
                acc = None
                for j in range(IDX_PAIRS):
                    rs = slice(j * qb, (j + 1) * qb)
                    term = (jnp.maximum(s_lo[rs, cs], 0.0) * wb_ref[2 * j]
                            + jnp.maximum(s_hi[rs, cs], 0.0) * wb_ref[2 * j + 1])
                    acc = term if acc is None else acc + term
                ci = chunk0 + c
                visible = (ci * LANES + lane) < limit
                sc_ref[ci] = jnp.where(visible, acc, jnp.nan)
                v = jnp.where(visible, acc, -jnp.inf)
                m1, v = jnp.maximum(m1, v), jnp.minimum(m1, v)
                m2 = jnp.maximum(m2, v)
        return m1, m2

    def over_tile_pairs(step, carry):
        carry = lax.fori_loop(0, n_tiles // 2, lambda t, c: step(2 * t, 2, c), carry)
        return lax.cond(n_tiles % 2 == 1, lambda c: step(n_tiles - 1, 1, c), lambda c: c, carry)

    ninf = jnp.full((qb, LANES), -jnp.inf, F32)
    m1, m2 = over_tile_pairs(
        lambda t0, nt, c: score_keys(t0 * chunks_per_tile, nt * tk // SCORE_KEYS, c), (ninf, ninf))

    def lane_fold(x, op):
        return jnp.broadcast_to(op(x, axis=1, keepdims=True), (qb, LANES))

    def count_rows(pred):
        def count_tile(t, cnt):
            for c in range(chunks_per_tile):
                ci = t * chunks_per_tile + c
                cnt = cnt + jnp.where(pred(sc_ref[ci], ci * LANES + lane), 1.0, 0.0)
            return cnt

        return lane_fold(lax.fori_loop(0, n_tiles, count_tile, jnp.zeros((qb, LANES), F32)), jnp.sum)

    def key_bisection():
        def ordered_to_f32(key):
            return lax.bitcast_convert_type(jnp.where(key < 0, key ^ jnp.int32(0x7FFFFFFF), key), F32)

        def bisect(i, carry):
            lo, cnt_lo = carry
            cand_key = lo + lax.shift_left(jnp.int32(1), 31 - i)
            cand = ordered_to_f32(cand_key)
            total = count_rows(lambda s, _: s >= cand)
            take = total >= topk
            return jnp.where(take, cand_key, lo), jnp.where(take, total, cnt_lo)

        lo, cnt_lo = lax.fori_loop(0, 32, bisect, (jnp.full((qb, LANES), INT_MIN, I32),
                                                   jnp.zeros((qb, LANES), F32)))
        return ordered_to_f32(lo), cnt_lo

    few = limit < topk

    bracketed = limit >= 2 * LANES
    top = m2 if topk > LANES else m1
    lo_f = jnp.where(bracketed, lane_fold(m2, jnp.min), 0.0)
    hi_f = jnp.where(bracketed, lane_fold(top, jnp.max), 0.0)
    hi_f = hi_f + (jnp.abs(hi_f) * 2.0 ** -20 + 1e-30)

    cnt_hi = lane_fold(jnp.where(m1 >= hi_f, 1.0, 0.0), jnp.sum)

    def halve(_, carry):
        lo_f, hi_f, cnt_hi = carry
        mid = lo_f + (hi_f - lo_f) * 0.5
        total = count_rows(lambda s, _: s >= mid)
        take = total >= topk
        return jnp.where(take, mid, lo_f), jnp.where(take, hi_f, mid), jnp.where(take, cnt_hi, total)

    lo_f, hi_f, n_above = lax.fori_loop(0, COARSE_STEPS, halve, (lo_f, hi_f, cnt_hi))

    def bucket_rows(rs):
        lo_r, hi_r = lo_f[rs], hi_f[rs]

        def bucket_tile(t, carry):
            b1, b2, b3, pop = carry
            for c in range(chunks_per_tile):
                s = sc_ref[t * chunks_per_tile + c, rs, :]
                inb = (s >= lo_r) & jnp.logical_not(s >= hi_r)
                v = jnp.where(inb, s, -jnp.inf)
                pop = pop + jnp.where(inb, 1.0, 0.0)
                b1, v = jnp.maximum(b1, v), jnp.minimum(b1, v)
                b2, v = jnp.maximum(b2, v), jnp.minimum(b2, v)
                b3 = jnp.maximum(b3, v)
            return b1, b2, b3, pop

        return lax.fori_loop(0, n_tiles, bucket_tile, (ninf[rs], ninf[rs], ninf[rs], jnp.zeros_like(lo_r)))

    halves = [bucket_rows(slice(h * qb // 2, (h + 1) * qb // 2)) for h in range(2)]
    b1, b2, b3, pop = (jnp.concatenate([half[k] for half in halves], axis=0) for k in range(4))
    want = topk - n_above

    transposed = qb == LANES
    if transposed:
        cands = jnp.concatenate([b1.T, b2.T, b3.T], axis=0)
        want_q = want.T[:1]
        axis = 0
    else:
        cands = jnp.concatenate([b1, b2, b3], axis=1)
        want_q = want[:, :1]
        axis = 1

    def walk(_, carry):
        prev, covered, tau_q, cnt_q = carry
        cur = jnp.max(jnp.where(cands < prev, cands, -jnp.inf), axis=axis, keepdims=True)
        covered_new = covered + jnp.sum(jnp.where(cands == cur, 1.0, 0.0), axis=axis, keepdims=True)
        found = (covered < want_q) & (covered_new >= want_q) & (cur > -jnp.inf)
        return cur, covered_new, jnp.where(found, cur, tau_q), jnp.where(found, covered_new, cnt_q)

    n_walk = jnp.minimum(jnp.max(jnp.where(few, 0.0, want)), float(FINE_STEPS)).astype(I32)
    nan_q = jnp.full(want_q.shape, jnp.nan, F32)
    _, _, tau_q, cnt_q = lax.fori_loop(
        0, n_walk, walk,
        (jnp.full(want_q.shape, jnp.inf, F32), jnp.zeros(want_q.shape, F32), nan_q, nan_q))
    if transposed:
        tau_fast = jnp.broadcast_to(tau_q, (LANES, qb)).T
        cnt_fast = n_above + jnp.broadcast_to(cnt_q, (LANES, qb)).T
    else:
        tau_fast = jnp.broadcast_to(tau_q, (qb, LANES))
        cnt_fast = n_above + jnp.broadcast_to(cnt_q, (qb, LANES))

    bad = jnp.logical_not(few) & (jnp.logical_not(bracketed) | (pop > 3.0) | (want < 1.0)
                                  | jnp.logical_not(tau_fast == tau_fast))
    tau, cnt_tau = lax.cond(jnp.max(jnp.where(bad, 1.0, 0.0)) > 0.0, key_bisection,
                            lambda: (tau_fast, cnt_fast))
    tau = jnp.where(few, -jnp.inf, tau)
    excess = jnp.where(few, 0.0, cnt_tau - topk)

    @pl.when(jnp.max(excess) > 0)
    def _resolve_surplus():
        idx_bits = (n_keys_pad - 1).bit_length()

        def kept(s, kidx, vstar, jstar):
            return (s >= tau) & ((s > vstar) | ((s == vstar) & (kidx < jstar)))

        def drop_step(state):
            exc, vstar, jstar = state

            def min_tile(t, vm):
                for c in range(chunks_per_tile):
                    ci = t * chunks_per_tile + c
                    s = sc_ref[ci]
                    vm = jnp.minimum(vm, jnp.where(kept(s, ci * LANES + lane, vstar, jstar), s, jnp.inf))
                return vm

            vm = lax.fori_loop(0, n_tiles, min_tile, jnp.full((qb, LANES), jnp.inf, F32))
            vmin = jnp.broadcast_to(jnp.min(vm, axis=1, keepdims=True), (qb, LANES))

            def tied(s, kidx):
                return kept(s, kidx, vstar, jstar) & (s == vmin)

            cmin = count_rows(tied)
            active = exc > 0
            drop_all = active & (exc >= cmin)
            partial = active & (exc < cmin)
            keep = cmin - exc

            def idx_bisect(b, j0):
                cand_j = j0 + lax.shift_left(jnp.int32(1), idx_bits - 1 - b)
                below = count_rows(lambda s, kidx: tied(s, kidx) & (kidx < cand_j))
                return jnp.where(below < keep, cand_j, j0)

            j0 = lax.fori_loop(0, idx_bits, idx_bisect, jnp.zeros((qb, LANES), I32))
            vstar = jnp.where(active, vmin, vstar)
            jstar = jnp.where(drop_all, 0, jnp.where(partial, j0 + 1, jstar))
            exc = jnp.where(drop_all, exc - cmin, jnp.where(partial, 0, exc))
            return exc, vstar, jstar

        _, vstar, jstar = lax.while_loop(
            lambda state: jnp.max(state[0]) > 0, drop_step,
            (excess, tau, jnp.full((qb, LANES), 2 ** 30, I32)))

        def rewrite_tile(t, carry):
            for c in range(chunks_per_tile):
                ci = t * chunks_per_tile + c
                s = sc_ref[ci]
                dropped = (s >= tau) & jnp.logical_not(kept(s, ci * LANES + lane, vstar, jstar))
                sc_ref[ci] = jnp.where(dropped, jnp.nan, s)
            return carry

        lax.fori_loop(0, n_tiles, rewrite_tile, 0)

    def masked_logits(t0, nt, kv):
        ks = pl.multiple_of(t0 * tk, tk)
        bias = jnp.concatenate(
            [jnp.where(sc_ref[t0 * chunks_per_tile + c] >= tau, 0.0, MASKED_LOGIT)
             for c in range(nt * chunks_per_tile)], axis=1)
        s = _dot_nt(qs_ref[kv], kb_ref[pl.ds(ks, nt * tk), kv * HEAD_DIM:(kv + 1) * HEAD_DIM])
        return (s.reshape(HEADS_PER_KV, qb, nt * tk) + bias[None]).reshape(HEADS_PER_KV * qb, nt * tk)

    def attend_tiles(t0, nt, carry):
        ks = pl.multiple_of(t0 * tk, tk)
        for kv in range(N_KV_HEADS):
            s = masked_logits(t0, nt, kv)
            m = m_ref[kv]
            l = l_ref[kv]
            ps = []
            for c in range(nt * chunks_per_tile):
                e = jnp.exp2(s[:, c * LANES:(c + 1) * LANES] - m)
                l = l + e
                ps.append(e.astype(BF16))
            l_ref[kv] = l
            p = jnp.concatenate(ps, axis=1)
            acc_ref[kv] += jnp.dot(p, vb_ref[pl.ds(ks, nt * tk), kv * HEAD_DIM:(kv + 1) * HEAD_DIM],
                                   preferred_element_type=F32)
        return carry

    def attend():
        l_ref[...] = jnp.zeros(l_ref.shape, F32)
        acc_ref[...] = jnp.zeros(acc_ref.shape, F32)
        over_tile_pairs(attend_tiles, 0)
        for kv in range(N_KV_HEADS):
            l_ref[kv] = jnp.broadcast_to(jnp.sum(l_ref[kv], axis=1, keepdims=True), l_ref.shape[1:])

    @pl.when(blk == 0)
    def _key_norms():
        for kv in range(N_KV_HEADS):
            def norm_tile(t, mx):
                ks = pl.multiple_of(t * tk, tk)
                k = kb_ref[pl.ds(ks, tk), kv * HEAD_DIM:(kv + 1) * HEAD_DIM].astype(F32)
                return jnp.maximum(mx, jnp.sum(k * k, axis=1, keepdims=True))

            mx = lax.fori_loop(0, n_keys_pad // tk, norm_tile, jnp.zeros((tk, 1), F32))
            kn_ref[kv] = jnp.broadcast_to(jnp.sqrt(jnp.max(mx, axis=0, keepdims=True)), kn_ref.shape[1:])

    for kv in range(N_KV_HEADS):
        qf = qs_ref[kv].astype(F32)
        qn = jnp.sqrt(jnp.sum(qf * qf, axis=1, keepdims=True))
        m_ref[kv] = jnp.broadcast_to(qn, m_ref.shape[1:]) * kn_ref[kv, 0:1, :]
    attend()

    l_min = jnp.min(l_ref[...])

    @pl.when(jnp.logical_not(l_min >= MIN_ROW_SUM))
    def _exact_shift():
        m_ref[...] = jnp.full(m_ref.shape, MASKED_LOGIT, F32)

        def max_tile(t, carry):
            for kv in range(N_KV_HEADS):
                s = masked_logits(t, 1, kv)
                mx = m_ref[kv]
                for c in range(chunks_per_tile):
                    mx = jnp.maximum(mx, s[:, c * LANES:(c + 1) * LANES])
                m_ref[kv] = mx
            return carry

        lax.fori_loop(0, n_tiles, max_tile, 0)
        for kv in range(N_KV_HEADS):
            m_ref[kv] = jnp.broadcast_to(jnp.max(m_ref[kv], axis=1, keepdims=True), m_ref.shape[1:])
        attend()

    for kv in range(N_KV_HEADS):
        o = acc_ref[kv] / l_ref[kv]
        for hh in range(HEADS_PER_KV):
            c0 = (kv * HEADS_PER_KV + hh) * HEAD_DIM
            gate = ga_ref[:, c0:c0 + HEAD_DIM].astype(F32)
            o_ref[:, c0:c0 + HEAD_DIM] = (o[hh * qb:(hh + 1) * qb] * gate).astype(BF16)


def _dsa(q, iq, iw, ga, kb, vb, iklo, ikhi, *, qb, causal, n_keys):
    b, s, _ = q.shape
    lk = kb.shape[1]
    tk = KEY_TILE
    assert lk % tk == 0 and s % qb == 0
    topk = min(TOPK_MAX, n_keys // 4)
    kern = functools.partial(_dsa_kernel, qb=qb, tk=tk, topk=topk, causal=causal, n_keys=n_keys,
                             n_keys_pad=lk)

    def q_spec(width):
        return pl.BlockSpec((None, qb, width), lambda bi, i: (bi, i, 0))

    def kv_spec(width):
        return pl.BlockSpec((None, lk, width), lambda bi, i: (bi, 0, 0))

    return pl.pallas_call(
        kern,
        out_shape=jax.ShapeDtypeStruct((b, s, D_ATTN), BF16),
        grid=(b, s // qb),
        in_specs=[q_spec(D_ATTN), q_spec(N_IDX_HEADS * IDX_DIM), q_spec(N_IDX_HEADS), q_spec(D_ATTN),
                  kv_spec(KV_DIM), kv_spec(KV_DIM), kv_spec(LANES), kv_spec(LANES)],
        out_specs=q_spec(D_ATTN),
        scratch_shapes=[
            pltpu.VMEM((lk // LANES, qb, LANES), F32),
            pltpu.VMEM((N_IDX_HEADS, qb, LANES), F32),
            pltpu.VMEM((IDX_PAIRS * qb, LANES), BF16),
            pltpu.VMEM((N_KV_HEADS, HEADS_PER_KV * qb, HEAD_DIM), BF16),
            pltpu.VMEM((N_KV_HEADS, HEADS_PER_KV * qb, LANES), F32),
            pltpu.VMEM((N_KV_HEADS, HEADS_PER_KV * qb, LANES), F32),
            pltpu.VMEM((N_KV_HEADS, HEADS_PER_KV * qb, HEAD_DIM), F32),
            pltpu.VMEM((N_KV_HEADS, 8, LANES), F32),
        ],
        compiler_params=pltpu.CompilerParams(dimension_semantics=("arbitrary", "arbitrary"),
                                             vmem_limit_bytes=V7X_VMEM_LIMIT_BYTES),
        name="dsa",
    )(q, iq, iw, ga, kb, vb, iklo, ikhi)


def _ssm_prep_kernel(lre_c_ref, lim_c_ref, lre_r_ref, lim_r_ref, ldt_ref, btr_ref, bti_ref,
                     ctr_ref, cti_ref, kflat_ref, mt_ref, em_ref,
                     a64r_ref, a64i_ref, a16r_ref, a16i_ref):
    hp = lax.Precision.HIGHEST
    dt = jnp.exp(ldt_ref[...])
    n_lag = CHUNK
    width = n_lag * SSM_GROUP

    ldr_c = lre_c_ref[...] * dt
    ldi_c = lim_c_ref[...] * dt
    lag = lax.broadcasted_iota(I32, (1, n_lag), 1).astype(F32)
    mag = jnp.exp(ldr_c * lag)
    pr_lag = mag * jnp.cos(ldi_c * lag)
    pi_lag = mag * jnp.sin(ldi_c * lag)
    def spread(x, pick):
        total = None
        for _ in range(3):
            piece = x.astype(BF16)
            x = x - piece.astype(F32)
            part = jnp.dot(piece, pick, preferred_element_type=F32)
            total = part if total is None else total + part
        return total

    col = lax.broadcasted_iota(I32, (n_lag, width), 1)
    pick_lag = (col // SSM_GROUP == lax.broadcasted_iota(I32, (n_lag, width), 0)).astype(BF16)
    col = lax.broadcasted_iota(I32, (SSM_GROUP, width), 1)
    pick_ch = (col % SSM_GROUP == lax.broadcasted_iota(I32, (SSM_GROUP, width), 0)).astype(BF16)
    pr = spread(pr_lag, pick_lag)
    pi = spread(pi_lag, pick_lag)
    ctr = spread(ctr_ref[...], pick_ch)
    cti = spread(cti_ref[...], pick_ch)
    qr = pr * ctr - pi * cti
    qi = pr * cti + pi * ctr

    lre_r = lre_r_ref[...]
    lim_r = lim_r_ref[...]
    ldr_r = lre_r * dt
    ldi_r = lim_r * dt
    lbr = jnp.exp(ldr_r) * jnp.cos(ldi_r)
    lbi = jnp.exp(ldr_r) * jnp.sin(ldi_r)
    den = lre_r * lre_r + lim_r * lim_r
    nr = lbr - 1.0
    fr = (nr * lre_r + lbi * lim_r) / den
    fi = (lbi * lre_r - nr * lim_r) / den
    btr = btr_ref[...]
    bti = bti_ref[...]
    bbr = fr * btr - fi * bti
    bbi = fr * bti + fi * btr

    kflat_ref[...] = (jnp.dot(bbr, qr, precision=hp, preferred_element_type=F32)
                      - jnp.dot(bbi, qi, precision=hp, preferred_element_type=F32))

    lbr_c = jnp.exp(ldr_c) * jnp.cos(ldi_c)
    lbi_c = jnp.exp(ldr_c) * jnp.sin(ldi_c)
    n_state = lbr_c.shape[0]
    em_ref[:n_state, :] = (lbr_c * qr - lbi_c * qi).astype(BF16)
    em_ref[n_state:, :] = (-(lbr_c * qi + lbi_c * qr)).astype(BF16)

    back = (n_lag - 1 - lax.broadcasted_iota(I32, (n_lag, 1), 0)).astype(F32)
    bmag = jnp.exp(ldr_r * back)
    bpr = bmag * jnp.cos(ldi_r * back)
    bpi = bmag * jnp.sin(ldi_r * back)
    for t in range(n_lag):
        wr = bpr[t:t + 1, :]
        wi = bpi[t:t + 1, :]
        rows = slice(t * SSM_GROUP, (t + 1) * SSM_GROUP)
        mt_ref[rows, :] = jnp.concatenate([wr * bbr - wi * bbi, wr * bbi + wi * bbr], axis=1).astype(BF16)

    for steps, ar_ref, ai_ref in ((float(CHUNK), a64r_ref, a64i_ref), (16.0, a16r_ref, a16i_ref)):
        amag = jnp.exp(ldr_r * steps)
        ar_ref[...] = amag * jnp.cos(ldi_r * steps)
        ai_ref[...] = amag * jnp.sin(ldi_r * steps)


def _ssm_prep(lambda_re, lambda_im, log_dt, b_re, b_im, c_re, c_im):
    g, p = lambda_re.shape
    width = CHUNK * SSM_GROUP
    lre_c = lambda_re.reshape(g, p, 1)
    lim_c = lambda_im.reshape(g, p, 1)
    lre_r = lambda_re.reshape(g, 1, p)
    lim_r = lambda_im.reshape(g, 1, p)
    ldt = log_dt.reshape(g, 1, 1)
    btr = jnp.swapaxes(b_re, 1, 2)
    bti = jnp.swapaxes(b_im, 1, 2)
    ctr = jnp.swapaxes(c_re, 1, 2)
    cti = jnp.swapaxes(c_im, 1, 2)

    def gspec(*shape):
        return pl.BlockSpec((None,) + shape, lambda i: (i,) + (0,) * len(shape))

    out_defs = [((SSM_GROUP, width), F32), ((width, 2 * p), BF16), ((2 * p, width), BF16),
                ((1, p), F32), ((1, p), F32), ((1, p), F32), ((1, p), F32)]
    return pl.pallas_call(
        _ssm_prep_kernel,
        out_shape=[jax.ShapeDtypeStruct((g,) + s, dt) for s, dt in out_defs],
        grid=(g,),
        in_specs=[gspec(p, 1), gspec(p, 1), gspec(1, p), gspec(1, p), gspec(1, 1),
                  gspec(SSM_GROUP, p), gspec(SSM_GROUP, p), gspec(p, SSM_GROUP), gspec(p, SSM_GROUP)],
        out_specs=[gspec(*s) for s, _ in out_defs],
        compiler_params=pltpu.CompilerParams(dimension_semantics=("arbitrary",)),
        name="ssm_prep",
    )(lre_c, lim_c, lre_r, lim_r, ldt, btr, bti, ctr, cti)


def _ssm_group(x, kflat_ref, mt_ref, em_ref, ar_ref, ai_ref,
               h0r_ref, h0i_ref, dvec_ref, hr_ref, hi_ref,
               toep_ref, gr_ref, gi_ref, hpr_ref, hpi_ref, *, t_len, n_seq, n_chunks):
    width = t_len * SSM_GROUP
    per_vreg = LANES // SSM_GROUP
    kflat = kflat_ref[:, :width]
    kext = jnp.concatenate([kflat, jnp.zeros((SSM_GROUP, LANES), F32)], axis=1)
    shifted = [kext] + [pltpu.roll(kext, r * SSM_GROUP, axis=1) for r in range(1, per_vreg)]
    for t in range(t_len):
        q, r = divmod(t, per_vreg)
        blk = shifted[r][:, :width - q * LANES]
        if q:
            blk = jnp.concatenate([jnp.zeros((SSM_GROUP, q * LANES), F32), blk], axis=1)
        toep_ref[t * SSM_GROUP:(t + 1) * SSM_GROUP, :] = blk.astype(BF16)

    g = jnp.dot(x, mt_ref[...], preferred_element_type=F32)
    gr_ref[...] = g[:, :SSM_STATE]
    gi_ref[...] = g[:, SSM_STATE:]

    ar = ar_ref[...]
    ai = ai_ref[...]

    def advance(h_r, h_i, a_r, a_i, g_r, g_i):
        return a_r * h_r - a_i * h_i + g_r, a_r * h_i + a_i * h_r + g_i

    block = _scan_block(n_chunks)
    if block == 1:
        def chunk_step(n, carry):
            h_r, h_i = carry
            rows = pl.ds(n * n_seq, n_seq)
            hpr_ref[rows, :] = h_r
            hpi_ref[rows, :] = h_i
            return advance(h_r, h_i, ar, ai, gr_ref[rows, :], gi_ref[rows, :])

        if n_chunks == 1:
            h_r, h_i = chunk_step(0, (h0r_ref[...], h0i_ref[...]))
        else:
            h_r, h_i = lax.fori_loop(0, n_chunks, chunk_step, (h0r_ref[...], h0i_ref[...]))
    else:
        n_blocks = n_chunks // block
        slab = n_blocks * n_seq
        zero = jnp.zeros((slab, SSM_STATE), F32)
        l_r, l_i = zero, zero
        local = []
        for j in range(block):
            local.append((l_r, l_i))
            rows = slice(j * slab, (j + 1) * slab)
            l_r, l_i = advance(l_r, l_i, ar, ai, gr_ref[rows, :], gi_ref[rows, :])
        powers = [(jnp.ones_like(ar), jnp.zeros_like(ar))]
        for j in range(block):
            p_r, p_i = powers[-1]
            powers.append((p_r * ar - p_i * ai, p_r * ai + p_i * ar))
        ab_r, ab_i = powers[block]
        h_r, h_i = h0r_ref[...], h0i_ref[...]
        entry = []
        for b in range(n_blocks):
            entry.append((h_r, h_i))
            rows = slice(b * n_seq, (b + 1) * n_seq)
            h_r, h_i = advance(h_r, h_i, ab_r, ab_i, l_r[rows], l_i[rows])
        e_r = jnp.concatenate([e[0] for e in entry], axis=0)
        e_i = jnp.concatenate([e[1] for e in entry], axis=0)
        for j in range(block):
            p_r, p_i = powers[j]
            rows = slice(j * slab, (j + 1) * slab)
            hpr_ref[rows, :] = local[j][0] + p_r * e_r - p_i * e_i
            hpi_ref[rows, :] = local[j][1] + p_r * e_i + p_i * e_r
    hr_ref[...] = h_r
    hi_ref[...] = h_i

    y = jnp.dot(x, toep_ref[...], preferred_element_type=F32)
    hp = jnp.concatenate([hpr_ref[...], hpi_ref[...]], axis=1).astype(BF16)
    y = y + jnp.dot(hp, em_ref[:, :width], preferred_element_type=F32)
    return y + x.astype(F32) * dvec_ref[...]


def _ssm_kernel(x_ref, kflat_ref, mt_ref, em_ref, ar_ref, ai_ref,
                h0r_ref, h0i_ref, dvec_ref, y_ref, hr_ref, hi_ref,
                toep_ref, gr_ref, gi_ref, hpr_ref, hpi_ref, **static):
    for gi in range(x_ref.shape[0]):
        per_group = [r.at[gi] for r in (kflat_ref, mt_ref, em_ref, ar_ref, ai_ref, h0r_ref, h0i_ref,
                                        dvec_ref, hr_ref, hi_ref)]
        y = _ssm_group(x_ref[gi], *per_group, toep_ref, gr_ref, gi_ref, hpr_ref, hpi_ref, **static)
        y_ref[gi] = y.astype(BF16)


def _ssm(xg, prep, h0r, h0i, dvec, *, t_len, n_seq, n_chunks):
    kflat, mt, em, a64r, a64i, a16r, a16i = prep
    g, rows, width = xg.shape
    p = SSM_STATE
    full = CHUNK * SSM_GROUP
    if t_len == CHUNK:
        ar, ai = a64r, a64i
        mt_block = 0
    else:
        assert t_len == 16
        ar, ai = a16r, a16i
        mt_block = (full - width) // width
    kern = functools.partial(_ssm_kernel, t_len=t_len, n_seq=n_seq, n_chunks=n_chunks)
    per_step = SSM_GROUPS_PER_STEP if n_chunks == 1 else 1

    def gspec(*shape):
        return pl.BlockSpec((per_step,) + shape, lambda i: (i,) + (0,) * len(shape))

    mt_spec = pl.BlockSpec((per_step, width, 2 * p), lambda i: (i, mt_block, 0))
    return pl.pallas_call(
        kern,
        out_shape=[jax.ShapeDtypeStruct((g, rows, width), BF16),
                   jax.ShapeDtypeStruct((g, n_seq, p), F32),
                   jax.ShapeDtypeStruct((g, n_seq, p), F32)],
        grid=(g // per_step,),
        in_specs=[gspec(rows, width), gspec(SSM_GROUP, full), mt_spec,
                  gspec(2 * p, full), gspec(1, p), gspec(1, p),
                  gspec(n_seq, p), gspec(n_seq, p), gspec(1, width)],
        out_specs=[gspec(rows, width), gspec(n_seq, p), gspec(n_seq, p)],
        scratch_shapes=[pltpu.VMEM((width, width), BF16),
                        pltpu.VMEM((rows, p), F32), pltpu.VMEM((rows, p), F32),
                        pltpu.VMEM((rows, p), F32), pltpu.VMEM((rows, p), F32)],
        compiler_params=pltpu.CompilerParams(dimension_semantics=("arbitrary",)),
        name="ssm",
    )(xg, kflat, mt, em, ar, ai, h0r, h0i, dvec)


def _out_kernel(a_ref, ys_ref, gs_ref, x_ref, gate_ref, wglu_ref, bglu_ref, woa_ref, wos_ref,
                gpost_ref, o_ref):
    y = ys_ref[...].astype(F32)
    y = 0.5 * y * (1.0 + jnp.tanh(math.sqrt(2.0 / math.pi) * (y + 0.044715 * (y * y * y))))
    z = jnp.dot(y.astype(BF16), wglu_ref[...], preferred_element_type=F32) + bglu_ref[...]
    y = y * jax.nn.sigmoid(z) * gs_ref[...].astype(F32)
    out = (jnp.dot(a_ref[...], woa_ref[...], preferred_element_type=F32)
           + jnp.dot(y.astype(BF16), wos_ref[...], preferred_element_type=F32))
    ms = jnp.mean(out * out, axis=-1, keepdims=True)
    normed = out * lax.rsqrt(ms + RMS_EPS) * gpost_ref[...]
    o_ref[...] = x_ref[...] + gate_ref[...] * normed


def _out_proj(a, ys, gs, x2d, gate, w_glu, b_glu, w_out, g_post, *, tm, rows_per_mod):
    n, d = x2d.shape
    if rows_per_mod is None:
        mod_spec = pl.BlockSpec((tm, d), lambda i: (i, 0))
    else:
        tiles_per_mod = rows_per_mod // tm
        mod_spec = pl.BlockSpec((None, 1, d), lambda i: (i // tiles_per_mod, 0, 0))

    def row_spec(width):
        return pl.BlockSpec((tm, width), lambda i: (i, 0))

    wglu = w_glu.astype(BF16)
    woa = w_out[:D_ATTN].astype(BF16)
    wos = w_out[D_ATTN:].astype(BF16)
    return pl.pallas_call(
        _out_kernel,
        out_shape=jax.ShapeDtypeStruct((n, d), F32),
        grid=(n // tm,),
        in_specs=[row_spec(D_ATTN), row_spec(D_SSM), row_spec(D_SSM), row_spec(d), mod_spec,
                  _const_spec(wglu.shape), _const_spec((1, D_SSM)), _const_spec(woa.shape),
                  _const_spec(wos.shape), _const_spec((1, d))],
        out_specs=row_spec(d),
        compiler_params=pltpu.CompilerParams(dimension_semantics=("arbitrary",),
                                             vmem_limit_bytes=V7X_VMEM_LIMIT_BYTES),
        name="out_proj",
    )(a, ys, gs, x2d, gate, wglu, b_glu.reshape(1, D_SSM), woa, wos, g_post.reshape(1, d))


def _scan_block(n_chunks):
    return SCAN_BLOCK if n_chunks % SCAN_BLOCK == 0 else 1


def _to_groups(u2d, n_seq, n_chunks, t_len):
    j = _scan_block(n_chunks)
    u6 = u2d.reshape(n_seq, n_chunks // j, j, t_len, N_SSM_GROUPS, SSM_GROUP)
    return u6.transpose(4, 2, 1, 0, 3, 5).reshape(N_SSM_GROUPS, n_seq * n_chunks, t_len * SSM_GROUP)


def _from_groups(yg, n_seq, n_chunks, t_len):
    j = _scan_block(n_chunks)
    y6 = yg.reshape(N_SSM_GROUPS, j, n_chunks // j, n_seq, t_len, SSM_GROUP)
    return y6.transpose(3, 2, 1, 4, 0, 5).reshape(n_seq * n_chunks * t_len, D_SSM)


def _pad_rows(a, rows):
    return jnp.pad(a, ((0, 0), (0, rows - a.shape[1]), (0, 0)))


def _layer(x, mod, past, g_pre, g_post, weights, prep, d_skip, w_glu, b_glu, w_out):
    bsz, s, d = x.shape
    n = bsz * s
    x2d = x.reshape(n, d)
    shift, scale, gate = mod[:, :d], mod[:, d:2 * d], mod[:, 2 * d:]
    if past is None:
        tm = 256
        rows_per_mod = s
        mods = [m.reshape(bsz, 1, d) for m in (scale, shift, gate)]
    else:
        tm = n
        rows_per_mod = None
        mods = [jnp.repeat(m, s, axis=0) for m in (scale, shift, gate)]
    scale_m, shift_m, gate_m = mods

    (q, k32, kb, v32, vb, iq, ik32, iklo, ikhi, iw, ga, u, gs) = _proj(
        x2d, scale_m, shift_m, g_pre.reshape(1, d), weights, tm=tm, rows_per_mod=rows_per_mod)

    def seq(a):
        return a.reshape(bsz, s, a.shape[-1])

    if past is None:
        attn = _dsa(seq(q), seq(iq), seq(iw), seq(ga), seq(kb), seq(vb), seq(iklo), seq(ikhi),
                    qb=Q_BLOCK, causal=True, n_keys=s)
        t_len = CHUNK if s % CHUNK == 0 else s
        h0r = jnp.zeros((N_SSM_GROUPS, bsz, SSM_STATE), F32)
        h0i = h0r
    else:
        ck, cv, cik, sre, sim = past
        past_len = ck.shape[1]
        n_keys = past_len + s
        lk = -(-n_keys // KEY_TILE) * KEY_TILE
        k_all = jnp.concatenate([ck.reshape(bsz, past_len, KV_DIM).astype(BF16), seq(kb)], axis=1)
        v_all = jnp.concatenate([cv.reshape(bsz, past_len, KV_DIM).astype(BF16), seq(vb)], axis=1)
        cik_b = cik.astype(BF16)
        zeros = jnp.zeros_like(cik_b)
        iklo_all = jnp.concatenate([jnp.concatenate([cik_b, zeros], axis=-1), seq(iklo)], axis=1)
        ikhi_all = jnp.concatenate([jnp.concatenate([zeros, cik_b], axis=-1), seq(ikhi)], axis=1)
        attn = _dsa(seq(q), seq(iq), seq(iw), seq(ga), _pad_rows(k_all, lk), _pad_rows(v_all, lk),
                    _pad_rows(iklo_all, lk), _pad_rows(ikhi_all, lk),
                    qb=s, causal=False, n_keys=n_keys)
        t_len = CHUNK if s % CHUNK == 0 else s
        h0r = jnp.swapaxes(sre.astype(F32), 0, 1)
        h0i = jnp.swapaxes(sim.astype(F32), 0, 1)

    n_chunks = s // t_len
    dvec = jnp.tile(d_skip.astype(F32).reshape(N_SSM_GROUPS, 1, SSM_GROUP), (1, 1, t_len))
    rows = bsz * n_chunks
    rows_pad = -(-rows // 16) * 16
    xg = _to_groups(u, bsz, n_chunks, t_len)
    xg = _pad_rows(xg, rows_pad) if rows_pad != rows else xg
    if rows_pad != rows:
        assert n_chunks == 1
        h0r = _pad_rows(h0r, rows_pad)
        h0i = _pad_rows(h0i, rows_pad)
    yg, hr, hi = _ssm(xg, prep, h0r, h0i, dvec, t_len=t_len,
                      n_seq=rows_pad // n_chunks, n_chunks=n_chunks)
    ys = _from_groups(yg[:, :rows], bsz, n_chunks, t_len)
    hr = jnp.swapaxes(hr[:, :bsz], 0, 1)
    hi = jnp.swapaxes(hi[:, :bsz], 0, 1)

    y = _out_proj(attn.reshape(n, D_ATTN), ys, gs, x2d, gate_m, w_glu, b_glu, w_out, g_post,
                  tm=tm, rows_per_mod=rows_per_mod)
    k_out = k32.reshape(bsz, s, N_KV_HEADS, HEAD_DIM)
    v_out = v32.reshape(bsz, s, N_KV_HEADS, HEAD_DIM)
    ik_out = ik32.reshape(bsz, s, IDX_DIM)
    return y.reshape(bsz, s, d), (k_out, v_out, ik_out, hr, hi)


def kernel(x_prompt, x_sample, c_prompt, c_sample, cache_k, cache_v, cache_idx_k, state_ssm_re,
           state_ssm_im, w_ada, b_ada, g_pre, g_post, w_in, lambda_re, lambda_im, log_dt, b_re, b_im,
           c_re, c_im, d_skip, w_glu, b_glu, w_out):
    depth = w_ada.shape[0]
    bp = c_prompt.shape[0]
    bs = c_sample.shape[0]
    yp, ys = x_prompt, x_sample
    outs_p = [[] for _ in range(5)]
    outs_s = [[] for _ in range(5)]
    c_all = jnp.concatenate([c_prompt, c_sample], axis=0)
    c_rows = -(-c_all.shape[0] // 8) * 8
    c_all = jnp.pad(c_all, ((0, c_rows - c_all.shape[0]), (0, 0)))
    for l in range(depth):
        mod = _adaln(c_all, w_ada[l], b_ada[l].reshape(1, -1))
        weights = _split_w_in(w_in[l])
        prep = _ssm_prep(lambda_re[l], lambda_im[l], log_dt[l], b_re[l], b_im[l], c_re[l], c_im[l])
        common = (g_pre[l], g_post[l], weights, prep, d_skip[l], w_glu[l], b_glu[l], w_out[l])
        yp, new_p = _layer(yp, mod[:bp], None, *common)
        past = (cache_k[l], cache_v[l], cache_idx_k[l], state_ssm_re[l], state_ssm_im[l])
        ys, new_s = _layer(ys, mod[bp:bp + bs], past, *common)
        for acc, val in zip(outs_p, new_p):
            acc.append(val)
        for acc, val in zip(outs_s, new_s):
            acc.append(val)
    return (yp, ys) + tuple(jnp.stack(a) for a in outs_p) + tuple(jnp.stack(a) for a in outs_s)
```

```python
import functools
import math

import jax
import jax.numpy as jnp
from jax import lax
from jax.experimental import pallas as pl
from jax.experimental.pallas import tpu as pltpu

F32 = jnp.float32
BF16 = jnp.bfloat16
I32 = jnp.int32

CHUNK = 64
Q_BLOCK = 128
D_ATTN = 1024
N_HEADS = 8
N_KV_HEADS = 2
HEAD_DIM = 128
KV_DIM = N_KV_HEADS * HEAD_DIM
HEADS_PER_KV = N_HEADS // N_KV_HEADS
N_IDX_HEADS = 16
IDX_DIM = 64
IDX_PAIRS = N_IDX_HEADS // 2
TOPK_MAX = 256
D_SSM = 1024
SSM_GROUP = 16
N_SSM_GROUPS = D_SSM // SSM_GROUP
SSM_STATE = 64
RMS_EPS = 1e-6
IN_SIZES = (D_ATTN, KV_DIM, KV_DIM, N_IDX_HEADS * IDX_DIM, IDX_DIM, N_IDX_HEADS, D_ATTN, D_SSM, D_SSM)

LANES = 128
V7X_VMEM_LIMIT_BYTES = 56 * 1024 * 1024

INT_MIN = -(2 ** 31)
MASKED_LOGIT = -1e30
MIN_ROW_SUM = 2.0 ** -100
KEY_TILE = 512
SCORE_KEYS = 256
SCAN_BLOCK = 8
SSM_GROUPS_PER_STEP = 8
COARSE_STEPS = 9
FINE_STEPS = 24


def _const_spec(shape):
    nd = len(shape)
    return pl.BlockSpec(shape, lambda *_: (0,) * nd, pipeline_mode=pl.Buffered(1))


def _dot_nt(a, b):
    return lax.dot_general(a, b, (((1,), (1,)), ((), ())), preferred_element_type=F32)


def _adaln_kernel(c_ref, w_ref, b_ref, o_ref):
    o_ref[...] = jnp.dot(c_ref[...], w_ref[...], preferred_element_type=F32) + b_ref[...]


def _adaln(c, w_ada, b_ada):
    rows, d = c.shape
    n_out = w_ada.shape[1]
    tn = 512
    return pl.pallas_call(
        _adaln_kernel,
        out_shape=jax.ShapeDtypeStruct((rows, n_out), F32),
        grid=(n_out // tn,),
        in_specs=[pl.BlockSpec((rows, d), lambda j: (0, 0)),
                  pl.BlockSpec((d, tn), lambda j: (0, j)),
                  pl.BlockSpec((1, tn), lambda j: (0, j))],
        out_specs=pl.BlockSpec((rows, tn), lambda j: (0, j)),
        compiler_params=pltpu.CompilerParams(dimension_semantics=("arbitrary",)),
        name="adaln",
    )(c, w_ada, b_ada)


def _proj_kernel(x_ref, scale_ref, shift_ref, g_ref,
                 wq_ref, wk_ref, wv_ref, wiq_ref, wnar_ref, wza_ref, wu_ref, wzs_ref,
                 q_ref, k_ref, kb_ref, v_ref, vb_ref, iq_ref, ik_ref, iklo_ref, ikhi_ref,
                 iw_ref, ga_ref, u_ref, gs_ref):
    x = x_ref[...]
    ms = jnp.mean(x * x, axis=-1, keepdims=True)
    y = x * lax.rsqrt(ms + RMS_EPS) * g_ref[...]
    h = (y * (1.0 + scale_ref[...]) + shift_ref[...]).astype(BF16)

    def mm(w_ref):
        return jnp.dot(h, w_ref[...], preferred_element_type=F32)

    q_ref[...] = (mm(wq_ref) * (HEAD_DIM ** -0.5 * math.log2(math.e))).astype(BF16)
    rows = x.shape[0]
    zk = mm(wk_ref)
    zv = mm(wv_ref)
    for hd in range(N_KV_HEADS):
        k_ref[pl.ds(hd, rows, stride=N_KV_HEADS), :] = zk[:, hd * HEAD_DIM:(hd + 1) * HEAD_DIM]
        v_ref[pl.ds(hd, rows, stride=N_KV_HEADS), :] = zv[:, hd * HEAD_DIM:(hd + 1) * HEAD_DIM]
    kb_ref[...] = zk.astype(BF16)
    vb_ref[...] = zv.astype(BF16)
    iq_ref[...] = (mm(wiq_ref) * (IDX_DIM ** -0.5)).astype(BF16)
    znar = mm(wnar_ref)
    zik = znar[:, :LANES]
    ik_ref[...] = zik[:, :IDX_DIM]
    lane = lax.broadcasted_iota(I32, zik.shape, 1)
    iklo_ref[...] = jnp.where(lane < IDX_DIM, zik, 0.0).astype(BF16)
    ikhi_ref[...] = jnp.where(lane >= IDX_DIM, zik, 0.0).astype(BF16)
    iw_ref[...] = znar[:, LANES:LANES + N_IDX_HEADS] * (N_IDX_HEADS ** -0.5)
    za = mm(wza_ref)
    ga_ref[...] = (za * jax.nn.sigmoid(za)).astype(BF16)
    u_ref[...] = mm(wu_ref).astype(BF16)
    zs = mm(wzs_ref)
    gs_ref[...] = (zs * jax.nn.sigmoid(zs)).astype(BF16)


def _split_w_in(w_in):
    offs = [0]
    for s in IN_SIZES:
        offs.append(offs[-1] + s)
    cols = [w_in[:, offs[i]:offs[i + 1]].astype(BF16) for i in range(len(IN_SIZES))]
    wq, wk, wv, wiq, wik, wiw, wza, wu, wzs = cols
    wnar = jnp.concatenate([wik, wik, jnp.pad(wiw, ((0, 0), (0, LANES - N_IDX_HEADS)))], axis=1)
    groups = (wq, wk, wv, wiq, wnar, wza, wu, wzs)
    order = sorted(range(len(groups)), key=lambda i: -groups[i].shape[1])
    blocks = [None] * len(groups)
    offset = 0
    for i in order:
        width = groups[i].shape[1]
        assert offset % width == 0
        blocks[i] = (width, offset // width)
        offset += width
    return jnp.concatenate([groups[i] for i in order], axis=1), tuple(blocks)


def _proj(x2d, scale, shift, g_pre, weights, *, tm, rows_per_mod):
    n, d = x2d.shape
    w_cat, blocks = weights
    if rows_per_mod is None:
        mod_spec = pl.BlockSpec((tm, d), lambda i: (i, 0))
    else:
        tiles_per_mod = rows_per_mod // tm
        mod_spec = pl.BlockSpec((None, 1, d), lambda i: (i // tiles_per_mod, 0, 0))

    def row_spec(width):
        return pl.BlockSpec((tm, width), lambda i: (i, 0))

    out_defs = [
        (1, D_ATTN, BF16), (N_KV_HEADS, HEAD_DIM, F32), (1, KV_DIM, BF16), (N_KV_HEADS, HEAD_DIM, F32),
        (1, KV_DIM, BF16), (1, N_IDX_HEADS * IDX_DIM, BF16), (1, IDX_DIM, F32), (1, LANES, BF16),
        (1, LANES, BF16), (1, N_IDX_HEADS, F32), (1, D_ATTN, BF16), (1, D_SSM, BF16), (1, D_SSM, BF16)]
    return pl.pallas_call(
        _proj_kernel,
        out_shape=[jax.ShapeDtypeStruct((r * n, w), dt) for r, w, dt in out_defs],
        grid=(n // tm,),
        in_specs=[row_spec(d), mod_spec, mod_spec, _const_spec((1, d))]
                 + [pl.BlockSpec((d, width), lambda i, j=j: (0, j), pipeline_mode=pl.Buffered(1))
                    for width, j in blocks],
        out_specs=[pl.BlockSpec((r * tm, w), lambda i: (i, 0)) for r, w, _ in out_defs],
        compiler_params=pltpu.CompilerParams(dimension_semantics=("arbitrary",),
                                             vmem_limit_bytes=V7X_VMEM_LIMIT_BYTES),
        name="in_proj",
    )(x2d, scale, shift, g_pre, *([w_cat] * len(blocks)))


def _dsa_kernel(q_ref, iq_ref, iw_ref, ga_ref, kb_ref, vb_ref, iklo_ref, ikhi_ref, o_ref,
                sc_ref, wb_ref, iqs_ref, qs_ref, m_ref, l_ref, acc_ref, kn_ref,
                *, qb, tk, topk, causal, n_keys, n_keys_pad):
    blk = pl.program_id(1)
    chunks_per_tile = tk // LANES
    row = lax.broadcasted_iota(I32, (qb, LANES), 0)
    if causal:
        n_vis = (blk + 1) * qb
        limit = (((blk * qb + row) // CHUNK) + 1) * CHUNK
        n_tiles = (n_vis + tk - 1) // tk
    else:
        limit = jnp.full((qb, LANES), n_keys, I32)
        n_tiles = (n_keys + tk - 1) // tk

    for j in range(IDX_PAIRS):
        iqs_ref[j * qb:(j + 1) * qb, :] = iq_ref[:, j * LANES:(j + 1) * LANES]
    for kv in range(N_KV_HEADS):
        for hh in range(HEADS_PER_KV):
            c0 = (kv * HEADS_PER_KV + hh) * HEAD_DIM
            qs_ref[kv, hh * qb:(hh + 1) * qb, :] = q_ref[:, c0:c0 + HEAD_DIM]
    iw = iw_ref[...]
    for h in range(N_IDX_HEADS):
        wb_ref[h] = jnp.broadcast_to(iw[:, h:h + 1], (qb, LANES))

    lane = lax.broadcasted_iota(I32, (qb, LANES), 1)

    def score_keys(first_chunk, n_matmuls, carry):
        m1, m2 = carry
        iqs = iqs_ref[...]
        for part in range(n_matmuls):
            chunk0 = first_chunk + part * (SCORE_KEYS // LANES)
            k0 = pl.multiple_of(chunk0 * LANES, SCORE_KEYS)
            s_lo = _dot_nt(iqs, iklo_ref[pl.ds(k0, SCORE_KEYS), :])
            s_hi = _dot_nt(iqs, ikhi_ref[pl.ds(k0, SCORE_KEYS), :])
            for c in range(SCORE_KEYS // LANES):
                cs = slice(c * LANES, (c + 1) * LANES)
                acc = None
                for j in range(IDX_PAIRS):
                    rs = slice(j * qb, (j + 1) * qb)
                    term = (jnp.maximum(s_lo[rs, cs], 0.0) * wb_ref[2 * j]
                            + jnp.maximum(s_hi[rs, cs], 0.0) * wb_ref[2 * j + 1])
                    acc = term if acc is None else acc + term
                ci = chunk0 + c
                visible = (ci * LANES + lane) < limit
                sc_ref[ci] = jnp.where(visible, acc, jnp.nan)
                v = jnp.where(visible, acc, -jnp.inf)
                m1, v = jnp.maximum(m1, v), jnp.minimum(m1, v)
                m2 = jnp.maximum(m2, v)
        return m1, m2

    def over_tile_pairs(step, carry):
        carry = lax.fori_loop(0, n_tiles // 2, lambda t, c: step(2 * t, 2, c), carry)
        return lax.cond(n_tiles % 2 == 1, lambda c: step(n_tiles - 1, 1, c), lambda c: c, carry)

    ninf = jnp.full((qb, LANES), -jnp.inf, F32)
    m1, m2 = over_tile_pairs(
        lambda t0, nt, c: score_keys(t0 * chunks_per_tile, nt * tk // SCORE_KEYS, c), (ninf, ninf))

    def lane_fold(x, op):
        return jnp.broadcast_to(op(x, axis=1, keepdims=True), (qb, LANES))

    def count_rows(pred):
        def count_tile(t, cnt):
            for c in range(chunks_per_tile):
                ci = t * chunks_per_tile + c
                cnt = cnt + jnp.where(pred(sc_ref[ci], ci * LANES + lane), 1.0, 0.0)
            return cnt

        return lane_fold(lax.fori_loop(0, n_tiles, count_tile, jnp.zeros((qb, LANES), F32)), jnp.sum)

    def key_bisection():
        def ordered_to_f32(key):
            return lax.bitcast_convert_type(jnp.where(key < 0, key ^ jnp.int32(0x7FFFFFFF), key), F32)

        def bisect(i, carry):
            lo, cnt_lo = carry
            cand_key = lo + lax.shift_left(jnp.int32(1), 31 - i)
            cand = ordered_to_f32(cand_key)
            total = count_rows(lambda s, _: s >= cand)
            take = total >= topk
            return jnp.where(take, cand_key, lo), jnp.where(take, total, cnt_lo)

        lo, cnt_lo = lax.fori_loop(0, 32, bisect, (jnp.full((qb, LANES), INT_MIN, I32),
                                                   jnp.zeros((qb, LANES), F32)))
        return ordered_to_f32(lo), cnt_lo

    few = limit < topk

    bracketed = limit >= 2 * LANES
    top = m2 if topk > LANES else m1
    lo_f = jnp.where(bracketed, lane_fold(m2, jnp.min), 0.0)
    hi_f = jnp.where(bracketed, lane_fold(top, jnp.max), 0.0)
    hi_f = hi_f + (jnp.abs(hi_f) * 2.0 ** -20 + 1e-30)

    cnt_hi = lane_fold(jnp.where(m1 >= hi_f, 1.0, 0.0), jnp.sum)

    def halve(_, carry):
        lo_f, hi_f, cnt_hi = carry
        mid = lo_f + (hi_f - lo_f) * 0.5
        total = count_rows(lambda s, _: s >= mid)
        take = total >= topk
        return jnp.where(take, mid, lo_f), jnp.where(take, hi_f, mid), jnp.where(take, cnt_hi, total)

    lo_f, hi_f, n_above = lax.fori_loop(0, COARSE_STEPS, halve, (lo_f, hi_f, cnt_hi))

    def bucket_rows(rs):
        lo_r, hi_r = lo_f[rs], hi_f[rs]

        def bucket_tile(t, carry):
            b1, b2, b3, pop = carry
            for c in range(chunks_per_tile):
                s = sc_ref[t * chunks_per_tile + c, rs, :]
                inb = (s >= lo_r) & jnp.logical_not(s >= hi_r)
                v = jnp.where(inb, s, -jnp.inf)
                pop = pop + jnp.where(inb, 1.0, 0.0)
                b1, v = jnp.maximum(b1, v), jnp.minimum(b1, v)
                b2, v = jnp.maximum(b2, v), jnp.minimum(b2, v)
                b3 = jnp.maximum(b3, v)
            return b1, b2, b3, pop

        return lax.fori_loop(0, n_tiles, bucket_tile, (ninf[rs], ninf[rs], ninf[rs], jnp.zeros_like(lo_r)))

    halves = [bucket_rows(slice(h * qb // 2, (h + 1) * qb // 2)) for h in range(2)]
    b1, b2, b3, pop = (jnp.concatenate([half[k] for half in halves], axis=0) for k in range(4))
    want = topk - n_above

    transposed = qb == LANES
    if transposed:
        cands = jnp.concatenate([b1.T, b2.T, b3.T], axis=0)
        want_q = want.T[:1]
        axis = 0
    else:
        cands = jnp.concatenate([b1, b2, b3], axis=1)
        want_q = want[:, :1]
        axis = 1

    def walk(_, carry):
        prev, covered, tau_q, cnt_q = carry
        cur = jnp.max(jnp.where(cands < prev, cands, -jnp.inf), axis=axis, keepdims=True)
        covered_new = covered + jnp.sum(jnp.where(cands == cur, 1.0, 0.0), axis=axis, keepdims=True)
        found = (covered < want_q) & (covered_new >= want_q) & (cur > -jnp.inf)
        return cur, covered_new, jnp.where(found, cur, tau_q), jnp.where(found, covered_new, cnt_q)

    n_walk = jnp.minimum(jnp.max(jnp.where(few, 0.0, want)), float(FINE_STEPS)).astype(I32)
    nan_q = jnp.full(want_q.shape, jnp.nan, F32)
    _, _, tau_q, cnt_q = lax.fori_loop(
        0, n_walk, walk,
        (jnp.full(want_q.shape, jnp.inf, F32), jnp.zeros(want_q.shape, F32), nan_q, nan_q))
    if transposed:
        tau_fast = jnp.broadcast_to(tau_q, (LANES, qb)).T
        cnt_fast = n_above + jnp.broadcast_to(cnt_q, (LANES, qb)).T
    else:
        tau_fast = jnp.broadcast_to(tau_q, (qb, LANES))
        cnt_fast = n_above + jnp.broadcast_to(cnt_q, (qb, LANES))

    bad = jnp.logical_not(few) & (jnp.logical_not(bracketed) | (pop > 3.0) | (want < 1.0)
                                  | jnp.logical_not(tau_fast == tau_fast))
    tau, cnt_tau = lax.cond(jnp.max(jnp.where(bad, 1.0, 0.0)) > 0.0, key_bisection,
                            lambda: (tau_fast, cnt_fast))
    tau = jnp.where(few, -jnp.inf, tau)
    excess = jnp.where(few, 0.0, cnt_tau - topk)

    @pl.when(jnp.max(excess) > 0)
    def _resolve_surplus():
        idx_bits = (n_keys_pad - 1).bit_length()

        def kept(s, kidx, vstar, jstar):
            return (s >= tau) & ((s > vstar) | ((s == vstar) & (kidx < jstar)))

        def drop_step(state):
            exc, vstar, jstar = state

            def min_tile(t, vm):
                for c in range(chunks_per_tile):
                    ci = t * chunks_per_tile + c
                    s = sc_ref[ci]
                    vm = jnp.minimum(vm, jnp.where(kept(s, ci * LANES + lane, vstar, jstar), s, jnp.inf))
                return vm

            vm = lax.fori_loop(0, n_tiles, min_tile, jnp.full((qb, LANES), jnp.inf, F32))
            vmin = jnp.broadcast_to(jnp.min(vm, axis=1, keepdims=True), (qb, LANES))

            def tied(s, kidx):
                return kept(s, kidx, vstar, jstar) & (s == vmin)

            cmin = count_rows(tied)
            active = exc > 0
            drop_all = active & (exc >= cmin)
            partial = active & (exc < cmin)
            keep = cmin - exc

            def idx_bisect(b, j0):
                cand_j = j0 + lax.shift_left(jnp.int32(1), idx_bits - 1 - b)
                below = count_rows(lambda s, kidx: tied(s, kidx) & (kidx < cand_j))
                return jnp.where(below < keep, cand_j, j0)

            j0 = lax.fori_loop(0, idx_bits, idx_bisect, jnp.zeros((qb, LANES), I32))
            vstar = jnp.where(active, vmin, vstar)
            jstar = jnp.where(drop_all, 0, jnp.where(partial, j0 + 1, jstar))
            exc = jnp.where(drop_all, exc - cmin, jnp.where(partial, 0, exc))
            return exc, vstar, jstar

        _, vstar, jstar = lax.while_loop(
            lambda state: jnp.max(state[0]) > 0, drop_step,
            (excess, tau, jnp.full((qb, LANES), 2 ** 30, I32)))

        def rewrite_tile(t, carry):
            for c in range(chunks_per_tile):
                ci = t * chunks_per_tile + c
                s = sc_ref[ci]
                dropped = (s >= tau) & jnp.logical_not(kept(s, ci * LANES + lane, vstar, jstar))
                sc_ref[ci] = jnp.where(dropped, jnp.nan, s)
            return carry

        lax.fori_loop(0, n_tiles, rewrite_tile, 0)

    def masked_logits(t0, nt, kv):
        ks = pl.multiple_of(t0 * tk, tk)
        bias = jnp.concatenate(
            [jnp.where(sc_ref[t0 * chunks_per_tile + c] >= tau, 0.0, MASKED_LOGIT)
             for c in range(nt * chunks_per_tile)], axis=1)
        s = _dot_nt(qs_ref[kv], kb_ref[pl.ds(ks, nt * tk), kv * HEAD_DIM:(kv + 1) * HEAD_DIM])
        return (s.reshape(HEADS_PER_KV, qb, nt * tk) + bias[None]).reshape(HEADS_PER_KV * qb, nt * tk)

    def attend_tiles(t0, nt, carry):
        ks = pl.multiple_of(t0 * tk, tk)
        for kv in range(N_KV_HEADS):
            s = masked_logits(t0, nt, kv)
            m = m_ref[kv]
            l = l_ref[kv]
            ps = []
            for c in range(nt * chunks_per_tile):
                e = jnp.exp2(s[:, c * LANES:(c + 1) * LANES] - m)
                l = l + e
                ps.append(e.astype(BF16))
            l_ref[kv] = l
            p = jnp.concatenate(ps, axis=1)
            acc_ref[kv] += jnp.dot(p, vb_ref[pl.ds(ks, nt * tk), kv * HEAD_DIM:(kv + 1) * HEAD_DIM],
                                   preferred_element_type=F32)
        return carry

    def attend():
        l_ref[...] = jnp.zeros(l_ref.shape, F32)
        acc_ref[...] = jnp.zeros(acc_ref.shape, F32)
        over_tile_pairs(attend_tiles, 0)
        for kv in range(N_KV_HEADS):
            l_ref[kv] = jnp.broadcast_to(jnp.sum(l_ref[kv], axis=1, keepdims=True), l_ref.shape[1:])

    @pl.when(blk == 0)
    def _key_norms():
        for kv in range(N_KV_HEADS):
            def norm_tile(t, mx):
                ks = pl.multiple_of(t * tk, tk)
                k = kb_ref[pl.ds(ks, tk), kv * HEAD_DIM:(kv + 1) * HEAD_DIM].astype(F32)
                return jnp.maximum(mx, jnp.sum(k * k, axis=1, keepdims=True))

            mx = lax.fori_loop(0, n_keys_pad // tk, norm_tile, jnp.zeros((tk, 1), F32))
            kn_ref[kv] = jnp.broadcast_to(jnp.sqrt(jnp.max(mx, axis=0, keepdims=True)), kn_ref.shape[1:])

    for kv in range(N_KV_HEADS):
        qf = qs_ref[kv].astype(F32)
        qn = jnp.sqrt(jnp.sum(qf * qf, axis=1, keepdims=True))
        m_ref[kv] = jnp.broadcast_to(qn, m_ref.shape[1:]) * kn_ref[kv, 0:1, :]
    attend()

    l_min = jnp.min(l_ref[...])

    @pl.when(jnp.logical_not(l_min >= MIN_ROW_SUM))
    def _exact_shift():
        m_ref[...] = jnp.full(m_ref.shape, MASKED_LOGIT, F32)

        def max_tile(t, carry):
            for kv in range(N_KV_HEADS):
                s = masked_logits(t, 1, kv)
                mx = m_ref[kv]
                for c in range(chunks_per_tile):
                    mx = jnp.maximum(mx, s[:, c * LANES:(c + 1) * LANES])
                m_ref[kv] = mx
            return carry

        lax.fori_loop(0, n_tiles, max_tile, 0)
        for kv in range(N_KV_HEADS):
            m_ref[kv] = jnp.broadcast_to(jnp.max(m_ref[kv], axis=1, keepdims=True), m_ref.shape[1:])
        attend()

    for kv in range(N_KV_HEADS):
        o = acc_ref[kv] / l_ref[kv]
        for hh in range(HEADS_PER_KV):
            c0 = (kv * HEADS_PER_KV + hh) * HEAD_DIM
            gate = ga_ref[:, c0:c0 + HEAD_DIM].astype(F32)
            o_ref[:, c0:c0 + HEAD_DIM] = (o[hh * qb:(hh + 1) * qb] * gate).astype(BF16)


def _dsa(q, iq, iw, ga, kb, vb, iklo, ikhi, *, qb, causal, n_keys):
    b, s, _ = q.shape
    lk = kb.shape[1]
    tk = KEY_TILE
    assert lk % tk == 0 and s % qb == 0
    topk = min(TOPK_MAX, n_keys // 4)
    kern = functools.partial(_dsa_kernel, qb=qb, tk=tk, topk=topk, causal=causal, n_keys=n_keys,
                             n_keys_pad=lk)

    def q_spec(width):
        return pl.BlockSpec((None, qb, width), lambda bi, i: (bi, i, 0))

    def kv_spec(width):
        return pl.BlockSpec((None, lk, width), lambda bi, i: (bi, 0, 0))

    return pl.pallas_call(
        kern,
        out_shape=jax.ShapeDtypeStruct((b, s, D_ATTN), BF16),
        grid=(b, s // qb),
        in_specs=[q_spec(D_ATTN), q_spec(N_IDX_HEADS * IDX_DIM), q_spec(N_IDX_HEADS), q_spec(D_ATTN),
                  kv_spec(KV_DIM), kv_spec(KV_DIM), kv_spec(LANES), kv_spec(LANES)],
        out_specs=q_spec(D_ATTN),
        scratch_shapes=[
            pltpu.VMEM((lk // LANES, qb, LANES), F32),
            pltpu.VMEM((N_IDX_HEADS, qb, LANES), F32),
            pltpu.VMEM((IDX_PAIRS * qb, LANES), BF16),
            pltpu.VMEM((N_KV_HEADS, HEADS_PER_KV * qb, HEAD_DIM), BF16),
            pltpu.VMEM((N_KV_HEADS, HEADS_PER_KV * qb, LANES), F32),
            pltpu.VMEM((N_KV_HEADS, HEADS_PER_KV * qb, LANES), F32),
            pltpu.VMEM((N_KV_HEADS, HEADS_PER_KV * qb, HEAD_DIM), F32),
            pltpu.VMEM((N_KV_HEADS, 8, LANES), F32),
        ],
        compiler_params=pltpu.CompilerParams(dimension_semantics=("arbitrary", "arbitrary"),
                                             vmem_limit_bytes=V7X_VMEM_LIMIT_BYTES),
        name="dsa",
    )(q, iq, iw, ga, kb, vb, iklo, ikhi)


def _ssm_prep_kernel(lre_c_ref, lim_c_ref, lre_r_ref, lim_r_ref, ldt_ref, btr_ref, bti_ref,
                     ctr_ref, cti_ref, kflat_ref, mt_ref, em_ref,
                     a64r_ref, a64i_ref, a16r_ref, a16i_ref):
    hp = lax.Precision.HIGHEST
    dt = jnp.exp(ldt_ref[...])
    n_lag = CHUNK
    width = n_lag * SSM_GROUP

    ldr_c = lre_c_ref[...] * dt
    ldi_c = lim_c_ref[...] * dt
    lag = lax.broadcasted_iota(I32, (1, n_lag), 1).astype(F32)
    mag = jnp.exp(ldr_c * lag)
    pr_lag = mag * jnp.cos(ldi_c * lag)
    pi_lag = mag * jnp.sin(ldi_c * lag)
    def spread(x, pick):
        total = None
        for _ in range(3):
            piece = x.astype(BF16)
            x = x - piece.astype(F32)
            part = jnp.dot(piece, pick, preferred_element_type=F32)
            total = part if total is None else total + part
        return total

    col = lax.broadcasted_iota(I32, (n_lag, width), 1)
    pick_lag = (col // SSM_GROUP == lax.broadcasted_iota(I32, (n_lag, width), 0)).astype(BF16)
    col = lax.broadcasted_iota(I32, (SSM_GROUP, width), 1)
    pick_ch = (col % SSM_GROUP == lax.broadcasted_iota(I32, (SSM_GROUP, width), 0)).astype(BF16)
    pr = spread(pr_lag, pick_lag)
    pi = spread(pi_lag, pick_lag)
    ctr = spread(ctr_ref[...], pick_ch)
    cti = spread(cti_ref[...], pick_ch)
    qr = pr * ctr - pi * cti
    qi = pr * cti + pi * ctr

    lre_r = lre_r_ref[...]
    lim_r = lim_r_ref[...]
    ldr_r = lre_r * dt
    ldi_r = lim_r * dt
    lbr = jnp.exp(ldr_r) * jnp.cos(ldi_r)
    lbi = jnp.exp(ldr_r) * jnp.sin(ldi_r)
    den = lre_r * lre_r + lim_r * lim_r
    nr = lbr - 1.0
    fr = (nr * lre_r + lbi * lim_r) / den
    fi = (lbi * lre_r - nr * lim_r) / den
    btr = btr_ref[...]
    bti = bti_ref[...]
    bbr = fr * btr - fi * bti
    bbi = fr * bti + fi * btr

    kflat_ref[...] = (jnp.dot(bbr, qr, precision=hp, preferred_element_type=F32)
                      - jnp.dot(bbi, qi, precision=hp, preferred_element_type=F32))

    lbr_c = jnp.exp(ldr_c) * jnp.cos(ldi_c)
    lbi_c = jnp.exp(ldr_c) * jnp.sin(ldi_c)
    n_state = lbr_c.shape[0]
    em_ref[:n_state, :] = (lbr_c * qr - lbi_c * qi).astype(BF16)
    em_ref[n_state:, :] = (-(lbr_c * qi + lbi_c * qr)).astype(BF16)

    back = (n_lag - 1 - lax.broadcasted_iota(I32, (n_lag, 1), 0)).astype(F32)
    bmag = jnp.exp(ldr_r * back)
    bpr = bmag * jnp.cos(ldi_r * back)
    bpi = bmag * jnp.sin(ldi_r * back)
    for t in range(n_lag):
        wr = bpr[t:t + 1, :]
        wi = bpi[t:t + 1, :]
        rows = slice(t * SSM_GROUP, (t + 1) * SSM_GROUP)
        mt_ref[rows, :] = jnp.concatenate([wr * bbr - wi * bbi, wr * bbi + wi * bbr], axis=1).astype(BF16)

    for steps, ar_ref, ai_ref in ((float(CHUNK), a64r_ref, a64i_ref), (16.0, a16r_ref, a16i_ref)):
        amag = jnp.exp(ldr_r * steps)
        ar_ref[...] = amag * jnp.cos(ldi_r * steps)
        ai_ref[...] = amag * jnp.sin(ldi_r * steps)


def _ssm_prep(lambda_re, lambda_im, log_dt, b_re, b_im, c_re, c_im):
    g, p = lambda_re.shape
    width = CHUNK * SSM_GROUP
    lre_c = lambda_re.reshape(g, p, 1)
    lim_c = lambda_im.reshape(g, p, 1)
    lre_r = lambda_re.reshape(g, 1, p)
    lim_r = lambda_im.reshape(g, 1, p)
    ldt = log_dt.reshape(g, 1, 1)
    btr = jnp.swapaxes(b_re, 1, 2)
    bti = jnp.swapaxes(b_im, 1, 2)
    ctr = jnp.swapaxes(c_re, 1, 2)
    cti = jnp.swapaxes(c_im, 1, 2)

    def gspec(*shape):
        return pl.BlockSpec((None,) + shape, lambda i: (i,) + (0,) * len(shape))

    out_defs = [((SSM_GROUP, width), F32), ((width, 2 * p), BF16), ((2 * p, width), BF16),
                ((1, p), F32), ((1, p), F32), ((1, p), F32), ((1, p), F32)]
    return pl.pallas_call(
        _ssm_prep_kernel,
        out_shape=[jax.ShapeDtypeStruct((g,) + s, dt) for s, dt in out_defs],
        grid=(g,),
        in_specs=[gspec(p, 1), gspec(p, 1), gspec(1, p), gspec(1, p), gspec(1, 1),
                  gspec(SSM_GROUP, p), gspec(SSM_GROUP, p), gspec(p, SSM_GROUP), gspec(p, SSM_GROUP)],
        out_specs=[gspec(*s) for s, _ in out_defs],
        compiler_params=pltpu.CompilerParams(dimension_semantics=("arbitrary",)),
        name="ssm_prep",
    )(lre_c, lim_c, lre_r, lim_r, ldt, btr, bti, ctr, cti)


def _ssm_group(x, kflat_ref, mt_ref, em_ref, ar_ref, ai_ref,
               h0r_ref, h0i_ref, dvec_ref, hr_ref, hi_ref,
               toep_ref, gr_ref, gi_ref, hpr_ref, hpi_ref, *, t_len, n_seq, n_chunks):
    width = t_len * SSM_GROUP
    per_vreg = LANES // SSM_GROUP
    kflat = kflat_ref[:, :width]
    kext = jnp.concatenate([kflat, jnp.zeros((SSM_GROUP, LANES), F32)], axis=1)
    shifted = [kext] + [pltpu.roll(kext, r * SSM_GROUP, axis=1) for r in range(1, per_vreg)]
    for t in range(t_len):
        q, r = divmod(t, per_vreg)
        blk = shifted[r][:, :width - q * LANES]
        if q:
            blk = jnp.concatenate([jnp.zeros((SSM_GROUP, q * LANES), F32), blk], axis=1)
        toep_ref[t * SSM_GROUP:(t + 1) * SSM_GROUP, :] = blk.astype(BF16)

    g = jnp.dot(x, mt_ref[...], preferred_element_type=F32)
    gr_ref[...] = g[:, :SSM_STATE]
    gi_ref[...] = g[:, SSM_STATE:]

    ar = ar_ref[...]
    ai = ai_ref[...]

    def advance(h_r, h_i, a_r, a_i, g_r, g_i):
        return a_r * h_r - a_i * h_i + g_r, a_r * h_i + a_i * h_r + g_i

    block = _scan_block(n_chunks)
    if block == 1:
        def chunk_step(n, carry):
            h_r, h_i = carry
            rows = pl.ds(n * n_seq, n_seq)
            hpr_ref[rows, :] = h_r
            hpi_ref[rows, :] = h_i
            return advance(h_r, h_i, ar, ai, gr_ref[rows, :], gi_ref[rows, :])

        if n_chunks == 1:
            h_r, h_i = chunk_step(0, (h0r_ref[...], h0i_ref[...]))
        else:
            h_r, h_i = lax.fori_loop(0, n_chunks, chunk_step, (h0r_ref[...], h0i_ref[...]))
    else:
        n_blocks = n_chunks // block
        slab = n_blocks * n_seq
        zero = jnp.zeros((slab, SSM_STATE), F32)
        l_r, l_i = zero, zero
        local = []
        for j in range(block):
            local.append((l_r, l_i))
            rows = slice(j * slab, (j + 1) * slab)
            l_r, l_i = advance(l_r, l_i, ar, ai, gr_ref[rows, :], gi_ref[rows, :])
        powers = [(jnp.ones_like(ar), jnp.zeros_like(ar))]
        for j in range(block):
            p_r, p_i = powers[-1]
            powers.append((p_r * ar - p_i * ai, p_r * ai + p_i * ar))
        ab_r, ab_i = powers[block]
        h_r, h_i = h0r_ref[...], h0i_ref[...]
        entry = []
        for b in range(n_blocks):
            entry.append((h_r, h_i))
            rows = slice(b * n_seq, (b + 1) * n_seq)
            h_r, h_i = advance(h_r, h_i, ab_r, ab_i, l_r[rows], l_i[rows])
        e_r = jnp.concatenate([e[0] for e in entry], axis=0)
        e_i = jnp.concatenate([e[1] for e in entry], axis=0)
        for j in range(block):
            p_r, p_i = powers[j]
            rows = slice(j * slab, (j + 1) * slab)
            hpr_ref[rows, :] = local[j][0] + p_r * e_r - p_i * e_i
            hpi_ref[rows, :] = local[j][1] + p_r * e_i + p_i * e_r
    hr_ref[...] = h_r
    hi_ref[...] = h_i

    y = jnp.dot(x, toep_ref[...], preferred_element_type=F32)
    hp = jnp.concatenate([hpr_ref[...], hpi_ref[...]], axis=1).astype(BF16)
    y = y + jnp.dot(hp, em_ref[:, :width], preferred_element_type=F32)
    return y + x.astype(F32) * dvec_ref[...]


def _ssm_kernel(x_ref, kflat_ref, mt_ref, em_ref, ar_ref, ai_ref,
                h0r_ref, h0i_ref, dvec_ref, y_ref, hr_ref, hi_ref,
                toep_ref, gr_ref, gi_ref, hpr_ref, hpi_ref, **static):
    for gi in range(x_ref.shape[0]):
        per_group = [r.at[gi] for r in (kflat_ref, mt_ref, em_ref, ar_ref, ai_ref, h0r_ref, h0i_ref,
                                        dvec_ref, hr_ref, hi_ref)]
        y = _ssm_group(x_ref[gi], *per_group, toep_ref, gr_ref, gi_ref, hpr_ref, hpi_ref, **static)
        y_ref[gi] = y.astype(BF16)


def _ssm(xg, prep, h0r, h0i, dvec, *, t_len, n_seq, n_chunks):
    kflat, mt, em, a64r, a64i, a16r, a16i = prep
    g, rows, width = xg.shape
    p = SSM_STATE
    full = CHUNK * SSM_GROUP
    if t_len == CHUNK:
        ar, ai = a64r, a64i
        mt_block = 0
    else:
        assert t_len == 16
        ar, ai = a16r, a16i
        mt_block = (full - width) // width
    kern = functools.partial(_ssm_kernel, t_len=t_len, n_seq=n_seq, n_chunks=n_chunks)
    per_step = SSM_GROUPS_PER_STEP if n_chunks == 1 else 1

    def gspec(*shape):
        return pl.BlockSpec((per_step,) + shape, lambda i: (i,) + (0,) * len(shape))

    mt_spec = pl.BlockSpec((per_step, width, 2 * p), lambda i: (i, mt_block, 0))
    return pl.pallas_call(
        kern,
        out_shape=[jax.ShapeDtypeStruct((g, rows, width), BF16),
                   jax.ShapeDtypeStruct((g, n_seq, p), F32),
                   jax.ShapeDtypeStruct((g, n_seq, p), F32)],
        grid=(g // per_step,),
        in_specs=[gspec(rows, width), gspec(SSM_GROUP, full), mt_spec,
                  gspec(2 * p, full), gspec(1, p), gspec(1, p),
                  gspec(n_seq, p), gspec(n_seq, p), gspec(1, width)],
        out_specs=[gspec(rows, width), gspec(n_seq, p), gspec(n_seq, p)],
        scratch_shapes=[pltpu.VMEM((width, width), BF16),
                        pltpu.VMEM((rows, p), F32), pltpu.VMEM((rows, p), F32),
                        pltpu.VMEM((rows, p), F32), pltpu.VMEM((rows, p), F32)],
        compiler_params=pltpu.CompilerParams(dimension_semantics=("arbitrary",)),
        name="ssm",
    )(xg, kflat, mt, em, ar, ai, h0r, h0i, dvec)


def _out_kernel(a_ref, ys_ref, gs_ref, x_ref, gate_ref, wglu_ref, bglu_ref, woa_ref, wos_ref,
                gpost_ref, o_ref):
    y = ys_ref[...].astype(F32)
    y = 0.5 * y * (1.0 + jnp.tanh(math.sqrt(2.0 / math.pi) * (y + 0.044715 * (y * y * y))))
    z = jnp.dot(y.astype(BF16), wglu_ref[...], preferred_element_type=F32) + bglu_ref[...]
    y = y * jax.nn.sigmoid(z) * gs_ref[...].astype(F32)
    out = (jnp.dot(a_ref[...], woa_ref[...], preferred_element_type=F32)
           + jnp.dot(y.astype(BF16), wos_ref[...], preferred_element_type=F32))
    ms = jnp.mean(out * out, axis=-1, keepdims=True)
    normed = out * lax.rsqrt(ms + RMS_EPS) * gpost_ref[...]
    o_ref[...] = x_ref[...] + gate_ref[...] * normed


def _out_proj(a, ys, gs, x2d, gate, w_glu, b_glu, w_out, g_post, *, tm, rows_per_mod):
    n, d = x2d.shape
    if rows_per_mod is None:
        mod_spec = pl.BlockSpec((tm, d), lambda i: (i, 0))
    else:
        tiles_per_mod = rows_per_mod // tm
        mod_spec = pl.BlockSpec((None, 1, d), lambda i: (i // tiles_per_mod, 0, 0))

    def row_spec(width):
        return pl.BlockSpec((tm, width), lambda i: (i, 0))

    wglu = w_glu.astype(BF16)
    woa = w_out[:D_ATTN].astype(BF16)
    wos = w_out[D_ATTN:].astype(BF16)
    return pl.pallas_call(
        _out_kernel,
        out_shape=jax.ShapeDtypeStruct((n, d), F32),
        grid=(n // tm,),
        in_specs=[row_spec(D_ATTN), row_spec(D_SSM), row_spec(D_SSM), row_spec(d), mod_spec,
                  _const_spec(wglu.shape), _const_spec((1, D_SSM)), _const_spec(woa.shape),
                  _const_spec(wos.shape), _const_spec((1, d))],
        out_specs=row_spec(d),
        compiler_params=pltpu.CompilerParams(dimension_semantics=("arbitrary",),
                                             vmem_limit_bytes=V7X_VMEM_LIMIT_BYTES),
        name="out_proj",
    )(a, ys, gs, x2d, gate, wglu, b_glu.reshape(1, D_SSM), woa, wos, g_post.reshape(1, d))


def _scan_block(n_chunks):
    return SCAN_BLOCK if n_chunks % SCAN_BLOCK == 0 else 1


def _to_groups(u2d, n_seq, n_chunks, t_len):
    j = _scan_block(n_chunks)
    u6 = u2d.reshape(n_seq, n_chunks // j, j, t_len, N_SSM_GROUPS, SSM_GROUP)
    return u6.transpose(4, 2, 1, 0, 3, 5).reshape(N_SSM_GROUPS, n_seq * n_chunks, t_len * SSM_GROUP)


def _from_groups(yg, n_seq, n_chunks, t_len):
    j = _scan_block(n_chunks)
    y6 = yg.reshape(N_SSM_GROUPS, j, n_chunks // j, n_seq, t_len, SSM_GROUP)
    return y6.transpose(3, 2, 1, 4, 0, 5).reshape(n_seq * n_chunks * t_len, D_SSM)


def _pad_rows(a, rows):
    return jnp.pad(a, ((0, 0), (0, rows - a.shape[1]), (0, 0)))


def _layer(x, mod, past, g_pre, g_post, weights, prep, d_skip, w_glu, b_glu, w_out):
    bsz, s, d = x.shape
    n = bsz * s
    x2d = x.reshape(n, d)
    shift, scale, gate = mod[:, :d], mod[:, d:2 * d], mod[:, 2 * d:]
    if past is None:
        tm = 256
        rows_per_mod = s
        mods = [m.reshape(bsz, 1, d) for m in (scale, shift, gate)]
    else:
        tm = n
        rows_per_mod = None
        mods = [jnp.repeat(m, s, axis=0) for m in (scale, shift, gate)]
    scale_m, shift_m, gate_m = mods

    (q, k32, kb, v32, vb, iq, ik32, iklo, ikhi, iw, ga, u, gs) = _proj(
        x2d, scale_m, shift_m, g_pre.reshape(1, d), weights, tm=tm, rows_per_mod=rows_per_mod)

    def seq(a):
        return a.reshape(bsz, s, a.shape[-1])

    if past is None:
        attn = _dsa(seq(q), seq(iq), seq(iw), seq(ga), seq(kb), seq(vb), seq(iklo), seq(ikhi),
                    qb=Q_BLOCK, causal=True, n_keys=s)
        t_len = CHUNK if s % CHUNK == 0 else s
        h0r = jnp.zeros((N_SSM_GROUPS, bsz, SSM_STATE), F32)
        h0i = h0r
    else:
        ck, cv, cik, sre, sim = past
        past_len = ck.shape[1]
        n_keys = past_len + s
        lk = -(-n_keys // KEY_TILE) * KEY_TILE
        k_all = jnp.concatenate([ck.reshape(bsz, past_len, KV_DIM).astype(BF16), seq(kb)], axis=1)
        v_all = jnp.concatenate([cv.reshape(bsz, past_len, KV_DIM).astype(BF16), seq(vb)], axis=1)
        cik_b = cik.astype(BF16)
        zeros = jnp.zeros_like(cik_b)
        iklo_all = jnp.concatenate([jnp.concatenate([cik_b, zeros], axis=-1), seq(iklo)], axis=1)
        ikhi_all = jnp.concatenate([jnp.concatenate([zeros, cik_b], axis=-1), seq(ikhi)], axis=1)
        attn = _dsa(seq(q), seq(iq), seq(iw), seq(ga), _pad_rows(k_all, lk), _pad_rows(v_all, lk),
                    _pad_rows(iklo_all, lk), _pad_rows(ikhi_all, lk),
                    qb=s, causal=False, n_keys=n_keys)
        t_len = CHUNK if s % CHUNK == 0 else s
        h0r = jnp.swapaxes(sre.astype(F32), 0, 1)
        h0i = jnp.swapaxes(sim.astype(F32), 0, 1)

    n_chunks = s // t_len
    dvec = jnp.tile(d_skip.astype(F32).reshape(N_SSM_GROUPS, 1, SSM_GROUP), (1, 1, t_len))
    rows = bsz * n_chunks
    rows_pad = -(-rows // 16) * 16
    xg = _to_groups(u, bsz, n_chunks, t_len)
    xg = _pad_rows(xg, rows_pad) if rows_pad != rows else xg
    if rows_pad != rows:
        assert n_chunks == 1
        h0r = _pad_rows(h0r, rows_pad)
        h0i = _pad_rows(h0i, rows_pad)
    yg, hr, hi = _ssm(xg, prep, h0r, h0i, dvec, t_len=t_len,
                      n_seq=rows_pad // n_chunks, n_chunks=n_chunks)
    ys = _from_groups(yg[:, :rows], bsz, n_chunks, t_len)
    hr = jnp.swapaxes(hr[:, :bsz], 0, 1)
    hi = jnp.swapaxes(hi[:, :bsz], 0, 1)

    y = _out_proj(attn.reshape(n, D_ATTN), ys, gs, x2d, gate_m, w_glu, b_glu, w_out, g_post,
                  tm=tm, rows_per_mod=rows_per_mod)
    k_out = k32.reshape(bsz, s, N_KV_HEADS, HEAD_DIM)
    v_out = v32.reshape(bsz, s, N_KV_HEADS, HEAD_DIM)
    ik_out = ik32.reshape(bsz, s, IDX_DIM)
    return y.reshape(bsz, s, d), (k_out, v_out, ik_out, hr, hi)


def kernel(x_prompt, x_sample, c_prompt, c_sample, cache_k, cache_v, cache_idx_k, state_ssm_re,
           state_ssm_im, w_ada, b_ada, g_pre, g_post, w_in, lambda_re, lambda_im, log_dt, b_re, b_im,
           c_re, c_im, d_skip, w_glu, b_glu, w_out):
    depth = w_ada.shape[0]
    bp = c_prompt.shape[0]
    bs = c_sample.shape[0]
    yp, ys = x_prompt, x_sample
    outs_p = [[] for _ in range(5)]
    outs_s = [[] for _ in range(5)]
    c_all = jnp.concatenate([c_prompt, c_sample], axis=0)
    c_rows = -(-c_all.shape[0] // 8) * 8
    c_all = jnp.pad(c_all, ((0, c_rows - c_all.shape[0]), (0, 0)))
    for l in range(depth):
        mod = _adaln(c_all, w_ada[l], b_ada[l].reshape(1, -1))
        weights = _split_w_in(w_in[l])
        prep = _ssm_prep(lambda_re[l], lambda_im[l], log_dt[l], b_re[l], b_im[l], c_re[l], c_im[l])
        common = (g_pre[l], g_post[l], weights, prep, d_skip[l], w_glu[l], b_glu[l], w_out[l])
        yp, new_p = _layer(yp, mod[:bp], None, *common)
        past = (cache_k[l], cache_v[l], cache_idx_k[l], state_ssm_re[l], state_ssm_im[l])
        ys, new_s = _layer(ys, mod[bp:bp + bs], past, *common)
        for acc, val in zip(outs_p, new_p):
            acc.append(val)
        for acc, val in zip(outs_s, new_s):
            acc.append(val)
    return (yp, ys) + tuple(jnp.stack(a) for a in outs_p) + tuple(jnp.stack(a) for a in outs_s)
```
